```python
import functools
import jax, jax.numpy as jnp
from jax import lax
import numpy as np

D_MODEL = 4096
BATCH = 1
SEQ = 8192
DEPTH = 1
DEC_BATCH = 32
DEC_SEQ = 8
PAST_LEN = 8192
PAGE_SIZE = 128

DIL_GROUPS = ((128, 1), (512, 4), (2048, 16))
N_GROUPS = 3
A_HEADS = D_MODEL // 512
HEAD_DIM = 128
A_WIDTH = A_HEADS * HEAD_DIM
BAND_BLOCK = 128
HG_HEADS = D_MODEL // 256
HG_KDIM = 128
HG_VDIM = 128
HG_FDIM = HG_HEADS * HG_KDIM
HG_WIDTH = HG_HEADS * HG_VDIM
HG_CHUNK = 32
FFN_HIDDEN = 4 * D_MODEL
PLE_DIM = 256
IN_SIZES = (N_GROUPS * A_WIDTH, N_GROUPS * A_WIDTH, N_GROUPS * A_WIDTH, HG_FDIM, HG_FDIM, HG_WIDTH, HG_WIDTH, D_MODEL, D_MODEL)
IN_WIDTH = sum(IN_SIZES)
NORM_EPS = 1e-6

kernel_name = 'hybrid_dilated_attn_hgrn2_decoder_step'


def rms_norm(x, g):
    xf = x.astype(jnp.float32)
    y = xf * lax.rsqrt(jnp.mean(xf * xf, axis=-1, keepdims=True) + NORM_EPS)
    return (y * g.astype(jnp.float32)).astype(x.dtype)


def _alibi_slopes():
    n = N_GROUPS * A_HEADS
    e = jnp.arange(1, n + 1, dtype=jnp.float32)
    return jnp.exp2(-8.0 * e / n).reshape(N_GROUPS, A_HEADS)


def _split_cols(z):
    out, start = [], 0
    for s in IN_SIZES:
        out.append(z[..., start:start + s])
        start += s
    return out


def _band_dilated_attention(q, k, v, window, dil, slopes):
    B, S, H, D = q.shape
    nk = window // dil
    L = S // dil
    nb = -(-L // BAND_BLOCK)
    Lp = nb * BAND_BLOCK

    def split(a):
        a = a.reshape(B, L, dil, H, D).transpose(0, 2, 1, 3, 4)
        a = jnp.pad(a, ((0, 0), (0, 0), (0, Lp - L), (0, 0), (0, 0)))
        return a.reshape(B, dil, nb, BAND_BLOCK, H, D)

    def with_prev(a):
        prev = jnp.pad(a, ((0, 0), (0, 0), (1, 0), (0, 0), (0, 0), (0, 0)))[:, :, :-1]
        return jnp.concatenate([prev, a], axis=3)

    qb = split(q)
    kk, vv = with_prev(split(k)), with_prev(split(v))
    logits = jnp.einsum('bznihd,bznjhd->bznhij', qb, kk).astype(jnp.float32) * (HEAD_DIM ** -0.5)
    i = jnp.arange(BAND_BLOCK)[:, None]
    j = jnp.arange(2 * BAND_BLOCK)[None, :]
    delta = BAND_BLOCK + i - j
    key_u = (jnp.arange(nb)[:, None, None] - 1) * BAND_BLOCK + j[None]
    valid = (delta >= 0)[None] & (delta <= nk)[None] & (key_u >= 0)
    bias = -slopes[:, None, None] * (delta * dil).astype(jnp.float32)[None]
    logits = jnp.where(valid[:, None], logits + bias, -jnp.inf)
    lse = jax.nn.logsumexp(logits, axis=-1)
    probs = jnp.exp(logits - lse[..., None])
    o = jnp.einsum('bznhij,bznjhd->bznihd', probs, vv.astype(jnp.float32))
    o = o.reshape(B, dil, Lp, H, D)[:, :, :L].transpose(0, 2, 1, 3, 4).reshape(B, S, H, D)
    lse = lse.transpose(0, 1, 2, 4, 3).reshape(B, dil, Lp, H)[:, :, :L].transpose(0, 2, 1, 3).reshape(B, S, H)
    return o, lse


def _merge_dilations(outs, lses, dtype):
    w = jax.nn.softmax(jnp.stack(lses, axis=0), axis=0)
    o = jnp.sum(w[..., None] * jnp.stack(outs, axis=0), axis=0)
    return o.reshape(o.shape[0], o.shape[1], A_WIDTH).astype(dtype)


def _attend_prompt(q, k, v, slopes):
    S = q.shape[1]
    outs, lses, rows = [], [], []
    for g, (window, dil) in enumerate(DIL_GROUPS):
        o, lse = _band_dilated_attention(q[:, :, g], k[:, :, g], v[:, :, g], window, dil, slopes[g])
        outs.append(o)
        lses.append(lse)
        L = min(window, S)
        rows.append(jnp.stack([k[:, S - L:, g], v[:, S - L:, g]], axis=2))
    return _merge_dilations(outs, lses, q.dtype), rows


def _attend_sample(q, k, v, caches, slopes):
    T = q.shape[1]
    outs, lses, rows = [], [], []
    for g, (window, dil) in enumerate(DIL_GROUPS):
        buf = caches[g]
        L = buf.shape[1]
        k_all = jnp.concatenate([buf[:, :, 0], k[:, :, g]], axis=1)
        v_all = jnp.concatenate([buf[:, :, 1], v[:, :, g]], axis=1)
        jj = jnp.arange(window // dil + 1)
        idx = (L + jnp.arange(T))[:, None] - dil * jj[None, :]
        valid = idx >= 0
        idx = jnp.maximum(idx, 0)
        ks, vs = k_all[:, idx], v_all[:, idx]
        logits = jnp.einsum('bthd,btjhd->bthj', q[:, :, g], ks).astype(jnp.float32) * (HEAD_DIM ** -0.5)
        bias = -slopes[g][:, None] * (dil * jj).astype(jnp.float32)[None, :]
        logits = jnp.where(valid[None, :, None, :], logits + bias, -jnp.inf)
        lse = jax.nn.logsumexp(logits, axis=-1)
        probs = jnp.exp(logits - lse[..., None])
        outs.append(jnp.einsum('bthj,btjhd->bthd', probs, vs.astype(jnp.float32)))
        lses.append(lse)
        rows.append(jnp.stack([k[:, :, g], v[:, :, g]], axis=2))
    return _merge_dilations(outs, lses, q.dtype), rows


def _gla_chunked(q, k, v, log_g, s0):
    B, T, H, K = q.shape
    V = v.shape[-1]
    n = -(-T // HG_CHUNK)
    pad = n * HG_CHUNK - T

    def prep(a):
        a = jnp.pad(a, ((0, 0), (0, pad), (0, 0), (0, 0)))
        return a.reshape(B, n, HG_CHUNK, H, a.shape[-1]).transpose(1, 0, 2, 3, 4)

    tri = jnp.tril(jnp.ones((HG_CHUNK, HG_CHUNK), dtype=bool))

    def step(S, inp):
        qc, kc, vc, gc = inp
        b = jnp.cumsum(gc, axis=1)
        q_dec = qc * jnp.exp(b)
        k_dec = kc * jnp.exp(-b)
        o = jnp.einsum('bchk,bhkv->bchv', q_dec, S)
        A = jnp.where(tri, jnp.einsum('bthk,bshk->bhts', q_dec, k_dec), 0.0)
        o = o + jnp.einsum('bhts,bshv->bthv', A, vc)
        b_last = b[:, -1]
        S = S * jnp.exp(b_last)[..., None] + jnp.einsum('bshk,bshv->bhkv', kc * jnp.exp(b_last[:, None] - b), vc)
        return S, o

    S, o = lax.scan(step, s0, (prep(q), prep(k), prep(v), prep(log_g)))
    o = o.transpose(1, 0, 2, 3, 4).reshape(B, n * HG_CHUNK, H, V)[:, :T]
    return o, S


def _hgrn2_branch(hq, hf, hi, hog, lb, g_out, s0):
    B, T, _ = hq.shape
    f32 = jnp.float32
    q = jax.nn.silu(hq.astype(f32)).reshape(B, T, HG_HEADS, HG_KDIM) * (HG_KDIM ** -0.5)
    f = hf.astype(f32)
    gate = lb + (1.0 - lb) * jax.nn.sigmoid(f)
    log_g = jnp.log(gate).reshape(B, T, HG_HEADS, HG_KDIM)
    k = ((1.0 - lb) * jax.nn.sigmoid(-f)).reshape(B, T, HG_HEADS, HG_KDIM)
    v = hi.astype(f32).reshape(B, T, HG_HEADS, HG_VDIM)
    o, s_new = _gla_chunked(q, k, v, log_g, s0.astype(f32))
    o = rms_norm(o, g_out) * jax.nn.sigmoid(hog.astype(f32).reshape(B, T, HG_HEADS, HG_VDIM))
    return o.reshape(B, T, HG_WIDTH).astype(hq.dtype), s_new.astype(s0.dtype)


def _layer(x, p, s0, attend, lb, g_mix, w_in, g_q, g_k, g_hg_out, w_up_attn, w_up_hgrn, w_out,
           g_ffn, w_ff_up, w_ff_down, w_ple, w_ple_gate):
    B, T, _ = x.shape
    n = rms_norm(x, g_mix)
    qa, ka, va, hq, hf, hi, hog, ga, gb = _split_cols(n @ w_in)
    qa = rms_norm(qa.reshape(B, T, N_GROUPS, A_HEADS, HEAD_DIM), g_q[:, None, :])
    ka = rms_norm(ka.reshape(B, T, N_GROUPS, A_HEADS, HEAD_DIM), g_k[:, None, :])
    va = va.reshape(B, T, N_GROUPS, A_HEADS, HEAD_DIM)
    o_attn, kv_rows = attend(qa, ka, va)
    o_hg, s_new = _hgrn2_branch(hq, hf, hi, hog, lb, g_hg_out, s0)
    merged = jax.nn.sigmoid(ga) * (o_attn @ w_up_attn) + jax.nn.sigmoid(gb) * (o_hg @ w_up_hgrn)
    x = x + merged @ w_out
    h = rms_norm(x, g_ffn)
    x = x + jnp.square(jax.nn.relu(h @ w_ff_up)) @ w_ff_down
    x = x + jax.nn.sigmoid(x @ w_ple_gate) * (p @ w_ple)
    return x, kv_rows, s_new


def setup_inputs(seed: int = 0) -> dict:
    key = jax.random.key(seed)
    ks = jax.random.split(key, 24)
    f32 = jnp.float32

    def nrm(k, shape, scale=1.0):
        return jax.random.normal(k, shape, f32) * scale

    def gain(k, shape):
        return 1.0 + 0.02 * jax.random.normal(k, shape, f32)

    buf_len = [min(w, PAST_LEN) for w, _ in DIL_GROUPS]
    return {
        'x_prompt': nrm(ks[0], (BATCH, SEQ, D_MODEL)),
        'x_sample': nrm(ks[1], (DEC_BATCH, DEC_SEQ, D_MODEL)),
        'cache_kv_w128': nrm(ks[2], (DEPTH, DEC_BATCH, buf_len[0], 2, A_HEADS, HEAD_DIM)),
        'cache_kv_w512': nrm(ks[3], (DEPTH, DEC_BATCH, buf_len[1], 2, A_HEADS, HEAD_DIM)),
        'cache_kv_w2048': nrm(ks[4], (DEPTH, DEC_BATCH, buf_len[2], 2, A_HEADS, HEAD_DIM)),
        'state_hgrn': nrm(ks[5], (DEPTH, DEC_BATCH, HG_HEADS, HG_KDIM, HG_VDIM), 0.5),
        'p_prompt': nrm(ks[6], (DEPTH, BATCH, SEQ, PLE_DIM)),
        'p_sample': nrm(ks[7], (DEPTH, DEC_BATCH, DEC_SEQ, PLE_DIM)),
        'g_mix': gain(ks[8], (DEPTH, D_MODEL)),
        'w_in': nrm(ks[9], (DEPTH, D_MODEL, IN_WIDTH), D_MODEL ** -0.5),
        'g_q': gain(ks[10], (DEPTH, N_GROUPS, HEAD_DIM)),
        'g_k': gain(ks[11], (DEPTH, N_GROUPS, HEAD_DIM)),
        'hg_lb_raw': nrm(ks[12], (DEPTH + 1, HG_FDIM), 0.1),
        'g_hg_out': gain(ks[13], (DEPTH, HG_VDIM)),
        'w_up_attn': nrm(ks[14], (DEPTH, A_WIDTH, D_MODEL), A_WIDTH ** -0.5),
        'w_up_hgrn': nrm(ks[15], (DEPTH, HG_WIDTH, D_MODEL), HG_WIDTH ** -0.5),
        'w_out': nrm(ks[16], (DEPTH, D_MODEL, D_MODEL), D_MODEL ** -0.5),
        'g_ffn': gain(ks[17], (DEPTH, D_MODEL)),
        'w_ff_up': nrm(ks[18], (DEPTH, D_MODEL, FFN_HIDDEN), D_MODEL ** -0.5),
        'w_ff_down': nrm(ks[19], (DEPTH, FFN_HIDDEN, D_MODEL), FFN_HIDDEN ** -0.5),
        'w_ple': nrm(ks[20], (DEPTH, PLE_DIM, D_MODEL), PLE_DIM ** -0.5),
        'w_ple_gate': nrm(ks[21], (DEPTH, D_MODEL, D_MODEL), D_MODEL ** -0.5),
    }


def reference(x_prompt, x_sample, cache_kv_w128, cache_kv_w512, cache_kv_w2048, state_hgrn,
              p_prompt, p_sample, g_mix, w_in, g_q, g_k, hg_lb_raw, g_hg_out, w_up_attn, w_up_hgrn,
              w_out, g_ffn, w_ff_up, w_ff_down, w_ple, w_ple_gate):
    slopes = _alibi_slopes()
    lb_all = jnp.cumsum(jax.nn.softmax(hg_lb_raw.astype(jnp.float32), axis=0), axis=0)
    yp, ys = x_prompt, x_sample
    kvp = ([], [], [])
    kvs = ([], [], [])
    stp, sts = [], []
    for i in range(DEPTH):
        lw = (lb_all[i], g_mix[i], w_in[i], g_q[i], g_k[i], g_hg_out[i], w_up_attn[i], w_up_hgrn[i],
              w_out[i], g_ffn[i], w_ff_up[i], w_ff_down[i], w_ple[i], w_ple_gate[i])
        s0p = jnp.zeros((x_prompt.shape[0], HG_HEADS, HG_KDIM, HG_VDIM), x_prompt.dtype)
        yp, rows_p, s_p = _layer(yp, p_prompt[i], s0p, functools.partial(_attend_prompt, slopes=slopes), *lw)
        caches = (cache_kv_w128[i], cache_kv_w512[i], cache_kv_w2048[i])
        ys, rows_s, s_s = _layer(ys, p_sample[i], state_hgrn[i],
                                 functools.partial(_attend_sample, caches=caches, slopes=slopes), *lw)
        for g in range(N_GROUPS):
            kvp[g].append(rows_p[g])
            kvs[g].append(rows_s[g])
        stp.append(s_p)
        sts.append(s_s)
    kv128_p, kv512_p, kv2048_p = jnp.stack(kvp[0]), jnp.stack(kvp[1]), jnp.stack(kvp[2])
    kv128_s, kv512_s, kv2048_s = jnp.stack(kvs[0]), jnp.stack(kvs[1]), jnp.stack(kvs[2])
    hg_p, hg_s = jnp.stack(stp), jnp.stack(sts)
    return (yp, ys, kv128_p, kv512_p, kv2048_p, hg_p, kv128_s, kv512_s, kv2048_s, hg_s)
```

```python
import functools

import jax
import jax.numpy as jnp
from jax import lax
from jax.experimental import pallas as pl
from jax.experimental.pallas import tpu as pltpu

F32 = jnp.float32
BF16 = jnp.bfloat16

D_MODEL = 4096
SEQ = 8192
DEC_BATCH = 32
DEC_SEQ = 8
N_SAMPLE = DEC_BATCH * DEC_SEQ
M_ALL = SEQ + N_SAMPLE
DIL_GROUPS = ((128, 1), (512, 4), (2048, 16))
N_GROUPS = 3
A_HEADS = 8
HEAD_DIM = 128
A_WIDTH = A_HEADS * HEAD_DIM
BAND = 128
HG_HEADS = 16
HG_KDIM = 128
HG_VDIM = 128
HG_WIDTH = HG_HEADS * HG_VDIM
HG_CHUNK = 32
FFN_HIDDEN = 4 * D_MODEL
PLE_DIM = 256
NORM_EPS = 1e-6
ATTN_SCALE = HEAD_DIM ** -0.5

COL_Q = 0
COL_K = COL_Q + N_GROUPS * A_WIDTH
COL_V = COL_K + N_GROUPS * A_WIDTH
COL_HQ = COL_V + N_GROUPS * A_WIDTH
COL_HF = COL_HQ + HG_HEADS * HG_KDIM
COL_HI = COL_HF + HG_HEADS * HG_KDIM
COL_HOG = COL_HI + HG_WIDTH
COL_GA = COL_HOG + HG_WIDTH
COL_GB = COL_GA + D_MODEL
IN_WIDTH = COL_GB + D_MODEL

VMEM_LIMIT = 56 * 1024 * 1024


def _dot(a, b):
    return jnp.dot(a, b, preferred_element_type=F32)


def _dot_nt(a, b):
    return lax.dot_general(a, b, (((1,), (1,)), ((), ())), preferred_element_type=F32)


def _dot_tn(a, b):
    return lax.dot_general(a, b, (((0,), (0,)), ((), ())), preferred_element_type=F32)


def _rms(x, g):
    return x * lax.rsqrt(jnp.mean(x * x, axis=-1, keepdims=True) + NORM_EPS) * g


def _sigmoid(x):
    return 1.0 / (1.0 + jnp.exp(-x))


def _mm_body_fullk(*refs, n_extra, n_out, epilogue, k_chunk):
    a_ref, w_ref = refs[:2]
    extra = refs[2:2 + n_extra]
    outs = refs[2 + n_extra:2 + n_extra + n_out]
    wb_ref, = refs[2 + n_extra + n_out:]
    kdim = w_ref.shape[0]

    @pl.when(pl.program_id(1) == 0)
    def _():
        def cast(c, carry):
            r0 = pl.multiple_of(c * k_chunk, k_chunk)
            wb_ref[pl.ds(r0, k_chunk), :] = w_ref[pl.ds(r0, k_chunk), :].astype(BF16)
            return carry
        lax.fori_loop(0, kdim // k_chunk, cast, 0)

    acc = _dot(a_ref[...].astype(BF16), wb_ref[...])
    res = epilogue(acc, *extra)
    for o_ref, r in zip(outs, res):
        o_ref[...] = r.astype(o_ref.dtype)


def _mm_body_ktiled(*refs, n_extra, n_out, epilogue, nk):
    a_ref, w_ref = refs[:2]
    extra = refs[2:2 + n_extra]
    outs = refs[2 + n_extra:2 + n_extra + n_out]
    acc_ref, = refs[2 + n_extra + n_out:]
    k = pl.program_id(2)
    part = _dot(a_ref[...].astype(BF16), w_ref[...].astype(BF16))

    @pl.when(k == 0)
    def _():
        acc_ref[...] = part

    @pl.when(k > 0)
    def _():
        acc_ref[...] += part

    @pl.when(k == nk - 1)
    def _():
        res = epilogue(acc_ref[...], *extra)
        for o_ref, r in zip(outs, res):
            o_ref[...] = r.astype(o_ref.dtype)


def _matmul(a, w, *, bm, bn, bk=None, a_col0=0, k_total=None, extras=(), extra_specs=(),
            out_dtypes=(F32,), epilogue=None, name="mm"):
    m = a.shape[0]
    kdim, n = w.shape
    if k_total is None:
        k_total = kdim
    assert kdim == k_total
    if epilogue is None:
        epilogue = lambda acc: (acc,)
    n_out = len(out_dtypes)
    out_shape = tuple(jax.ShapeDtypeStruct((m, n), dt) for dt in out_dtypes)
    assert m % bm == 0 and n % bn == 0
    if bk is None or bk == kdim:
        assert a_col0 % kdim == 0
        ac = a_col0 // kdim
        grid = (n // bn, m // bm)
        in_specs = [pl.BlockSpec((bm, kdim), lambda j, i: (i, ac)),
                    pl.BlockSpec((kdim, bn), lambda j, i: (0, j))] + list(extra_specs)
        out_specs = tuple(pl.BlockSpec((bm, bn), lambda j, i: (i, j)) for _ in out_dtypes)
        body = functools.partial(_mm_body_fullk, n_extra=len(extras), n_out=n_out, epilogue=epilogue,
                                 k_chunk=min(512, kdim))
        scratch = [pltpu.VMEM((kdim, bn), BF16)]
        sem = ("arbitrary", "arbitrary")
    else:
        assert kdim % bk == 0 and a_col0 % bk == 0
        ac = a_col0 // bk
        nk = kdim // bk
        grid = (n // bn, m // bm, nk)
        in_specs = [pl.BlockSpec((bm, bk), lambda j, i, k: (i, ac + k)),
                    pl.BlockSpec((bk, bn), lambda j, i, k: (k, j))] + list(extra_specs)
        out_specs = tuple(pl.BlockSpec((bm, bn), lambda j, i, k: (i, j)) for _ in out_dtypes)
        body = functools.partial(_mm_body_ktiled, n_extra=len(extras), n_out=n_out, epilogue=epilogue, nk=nk)
        scratch = [pltpu.VMEM((bm, bn), F32)]
        sem = ("arbitrary", "arbitrary", "arbitrary")
    res = pl.pallas_call(
        body,
        grid=grid,
        in_specs=in_specs,
        out_specs=out_specs,
        out_shape=out_shape,
        scratch_shapes=scratch,
        compiler_params=pltpu.CompilerParams(dimension_semantics=sem, vmem_limit_bytes=VMEM_LIMIT),
        name=name,
    )(a, w, *extras)
    return res


def _rmsnorm_body(x_ref, g_ref, o_ref):
    o_ref[...] = _rms(x_ref[...], g_ref[...]).astype(o_ref.dtype)


def _rmsnorm_rows(x, g, *, bm=256, name="rmsnorm"):
    m, d = x.shape
    return pl.pallas_call(
        _rmsnorm_body,
        grid=(m // bm,),
        in_specs=[pl.BlockSpec((bm, d), lambda i: (i, 0)), pl.BlockSpec((1, d), lambda i: (0, 0))],
        out_specs=pl.BlockSpec((bm, d), lambda i: (i, 0)),
        out_shape=jax.ShapeDtypeStruct((m, d), BF16),
        compiler_params=pltpu.CompilerParams(dimension_semantics=("arbitrary",)),
        name=name,
    )(x, g.reshape(1, d))


def _ld_rows(ref, start, dil):
    if dil == 1:
        return ref[pl.ds(start, BAND), :]
    return ref[pl.ds(start, BAND, stride=dil), :]


def _st_rows(ref, g, start, dil, val):
    if dil == 1:
        ref[g, pl.ds(start, BAND), :] = val
    else:
        ref[g, pl.ds(start, BAND, stride=dil), :] = val


def _attn_prompt_body(slope_ref, gq_ref, gk_ref,
                      q0, k0, v0, kp0, vp0, q1, k1, v1, kp1, vp1, q2, k2, v2, kp2, vp2,
                      o_ref, kn0, kn1, kn2,
                      qn_s, kn_s, vv_s, og_s, lse_s, *, sb, dils):
    first_block = pl.program_id(1) == 0
    groups = ((q0, k0, v0, kp0, vp0, kn0), (q1, k1, v1, kp1, vp1, kn1), (q2, k2, v2, kp2, vp2, kn2))
    ii = lax.broadcasted_iota(jnp.int32, (BAND, BAND), 0)
    jj = lax.broadcasted_iota(jnp.int32, (BAND, BAND), 1)
    dist_cur = (ii - jj).astype(F32)
    dist_prev = (BAND + ii - jj).astype(F32)
    ok_cur = jj <= ii
    ok_prev = jj >= ii
    neg_inf = jnp.float32(-jnp.inf)
    base = kn_s.shape[0] - sb

    for g, dil in enumerate(dils):
        q_ref, k_ref, v_ref, kp_ref, vp_ref, kn_out = groups[g]
        pr = BAND * dil
        gq = gq_ref[g:g + 1, :]
        gk = gk_ref[g:g + 1, :]
        slope = slope_ref[0, :, g * HEAD_DIM:(g + 1) * HEAD_DIM]
        qn_s[...] = _rms(q_ref[...], gq)
        kn = _rms(k_ref[...], gk)
        kn_out[...] = kn
        kn_s[pl.ds(base, sb), :] = kn
        kn_s[pl.ds(base - pr, pr), :] = _rms(kp_ref[...], gk)
        vv_s[pl.ds(base, sb), :] = v_ref[...]
        vv_s[pl.ds(base - pr, pr), :] = vp_ref[...]
        bias_cur = jnp.where(ok_cur, -slope * (dist_cur * dil), neg_inf)
        bias_prev = jnp.where(ok_prev, -slope * (dist_prev * dil), neg_inf)
        bias_prev_first = jnp.where(first_block, neg_inf, bias_prev)
        for r in range(dil):
            for s in range(sb // pr):
                c0 = s * pr + r
                qs = _ld_rows(qn_s, c0, dil).astype(BF16)
                kc = _ld_rows(kn_s, base + c0, dil).astype(BF16)
                kp = _ld_rows(kn_s, base + c0 - pr, dil).astype(BF16)
                vc = _ld_rows(vv_s, base + c0, dil).astype(BF16)
                vp = _ld_rows(vv_s, base + c0 - pr, dil).astype(BF16)
                lc = _dot_nt(qs, kc) * ATTN_SCALE + bias_cur
                lp = _dot_nt(qs, kp) * ATTN_SCALE + (bias_prev_first if s == 0 else bias_prev)
                mx = jnp.maximum(jnp.max(lc, axis=-1, keepdims=True), jnp.max(lp, axis=-1, keepdims=True))
                pc = jnp.exp(lc - mx)
                pp = jnp.exp(lp - mx)
                ssum = jnp.sum(pc, axis=-1, keepdims=True) + jnp.sum(pp, axis=-1, keepdims=True)
                o = (_dot(pc.astype(BF16), vc) + _dot(pp.astype(BF16), vp)) / ssum
                lse = mx + jnp.log(ssum)
                _st_rows(og_s, g, c0, dil, o)
                _st_rows(lse_s, g, c0, dil, jnp.broadcast_to(lse, (BAND, HEAD_DIM)))

    l0, l1, l2 = lse_s[0], lse_s[1], lse_s[2]
    mx = jnp.maximum(jnp.maximum(l0, l1), l2)
    w0, w1, w2 = jnp.exp(l0 - mx), jnp.exp(l1 - mx), jnp.exp(l2 - mx)
    o = (w0 * og_s[0] + w1 * og_s[1] + w2 * og_s[2]) / (w0 + w1 + w2)
    o_ref[...] = o.astype(o_ref.dtype)


def _alibi_slopes():
    n = N_GROUPS * A_HEADS
    e = jnp.arange(1, n + 1, dtype=F32)
    return jnp.exp2(-8.0 * e / n).reshape(N_GROUPS, A_HEADS)


def _attn_prompt(z, g_q, g_k, *, seq, sb, dils, n_heads=A_HEADS, col_q=COL_Q, col_k=COL_K, col_v=COL_V):
    nb = seq // sb
    slopes = _alibi_slopes()
    slope_arr = jnp.broadcast_to(slopes.T[:, None, :, None], (n_heads, 1, N_GROUPS, HEAD_DIM))
    slope_arr = slope_arr.reshape(n_heads, 1, N_GROUPS * HEAD_DIM)
    a_width = n_heads * HEAD_DIM
    in_specs = [pl.BlockSpec((1, 1, N_GROUPS * HEAD_DIM), lambda h, i: (h, 0, 0)),
                pl.BlockSpec((N_GROUPS, HEAD_DIM), lambda h, i: (0, 0)),
                pl.BlockSpec((N_GROUPS, HEAD_DIM), lambda h, i: (0, 0))]
    operands = [slope_arr, g_q, g_k]
    max_pr = BAND * max(dils)
    for g, dil in enumerate(dils):
        pr = BAND * dil
        ratio = sb // pr
        cq = (col_q + g * a_width) // HEAD_DIM
        ck = (col_k + g * a_width) // HEAD_DIM
        cv = (col_v + g * a_width) // HEAD_DIM
        cur = lambda c: pl.BlockSpec((sb, HEAD_DIM), lambda h, i, c=c: (i, c + h))
        prev = lambda c: pl.BlockSpec((pr, HEAD_DIM),
                                      lambda h, i, c=c, ratio=ratio: (jnp.maximum(i * ratio - 1, 0), c + h))
        in_specs += [cur(cq), cur(ck), cur(cv), prev(ck), prev(cv)]
        operands += [z, z, z, z, z]
    out_block = pl.BlockSpec((sb, HEAD_DIM), lambda h, i: (i, h))
    out_shape = (jax.ShapeDtypeStruct((seq, a_width), BF16),) + tuple(
        jax.ShapeDtypeStruct((seq, a_width), F32) for _ in dils)
    body = functools.partial(_attn_prompt_body, sb=sb, dils=tuple(dils))
    return pl.pallas_call(
        body,
        grid=(n_heads, nb),
        in_specs=in_specs,
        out_specs=(out_block,) * 4,
        out_shape=out_shape,
        scratch_shapes=[pltpu.VMEM((sb, HEAD_DIM), F32),
                        pltpu.VMEM((sb + max_pr, HEAD_DIM), F32),
                        pltpu.VMEM((sb + max_pr, HEAD_DIM), F32),
                        pltpu.VMEM((N_GROUPS, sb, HEAD_DIM), F32),
                        pltpu.VMEM((N_GROUPS, sb, HEAD_DIM), F32)],
        compiler_params=pltpu.CompilerParams(dimension_semantics=("arbitrary", "arbitrary"),
                                             vmem_limit_bytes=VMEM_LIMIT),
        name="attn_prompt",
    )(*operands)


def _attn_sample_body(gq_ref, gk_ref, q_ref, k_ref, v_ref,
                      kc0, vc0, kc1, vc1, kc2, vc2, o_ref, kn_ref, *, t_new, groups, n_heads):
    caches = ((kc0, vc0), (kc1, vc1), (kc2, vc2))
    neg_inf = jnp.float32(-jnp.inf)
    n_softmax_heads = len(groups) * n_heads
    a_width = n_heads * HEAD_DIM
    tt = lax.broadcasted_iota(jnp.int32, (t_new, t_new), 0)
    ss = lax.broadcasted_iota(jnp.int32, (t_new, t_new), 1)
    outs = [[None] * len(groups) for _ in range(n_heads)]
    lses = [[None] * len(groups) for _ in range(n_heads)]
    for g, (window, dil) in enumerate(groups):
        kc_ref, vc_ref = caches[g]
        length = kc_ref.shape[1]
        gq = gq_ref[g:g + 1, :]
        gk = gk_ref[g:g + 1, :]
        tq = lax.broadcasted_iota(jnp.int32, (t_new, length), 0)
        sk = lax.broadcasted_iota(jnp.int32, (t_new, length), 1)
        dist_c = length + tq - sk
        ok_c = ((dist_c & (dil - 1)) == 0) & (dist_c <= window)
        dist_cf = dist_c.astype(F32)
        dist_n = tt - ss
        ok_n = (dist_n >= 0) & ((dist_n & (dil - 1)) == 0)
        dist_nf = dist_n.astype(F32)
        for h in range(n_heads):
            slope = 2.0 ** (-8.0 * (g * n_heads + h + 1) / n_softmax_heads)
            c0 = g * a_width + h * HEAD_DIM
            q = _rms(q_ref[:, c0:c0 + HEAD_DIM], gq).astype(BF16)
            kn = _rms(k_ref[:, c0:c0 + HEAD_DIM], gk)
            kn_ref[:, c0:c0 + HEAD_DIM] = kn
            vn = v_ref[:, c0:c0 + HEAD_DIM].astype(BF16)
            kcache = kc_ref[0, :, h * HEAD_DIM:(h + 1) * HEAD_DIM].astype(BF16)
            vcache = vc_ref[0, :, h * HEAD_DIM:(h + 1) * HEAD_DIM].astype(BF16)
            lc = jnp.where(ok_c, _dot_nt(q, kcache) * ATTN_SCALE - slope * dist_cf, neg_inf)
            ln = jnp.where(ok_n, _dot_nt(q, kn.astype(BF16)) * ATTN_SCALE - slope * dist_nf, neg_inf)
            mx = jnp.maximum(jnp.max(lc, axis=-1, keepdims=True), jnp.max(ln, axis=-1, keepdims=True))
            pc = jnp.exp(lc - mx)
            pn = jnp.exp(ln - mx)
            ssum = jnp.sum(pc, axis=-1, keepdims=True) + jnp.sum(pn, axis=-1, keepdims=True)
            outs[h][g] = (_dot(pc.astype(BF16), vcache) + _dot(pn.astype(BF16), vn)) / ssum
            lses[h][g] = mx + jnp.log(ssum)
    for h in range(n_heads):
        mx = functools.reduce(jnp.maximum, lses[h])
        ws = [jnp.exp(l - mx) for l in lses[h]]
        num = functools.reduce(lambda a, b: a + b, [w * o for w, o in zip(ws, outs[h])])
        o_ref[:, h * HEAD_DIM:(h + 1) * HEAD_DIM] = num / functools.reduce(lambda a, b: a + b, ws)


def _attn_sample(qs, ks, vs, caches, g_q, g_k, *, n_batch, t_new, groups, n_heads=A_HEADS):
    a_width = n_heads * HEAD_DIM
    gw = len(groups) * a_width
    row = lambda: pl.BlockSpec((t_new, gw), lambda b: (b, 0))
    in_specs = [pl.BlockSpec((len(groups), HEAD_DIM), lambda b: (0, 0)),
                pl.BlockSpec((len(groups), HEAD_DIM), lambda b: (0, 0)),
                row(), row(), row()]
    operands = [g_q, g_k, qs, ks, vs]
    for c in caches:
        length = c.shape[1]
        in_specs += [pl.BlockSpec((1, length, a_width), lambda b: (b, 0, 0)),
                     pl.BlockSpec((1, length, a_width), lambda b: (b, 0, 1))]
        operands += [c, c]
    body = functools.partial(_attn_sample_body, t_new=t_new, groups=tuple(groups), n_heads=n_heads)
    return pl.pallas_call(
        body,
        grid=(n_batch,),
        in_specs=in_specs,
        out_specs=(pl.BlockSpec((t_new, a_width), lambda b: (b, 0)),
                   pl.BlockSpec((t_new, gw), lambda b: (b, 0))),
        out_shape=(jax.ShapeDtypeStruct((n_batch * t_new, a_width), F32),
                   jax.ShapeDtypeStruct((n_batch * t_new, gw), F32)),
        compiler_params=pltpu.CompilerParams(dimension_semantics=("arbitrary",),
                                             vmem_limit_bytes=VMEM_LIMIT),
        name="attn_sample",
    )(*operands)


def _split3(x):
    hi = x.astype(BF16)
    r1 = x - hi.astype(F32)
    mid = r1.astype(BF16)
    lo = (r1 - mid.astype(F32)).astype(BF16)
    return hi, mid, lo


def _hgrn_block(hq, hf, hi, hog, lb, g_out, st, *, tb, n_valid=None):
    c = HG_CHUNK
    q = hq * _sigmoid(hq) * (HG_KDIM ** -0.5)
    gate = lb + (1.0 - lb) * _sigmoid(hf)
    log_g = jnp.log(gate)
    k = (1.0 - lb) * _sigmoid(-hf)
    if n_valid is not None:
        rows = lax.broadcasted_iota(jnp.int32, (tb, HG_KDIM), 0)
        log_g = jnp.where(rows < n_valid, log_g, 0.0)
        k = jnp.where(rows < n_valid, k, 0.0)
        q = jnp.where(rows < n_valid, q, 0.0)
    row = lax.broadcasted_iota(jnp.int32, (tb, tb), 0)
    col = lax.broadcasted_iota(jnp.int32, (tb, tb), 1)
    same = (row // c) == (col // c)
    tri_ok = same & (col <= row)
    tri = jnp.where(tri_ok, 1.0, 0.0).astype(BF16)
    blk = jnp.where(same, 1.0, 0.0).astype(BF16)
    p_hi, p_mid, p_lo = _split3(log_g)
    b = _dot(tri, p_hi) + _dot(tri, p_mid) + _dot(tri, p_lo)
    b_last = _dot(blk, p_hi) + _dot(blk, p_mid) + _dot(blk, p_lo)
    q_dec = (q * jnp.exp(b)).astype(BF16)
    k_dec = (k * jnp.exp(-b)).astype(BF16)
    k_end = (k * jnp.exp(b_last - b)).astype(BF16)
    decay = jnp.exp(b_last)
    vb = hi.astype(BF16)
    a = jnp.where(tri_ok, _dot_nt(q_dec, k_dec), 0.0)
    o = _dot(a.astype(BF16), vb)
    parts = []
    for ci in range(tb // c):
        sl = slice(ci * c, (ci + 1) * c)
        parts.append(_dot_nt(q_dec[sl], st.astype(BF16)))
        st = st * decay[ci * c:ci * c + 1, :] + _dot_tn(vb[sl], k_end[sl])
    o = o + (jnp.concatenate(parts, axis=0) if len(parts) > 1 else parts[0])
    o = _rms(o, g_out) * _sigmoid(hog)
    return o, st


def _hgrn_prompt_body(hq_ref, hf_ref, hi_ref, hog_ref, lb_ref, go_ref, o_ref, s_ref, st_s, *, tb, nt):
    t = pl.program_id(1)

    @pl.when(t == 0)
    def _():
        st_s[...] = jnp.zeros_like(st_s)

    o, st = _hgrn_block(hq_ref[...], hf_ref[...], hi_ref[...], hog_ref[...], lb_ref[0], go_ref[...],
                        st_s[...], tb=tb)
    o_ref[...] = o.astype(o_ref.dtype)
    st_s[...] = st

    @pl.when(t == nt - 1)
    def _():
        s_ref[0] = st.T


def _hgrn_prompt(z, lb, g_out, *, seq, tb, n_heads=HG_HEADS, col_hq=COL_HQ, col_hf=COL_HF, col_hi=COL_HI,
                 col_hog=COL_HOG):
    nt = seq // tb
    blk = lambda c0: pl.BlockSpec((tb, HG_KDIM), lambda h, t, c=c0 // HG_KDIM: (t, c + h))
    body = functools.partial(_hgrn_prompt_body, tb=tb, nt=nt)
    return pl.pallas_call(
        body,
        grid=(n_heads, nt),
        in_specs=[blk(col_hq), blk(col_hf), blk(col_hi), blk(col_hog),
                  pl.BlockSpec((1, 1, HG_KDIM), lambda h, t: (h, 0, 0)),
                  pl.BlockSpec((1, HG_VDIM), lambda h, t: (0, 0))],
        out_specs=(pl.BlockSpec((tb, HG_VDIM), lambda h, t: (t, h)),
                   pl.BlockSpec((1, HG_KDIM, HG_VDIM), lambda h, t: (h, 0, 0))),
        out_shape=(jax.ShapeDtypeStruct((seq, n_heads * HG_VDIM), BF16),
                   jax.ShapeDtypeStruct((n_heads, HG_KDIM, HG_VDIM), F32)),
        scratch_shapes=[pltpu.VMEM((HG_VDIM, HG_KDIM), F32)],
        compiler_params=pltpu.CompilerParams(dimension_semantics=("arbitrary", "arbitrary"),
                                             vmem_limit_bytes=VMEM_LIMIT),
        name="hgrn_prompt",
    )(z, z, z, z, lb.reshape(n_heads, 1, HG_KDIM), g_out.reshape(1, HG_VDIM))


def _hgrn_sample_body(hq_ref, hf_ref, hi_ref, hog_ref, lb_ref, go_ref, s0_ref, o_ref, s_ref, *, t_new, n_heads):
    pad = jnp.zeros((HG_CHUNK - t_new, HG_KDIM), F32)
    for h in range(n_heads):
        cs = slice(h * HG_KDIM, (h + 1) * HG_KDIM)
        ext = lambda ref: jnp.concatenate([ref[:, cs], pad], axis=0)
        o, st = _hgrn_block(ext(hq_ref), ext(hf_ref), ext(hi_ref), ext(hog_ref), lb_ref[:, cs], go_ref[...],
                            s0_ref[0, h].T, tb=HG_CHUNK, n_valid=t_new)
        o_ref[:, cs] = o[:t_new]
        s_ref[0, h] = st.T


def _hgrn_sample(hq, hf, hi, hog, lb, g_out, s0, *, n_batch, t_new, n_heads=HG_HEADS):
    width = n_heads * HG_KDIM
    row = lambda: pl.BlockSpec((t_new, width), lambda b: (b, 0))
    st = lambda: pl.BlockSpec((1, n_heads, HG_KDIM, HG_VDIM), lambda b: (b, 0, 0, 0))
    body = functools.partial(_hgrn_sample_body, t_new=t_new, n_heads=n_heads)
    return pl.pallas_call(
        body,
        grid=(n_batch,),
        in_specs=[row(), row(), row(), row(),
                  pl.BlockSpec((1, width), lambda b: (0, 0)),
                  pl.BlockSpec((1, HG_VDIM), lambda b: (0, 0)),
                  st()],
        out_specs=(row(), st()),
        out_shape=(jax.ShapeDtypeStruct((n_batch * t_new, width), F32),
                   jax.ShapeDtypeStruct(s0.shape, F32)),
        compiler_params=pltpu.CompilerParams(dimension_semantics=("arbitrary",),
                                             vmem_limit_bytes=VMEM_LIMIT),
        name="hgrn_sample",
    )(hq, hf, hi, hog, lb.reshape(1, width), g_out.reshape(1, HG_VDIM), s0)


def _tile_spec(bm, bn, col0):
    cb = col0 // bn
    return pl.BlockSpec((bm, bn), lambda j, i, *_: (i, cb + j))


def kernel(x_prompt, x_sample, cache_kv_w128, cache_kv_w512, cache_kv_w2048, state_hgrn, p_prompt, p_sample,
           g_mix, w_in, g_q, g_k, hg_lb_raw, g_hg_out, w_up_attn, w_up_hgrn, w_out, g_ffn, w_ff_up, w_ff_down,
           w_ple, w_ple_gate):
    bm = 768
    x_all = jnp.concatenate([x_prompt.reshape(SEQ, D_MODEL), x_sample.reshape(N_SAMPLE, D_MODEL)], axis=0)
    p_all = jnp.concatenate([p_prompt.reshape(SEQ, PLE_DIM), p_sample.reshape(N_SAMPLE, PLE_DIM)], axis=0)
    lb = jnp.cumsum(jax.nn.softmax(hg_lb_raw.astype(F32), axis=0), axis=0)[0]

    n_mix = _rmsnorm_rows(x_all, g_mix[0], name="norm_mix")
    z, = _matmul(n_mix, w_in[0], bm=bm, bn=512, name="in_proj")

    o_attn_p, kn0, kn1, kn2 = _attn_prompt(z, g_q[0], g_k[0], seq=SEQ, sb=2048,
                                           dils=tuple(d for _, d in DIL_GROUPS))
    zs = z[SEQ:]
    caches = [c.reshape(DEC_BATCH, c.shape[2], 2 * A_WIDTH) for c in (cache_kv_w128, cache_kv_w512, cache_kv_w2048)]
    o_attn_s, kn_s = _attn_sample(zs[:, COL_Q:COL_K], zs[:, COL_K:COL_V], zs[:, COL_V:COL_HQ], caches,
                                  g_q[0], g_k[0], n_batch=DEC_BATCH, t_new=DEC_SEQ, groups=DIL_GROUPS)
    o_attn = jnp.concatenate([o_attn_p, o_attn_s.astype(BF16)], axis=0)

    o_hg_p, st_p = _hgrn_prompt(z, lb, g_hg_out[0], seq=SEQ, tb=256)
    o_hg_s, st_s = _hgrn_sample(zs[:, COL_HQ:COL_HF], zs[:, COL_HF:COL_HI], zs[:, COL_HI:COL_HOG],
                                zs[:, COL_HOG:COL_GA], lb, g_hg_out[0], state_hgrn[0],
                                n_batch=DEC_BATCH, t_new=DEC_SEQ)
    o_hg = jnp.concatenate([o_hg_p, o_hg_s.astype(BF16)], axis=0)

    bn = 512
    t_attn, = _matmul(o_attn, w_up_attn[0], bm=bm, bn=bn, extras=(z,), extra_specs=(_tile_spec(bm, bn, COL_GA),),
                      epilogue=lambda acc, ga: (_sigmoid(ga[...]) * acc,), name="up_attn")
    merged, = _matmul(o_hg, w_up_hgrn[0], bm=bm, bn=bn, extras=(z, t_attn),
                      extra_specs=(_tile_spec(bm, bn, COL_GB), _tile_spec(bm, bn, 0)),
                      epilogue=lambda acc, gb, ta: (ta[...] + _sigmoid(gb[...]) * acc,),
                      out_dtypes=(BF16,), name="up_hgrn")
    x1, = _matmul(merged, w_out[0], bm=bm, bn=bn, extras=(x_all,), extra_specs=(_tile_spec(bm, bn, 0),),
                  epilogue=lambda acc, xr: (xr[...] + acc,), name="out_proj")

    h_ffn = _rmsnorm_rows(x1, g_ffn[0], name="norm_ffn")
    hid, = _matmul(h_ffn, w_ff_up[0], bm=bm, bn=bn, out_dtypes=(BF16,),
                   epilogue=lambda acc: (jnp.square(jnp.maximum(acc, 0.0)),), name="ffn_up")
    bm_d, bn_d, bk_d = 1408, 1024, 1024
    x2, x2b = _matmul(hid, w_ff_down[0], bm=bm_d, bn=bn_d, bk=bk_d, extras=(x1,),
                      extra_specs=(_tile_spec(bm_d, bn_d, 0),),
                      epilogue=lambda acc, xr: (xr[...] + acc,) * 2, out_dtypes=(F32, BF16), name="ffn_down")

    def ple_epilogue(acc, xr, p_ref, wp_ref):
        pe = _dot(p_ref[...].astype(BF16), wp_ref[...].astype(BF16))
        return (xr[...] + _sigmoid(acc) * pe,)

    x3, = _matmul(x2b, w_ple_gate[0], bm=bm, bn=bn, extras=(x2, p_all, w_ple[0]),
                  extra_specs=(_tile_spec(bm, bn, 0),
                               pl.BlockSpec((bm, PLE_DIM), lambda j, i: (i, 0)),
                               pl.BlockSpec((PLE_DIM, bn), lambda j, i: (0, j))),
                  epilogue=ple_epilogue, name="ple")

    y_prompt = x3[:SEQ].reshape(1, SEQ, D_MODEL)
    y_sample = x3[SEQ:].reshape(DEC_BATCH, DEC_SEQ, D_MODEL)
    kv_p, kv_s = [], []
    for g, (window, _) in enumerate(DIL_GROUPS):
        length = min(window, SEQ)
        kn_g = (kn0, kn1, kn2)[g][SEQ - length:].reshape(length, A_HEADS, HEAD_DIM)
        v_g = z[SEQ - length:SEQ, COL_V + g * A_WIDTH:COL_V + (g + 1) * A_WIDTH].reshape(length, A_HEADS, HEAD_DIM)
        kv_p.append(jnp.stack([kn_g, v_g], axis=1)[None, None])
        ks_g = kn_s[:, g * A_WIDTH:(g + 1) * A_WIDTH].reshape(DEC_BATCH, DEC_SEQ, A_HEADS, HEAD_DIM)
        vs_g = zs[:, COL_V + g * A_WIDTH:COL_V + (g + 1) * A_WIDTH].reshape(DEC_BATCH, DEC_SEQ, A_HEADS, HEAD_DIM)
        kv_s.append(jnp.stack([ks_g, vs_g], axis=2)[None])
    return (y_prompt, y_sample, kv_p[0], kv_p[1], kv_p[2], st_p[None, None],
            kv_s[0], kv_s[1], kv_s[2], st_s[None])
```

```python
import functools

import numpy as np
import jax
import jax.numpy as jnp
from jax import lax
from jax.experimental import pallas as pl
from jax.experimental.pallas import tpu as pltpu

F32 = jnp.float32
BF16 = jnp.bfloat16

D_MODEL = 4096
SEQ = 8192
DEC_BATCH = 32
DEC_SEQ = 8
N_SAMPLE = DEC_BATCH * DEC_SEQ
M_ALL = SEQ + N_SAMPLE
DIL_GROUPS = ((128, 1), (512, 4), (2048, 16))
N_GROUPS = 3
A_HEADS = 8
HEAD_DIM = 128
A_WIDTH = A_HEADS * HEAD_DIM
BAND = 128
HG_HEADS = 16
HG_KDIM = 128
HG_VDIM = 128
HG_WIDTH = HG_HEADS * HG_VDIM
HG_CHUNK = 32
FFN_HIDDEN = 4 * D_MODEL
PLE_DIM = 256
NORM_EPS = 1e-6
ATTN_SCALE = HEAD_DIM ** -0.5

COL_Q = 0
COL_K = COL_Q + N_GROUPS * A_WIDTH
COL_V = COL_K + N_GROUPS * A_WIDTH
COL_HQ = COL_V + N_GROUPS * A_WIDTH
COL_HF = COL_HQ + HG_HEADS * HG_KDIM
COL_HI = COL_HF + HG_HEADS * HG_KDIM
COL_HOG = COL_HI + HG_WIDTH
COL_GA = COL_HOG + HG_WIDTH
COL_GB = COL_GA + D_MODEL
IN_WIDTH = COL_GB + D_MODEL

VMEM_LIMIT = 56 * 1024 * 1024


def _dot(a, b):
    return jnp.dot(a, b, preferred_element_type=F32)


def _dot_nt(a, b):
    return lax.dot_general(a, b, (((1,), (1,)), ((), ())), preferred_element_type=F32)


def _dot_tn(a, b):
    return lax.dot_general(a, b, (((0,), (0,)), ((), ())), preferred_element_type=F32)


def _rms(x, g):
    return x * lax.rsqrt(jnp.mean(x * x, axis=-1, keepdims=True) + NORM_EPS) * g


def _sigmoid(x):
    return 1.0 / (1.0 + jnp.exp(-x))


def _mm_body_fullk(*refs, n_extra, n_out, epilogue, k_chunk):
    a_ref, w_ref = refs[:2]
    extra = refs[2:2 + n_extra]
    outs = refs[2 + n_extra:2 + n_extra + n_out]
    wb_ref, = refs[2 + n_extra + n_out:]
    kdim = w_ref.shape[0]

    @pl.when(pl.program_id(1) == 0)
    def _():
        def cast(c, carry):
            r0 = pl.multiple_of(c * k_chunk, k_chunk)
            wb_ref[pl.ds(r0, k_chunk), :] = w_ref[pl.ds(r0, k_chunk), :].astype(BF16)
            return carry
        lax.fori_loop(0, kdim // k_chunk, cast, 0)

    acc = _dot(a_ref[...].astype(BF16), wb_ref[...])
    res = epilogue(acc, *extra)
    for o_ref, r in zip(outs, res):
        o_ref[...] = r.astype(o_ref.dtype)


def _mm_body_ktiled(*refs, n_extra, n_out, epilogue, nk):
    a_ref, w_ref = refs[:2]
    extra = refs[2:2 + n_extra]
    outs = refs[2 + n_extra:2 + n_extra + n_out]
    acc_ref, = refs[2 + n_extra + n_out:]
    k = pl.program_id(2)

    @pl.when(k == 0)
    def _():
        acc_ref[...] = jnp.zeros_like(acc_ref)

    acc_ref[...] += _dot(a_ref[...].astype(BF16), w_ref[...].astype(BF16))

    @pl.when(k == nk - 1)
    def _():
        res = epilogue(acc_ref[...], *extra)
        for o_ref, r in zip(outs, res):
            o_ref[...] = r.astype(o_ref.dtype)


def _matmul(a, w, *, bm, bn, bk=None, a_col0=0, k_total=None, extras=(), extra_specs=(),
            out_dtypes=(F32,), epilogue=None, name="mm"):
    m = a.shape[0]
    kdim, n = w.shape
    if k_total is None:
        k_total = kdim
    assert kdim == k_total
    if epilogue is None:
        epilogue = lambda acc: (acc,)
    n_out = len(out_dtypes)
    out_shape = tuple(jax.ShapeDtypeStruct((m, n), dt) for dt in out_dtypes)
    assert m % bm == 0 and n % bn == 0
    if bk is None or bk == kdim:
        assert a_col0 % kdim == 0
        ac = a_col0 // kdim
        grid = (n // bn, m // bm)
        in_specs = [pl.BlockSpec((bm, kdim), lambda j, i: (i, ac)),
                    pl.BlockSpec((kdim, bn), lambda j, i: (0, j))] + list(extra_specs)
        out_specs = tuple(pl.BlockSpec((bm, bn), lambda j, i: (i, j)) for _ in out_dtypes)
        body = functools.partial(_mm_body_fullk, n_extra=len(extras), n_out=n_out, epilogue=epilogue,
                                 k_chunk=min(512, kdim))
        scratch = [pltpu.VMEM((kdim, bn), BF16)]
        sem = ("arbitrary", "arbitrary")
    else:
        assert kdim % bk == 0 and a_col0 % bk == 0
        ac = a_col0 // bk
        nk = kdim // bk
        grid = (n // bn, m // bm, nk)
        in_specs = [pl.BlockSpec((bm, bk), lambda j, i, k: (i, ac + k)),
                    pl.BlockSpec((bk, bn), lambda j, i, k: (k, j))] + list(extra_specs)
        out_specs = tuple(pl.BlockSpec((bm, bn), lambda j, i, k: (i, j)) for _ in out_dtypes)
        body = functools.partial(_mm_body_ktiled, n_extra=len(extras), n_out=n_out, epilogue=epilogue, nk=nk)
        scratch = [pltpu.VMEM((bm, bn), F32)]
        sem = ("arbitrary", "arbitrary", "arbitrary")
    res = pl.pallas_call(
        body,
        grid=grid,
        in_specs=in_specs,
        out_specs=out_specs,
        out_shape=out_shape,
        scratch_shapes=scratch,
        compiler_params=pltpu.CompilerParams(dimension_semantics=sem, vmem_limit_bytes=VMEM_LIMIT),
        name=name,
    )(a, w, *extras)
    return res


def _rmsnorm_body(x_ref, g_ref, o_ref):
    o_ref[...] = _rms(x_ref[...], g_ref[...]).astype(o_ref.dtype)


def _rmsnorm_rows(x, g, *, bm=256, name="rmsnorm"):
    m, d = x.shape
    return pl.pallas_call(
        _rmsnorm_body,
        grid=(m // bm,),
        in_specs=[pl.BlockSpec((bm, d), lambda i: (i, 0)), pl.BlockSpec((1, d), lambda i: (0, 0))],
        out_specs=pl.BlockSpec((bm, d), lambda i: (i, 0)),
        out_shape=jax.ShapeDtypeStruct((m, d), BF16),
        compiler_params=pltpu.CompilerParams(dimension_semantics=("arbitrary",)),
        name=name,
    )(x, g.reshape(1, d))


def _ld_rows(ref, start, dil):
    if dil == 1:
        return ref[pl.ds(start, BAND), :]
    return ref[pl.ds(start, BAND, stride=dil), :]


def _st_rows(ref, g, start, dil, val):
    if dil == 1:
        ref[g, pl.ds(start, BAND), :] = val
    else:
        ref[g, pl.ds(start, BAND, stride=dil), :] = val


def _attn_prompt_body(slope_ref, gq_ref, gk_ref,
                      q0, k0, v0, kp0, vp0, q1, k1, v1, kp1, vp1, q2, k2, v2, kp2, vp2,
                      o_ref, kn0, kn1, kn2,
                      qn_s, kn_s, vv_s, og_s, lse_s, *, sb, dils):
    first_block = pl.program_id(1) == 0
    groups = ((q0, k0, v0, kp0, vp0, kn0), (q1, k1, v1, kp1, vp1, kn1), (q2, k2, v2, kp2, vp2, kn2))
    ii = lax.broadcasted_iota(jnp.int32, (BAND, BAND), 0)
    jj = lax.broadcasted_iota(jnp.int32, (BAND, BAND), 1)
    dist_cur = (ii - jj).astype(F32)
    dist_prev = (BAND + ii - jj).astype(F32)
    ok_cur = jj <= ii
    ok_prev = jj >= ii
    neg_inf = jnp.float32(-jnp.inf)
    base = kn_s.shape[0] - sb

    for g, dil in enumerate(dils):
        q_ref, k_ref, v_ref, kp_ref, vp_ref, kn_out = groups[g]
        pr = BAND * dil
        gq = gq_ref[g:g + 1, :]
        gk = gk_ref[g:g + 1, :]
        slope = slope_ref[0, :, g * HEAD_DIM:(g + 1) * HEAD_DIM]
        qn_s[...] = _rms(q_ref[...], gq)
        kn = _rms(k_ref[...], gk)
        kn_out[...] = kn
        kn_s[pl.ds(base, sb), :] = kn
        kn_s[pl.ds(base - pr, pr), :] = _rms(kp_ref[...], gk)
        vv_s[pl.ds(base, sb), :] = v_ref[...]
        vv_s[pl.ds(base - pr, pr), :] = vp_ref[...]
        bias_cur = jnp.where(ok_cur, -slope * (dist_cur * dil), neg_inf)
        bias_prev = jnp.where(ok_prev, -slope * (dist_prev * dil), neg_inf)
        bias_prev_first = jnp.where(first_block, neg_inf, bias_prev)
        for r in range(dil):
            for s in range(sb // pr):
                c0 = s * pr + r
                qs = _ld_rows(qn_s, c0, dil).astype(BF16)
                kc = _ld_rows(kn_s, base + c0, dil).astype(BF16)
                kp = _ld_rows(kn_s, base + c0 - pr, dil).astype(BF16)
                vc = _ld_rows(vv_s, base + c0, dil).astype(BF16)
                vp = _ld_rows(vv_s, base + c0 - pr, dil).astype(BF16)
                lc = _dot_nt(qs, kc) * ATTN_SCALE + bias_cur
                lp = _dot_nt(qs, kp) * ATTN_SCALE + (bias_prev_first if s == 0 else bias_prev)
                mx = jnp.maximum(jnp.max(lc, axis=-1, keepdims=True), jnp.max(lp, axis=-1, keepdims=True))
                pc = jnp.exp(lc - mx)
                pp = jnp.exp(lp - mx)
                ssum = jnp.sum(pc, axis=-1, keepdims=True) + jnp.sum(pp, axis=-1, keepdims=True)
                o = (_dot(pc.astype(BF16), vc) + _dot(pp.astype(BF16), vp)) / ssum
                lse = mx + jnp.log(ssum)
                _st_rows(og_s, g, c0, dil, o)
                _st_rows(lse_s, g, c0, dil, jnp.broadcast_to(lse, (BAND, HEAD_DIM)))

    l0, l1, l2 = lse_s[0], lse_s[1], lse_s[2]
    mx = jnp.maximum(jnp.maximum(l0, l1), l2)
    w0, w1, w2 = jnp.exp(l0 - mx), jnp.exp(l1 - mx), jnp.exp(l2 - mx)
    o = (w0 * og_s[0] + w1 * og_s[1] + w2 * og_s[2]) / (w0 + w1 + w2)
    o_ref[...] = o.astype(o_ref.dtype)


def _alibi_slopes():
    n = N_GROUPS * A_HEADS
    e = jnp.arange(1, n + 1, dtype=F32)
    return jnp.exp2(-8.0 * e / n).reshape(N_GROUPS, A_HEADS)


def _attn_prompt(z, g_q, g_k, *, seq, sb, dils, n_heads=A_HEADS, col_q=COL_Q, col_k=COL_K, col_v=COL_V):
    nb = seq // sb
    slopes = _alibi_slopes()
    slope_arr = jnp.broadcast_to(slopes.T[:, None, :, None], (n_heads, 1, N_GROUPS, HEAD_DIM))
    slope_arr = slope_arr.reshape(n_heads, 1, N_GROUPS * HEAD_DIM)
    a_width = n_heads * HEAD_DIM
    in_specs = [pl.BlockSpec((1, 1, N_GROUPS * HEAD_DIM), lambda h, i: (h, 0, 0)),
                pl.BlockSpec((N_GROUPS, HEAD_DIM), lambda h, i: (0, 0)),
                pl.BlockSpec((N_GROUPS, HEAD_DIM), lambda h, i: (0, 0))]
    operands = [slope_arr, g_q, g_k]
    max_pr = BAND * max(dils)
    for g, dil in enumerate(dils):
        pr = BAND * dil
        ratio = sb // pr
        cq = (col_q + g * a_width) // HEAD_DIM
        ck = (col_k + g * a_width) // HEAD_DIM
        cv = (col_v + g * a_width) // HEAD_DIM
        cur = lambda c: pl.BlockSpec((sb, HEAD_DIM), lambda h, i, c=c: (i, c + h))
        prev = lambda c: pl.BlockSpec((pr, HEAD_DIM),
                                      lambda h, i, c=c, ratio=ratio: (jnp.maximum(i * ratio - 1, 0), c + h))
        in_specs += [cur(cq), cur(ck), cur(cv), prev(ck), prev(cv)]
        operands += [z, z, z, z, z]
    out_block = pl.BlockSpec((sb, HEAD_DIM), lambda h, i: (i, h))
    out_shape = (jax.ShapeDtypeStruct((seq, a_width), BF16),) + tuple(
        jax.ShapeDtypeStruct((seq, a_width), F32) for _ in dils)
    body = functools.partial(_attn_prompt_body, sb=sb, dils=tuple(dils))
    return pl.pallas_call(
        body,
        grid=(n_heads, nb),
        in_specs=in_specs,
        out_specs=(out_block,) * 4,
        out_shape=out_shape,
        scratch_shapes=[pltpu.VMEM((sb, HEAD_DIM), F32),
                        pltpu.VMEM((sb + max_pr, HEAD_DIM), F32),
                        pltpu.VMEM((sb + max_pr, HEAD_DIM), F32),
                        pltpu.VMEM((N_GROUPS, sb, HEAD_DIM), F32),
                        pltpu.VMEM((N_GROUPS, sb, HEAD_DIM), F32)],
        compiler_params=pltpu.CompilerParams(dimension_semantics=("arbitrary", "arbitrary"),
                                             vmem_limit_bytes=VMEM_LIMIT),
        name="attn_prompt",
    )(*operands)


def _sample_problems(dil, t_new):
    n_prob = min(dil, t_new)
    return n_prob, t_new // n_prob


def _sample_bias_tables(groups, t_new, n_heads):
    n_soft = len(groups) * n_heads
    tabs_c, tabs_n = [], []
    for g, (window, dil) in enumerate(groups):
        n_prob, tok_per = _sample_problems(dil, t_new)
        slopes = 2.0 ** (-8.0 * (g * n_heads + np.arange(n_heads) + 1) / n_soft)
        rows = np.arange(tok_per * n_heads)
        i, hp = rows // n_heads, rows % n_heads
        cols = np.arange(BAND * n_heads)
        m, h = cols // n_heads, cols % n_heads
        dist = window + i[:, None] * n_prob - m[None, :] * dil
        ok = (h[None, :] == hp[:, None]) & (dist <= window) & (dist > 0) & (dist % dil == 0)
        tabs_c.append(np.where(ok, -slopes[hp][:, None] * dist, -np.inf).astype(np.float32))
        cols = np.arange(t_new * n_heads)
        s, h = cols // n_heads, cols % n_heads
        per_p = []
        for p in range(n_prob):
            dist = (p + i * n_prob)[:, None] - s[None, :]
            ok = (h[None, :] == hp[:, None]) & (dist >= 0) & (dist % dil == 0)
            per_p.append(np.where(ok, -slopes[hp][:, None] * dist, -np.inf).astype(np.float32))
        tabs_n.append(np.stack(per_p))
    return tabs_c, tabs_n


def _attn_sample_body(gq_ref, gk_ref, q_ref, k_ref, v_ref, c0_ref, c1_ref, c2_ref,
                      bc0, bc1, bc2, bn0, bn1, bn2, o_ref, kn_ref, og_s, lse_s, *, t_new, groups, n_heads):
    caches = (c0_ref, c1_ref, c2_ref)
    bias_c = (bc0, bc1, bc2)
    bias_n = (bn0, bn1, bn2)
    kv_rows = 2 * n_heads
    for g, (window, dil) in enumerate(groups):
        c_ref = caches[g]
        n_prob, tok_per = _sample_problems(dil, t_new)
        qn = _rms(q_ref[0, g], gq_ref[g:g + 1, :])
        kn = _rms(k_ref[0, g], gk_ref[g:g + 1, :])
        kn_ref[0, g] = kn
        knb = kn.astype(BF16)
        vnb = v_ref[0, g].astype(BF16)
        for p in range(n_prob):
            toks = [p + i * n_prob for i in range(tok_per)]
            parts = [qn[t * n_heads:(t + 1) * n_heads] for t in toks]
            qp = (parts[0] if tok_per == 1 else jnp.concatenate(parts, axis=0)).astype(BF16)
            kc = c_ref[:, p * kv_rows:p * kv_rows + n_heads, :].reshape(BAND * n_heads, HEAD_DIM).astype(BF16)
            vc = c_ref[:, p * kv_rows + n_heads:(p + 1) * kv_rows, :].reshape(BAND * n_heads, HEAD_DIM).astype(BF16)
            lc = _dot_nt(qp, kc) * ATTN_SCALE + bias_c[g][...]
            ln = _dot_nt(qp, knb) * ATTN_SCALE + bias_n[g][p]
            mx = jnp.maximum(jnp.max(lc, axis=-1, keepdims=True), jnp.max(ln, axis=-1, keepdims=True))
            pc = jnp.exp(lc - mx)
            pn = jnp.exp(ln - mx)
            ssum = jnp.sum(pc, axis=-1, keepdims=True) + jnp.sum(pn, axis=-1, keepdims=True)
            o = (_dot(pc.astype(BF16), vc) + _dot(pn.astype(BF16), vnb)) / ssum
            lse = jnp.broadcast_to(mx + jnp.log(ssum), o.shape)
            for i, t in enumerate(toks):
                og_s[g, t * n_heads:(t + 1) * n_heads, :] = o[i * n_heads:(i + 1) * n_heads]
                lse_s[g, t * n_heads:(t + 1) * n_heads, :] = lse[i * n_heads:(i + 1) * n_heads]
    l0, l1, l2 = lse_s[0], lse_s[1], lse_s[2]
    mx = jnp.maximum(jnp.maximum(l0, l1), l2)
    w0, w1, w2 = jnp.exp(l0 - mx), jnp.exp(l1 - mx), jnp.exp(l2 - mx)
    o_ref[0] = (w0 * og_s[0] + w1 * og_s[1] + w2 * og_s[2]) / (w0 + w1 + w2)


def _attn_sample(qs, ks, vs, caches, g_q, g_k, *, n_batch, t_new, groups, n_heads=A_HEADS):
    n_g = len(groups)
    rows = t_new * n_heads
    tabs_c, tabs_n = _sample_bias_tables(groups, t_new, n_heads)
    new_spec = lambda: pl.BlockSpec((1, n_g, rows, HEAD_DIM), lambda b: (b, 0, 0, 0))
    in_specs = [pl.BlockSpec((n_g, HEAD_DIM), lambda b: (0, 0)),
                pl.BlockSpec((n_g, HEAD_DIM), lambda b: (0, 0)),
                new_spec(), new_spec(), new_spec()]
    operands = [g_q, g_k, qs, ks, vs]
    for c, (window, dil) in zip(caches, groups):
        assert c.shape[1] == window and window == BAND * dil
        n_prob, _ = _sample_problems(dil, t_new)
        c3 = c.reshape(n_batch * BAND, dil * 2 * n_heads, HEAD_DIM)
        in_specs.append(pl.BlockSpec((BAND, n_prob * 2 * n_heads, HEAD_DIM), lambda b: (b, 0, 0)))
        operands.append(c3)
    for tab in tabs_c:
        in_specs.append(pl.BlockSpec(tab.shape, lambda b: (0, 0)))
        operands.append(jnp.asarray(tab))
    for tab in tabs_n:
        in_specs.append(pl.BlockSpec(tab.shape, lambda b: (0, 0, 0)))
        operands.append(jnp.asarray(tab))
    body = functools.partial(_attn_sample_body, t_new=t_new, groups=tuple(groups), n_heads=n_heads)
    return pl.pallas_call(
        body,
        grid=(n_batch,),
        in_specs=in_specs,
        out_specs=(pl.BlockSpec((1, rows, HEAD_DIM), lambda b: (b, 0, 0)), new_spec()),
        out_shape=(jax.ShapeDtypeStruct((n_batch, rows, HEAD_DIM), F32),
                   jax.ShapeDtypeStruct((n_batch, n_g, rows, HEAD_DIM), F32)),
        scratch_shapes=[pltpu.VMEM((n_g, rows, HEAD_DIM), F32), pltpu.VMEM((n_g, rows, HEAD_DIM), F32)],
        compiler_params=pltpu.CompilerParams(dimension_semantics=("arbitrary",),
                                             vmem_limit_bytes=VMEM_LIMIT),
        name="attn_sample",
    )(*operands)


def _split3(x):
    hi = x.astype(BF16)
    r1 = x - hi.astype(F32)
    mid = r1.astype(BF16)
    lo = (r1 - mid.astype(F32)).astype(BF16)
    return hi, mid, lo


def _hgrn_block(hq, hf, hi, hog, lb, g_out, st, *, tb, n_valid=None):
    c = HG_CHUNK
    q = hq * _sigmoid(hq) * (HG_KDIM ** -0.5)
    gate = lb + (1.0 - lb) * _sigmoid(hf)
    log_g = jnp.log(gate)
    k = (1.0 - lb) * _sigmoid(-hf)
    if n_valid is not None:
        rows = lax.broadcasted_iota(jnp.int32, (tb, HG_KDIM), 0)
        log_g = jnp.where(rows < n_valid, log_g, 0.0)
        k = jnp.where(rows < n_valid, k, 0.0)
        q = jnp.where(rows < n_valid, q, 0.0)
    row = lax.broadcasted_iota(jnp.int32, (tb, tb), 0)
    col = lax.broadcasted_iota(jnp.int32, (tb, tb), 1)
    same = (row // c) == (col // c)
    tri_ok = same & (col <= row)
    tri = jnp.where(tri_ok, 1.0, 0.0).astype(BF16)
    blk = jnp.where(same, 1.0, 0.0).astype(BF16)
    p_hi, p_mid, p_lo = _split3(log_g)
    b = _dot(tri, p_hi) + _dot(tri, p_mid) + _dot(tri, p_lo)
    b_last = _dot(blk, p_hi) + _dot(blk, p_mid) + _dot(blk, p_lo)
    q_dec = (q * jnp.exp(b)).astype(BF16)
    k_dec = (k * jnp.exp(-b)).astype(BF16)
    k_end = (k * jnp.exp(b_last - b)).astype(BF16)
    decay = jnp.exp(b_last)
    vb = hi.astype(BF16)
    a = jnp.where(tri_ok, _dot_nt(q_dec, k_dec), 0.0)
    o = _dot(a.astype(BF16), vb)
    parts = []
    for ci in range(tb // c):
        sl = slice(ci * c, (ci + 1) * c)
        parts.append(_dot_nt(q_dec[sl], st.astype(BF16)))
        st = st * decay[ci * c:ci * c + 1, :] + _dot_tn(vb[sl], k_end[sl])
    o = o + (jnp.concatenate(parts, axis=0) if len(parts) > 1 else parts[0])
    o = _rms(o, g_out) * _sigmoid(hog)
    return o, st


def _hgrn_prompt_body(hq_ref, hf_ref, hi_ref, hog_ref, lb_ref, go_ref, o_ref, s_ref, st_s, *, tb, nt, hb):
    t = pl.program_id(1)

    @pl.when(t == 0)
    def _():
        st_s[...] = jnp.zeros_like(st_s)

    for hh in range(hb):
        cs = slice(hh * HG_KDIM, (hh + 1) * HG_KDIM)
        o, st = _hgrn_block(hq_ref[:, cs], hf_ref[:, cs], hi_ref[:, cs], hog_ref[:, cs], lb_ref[0, :, cs],
                            go_ref[...], st_s[hh], tb=tb)
        o_ref[:, cs] = o.astype(o_ref.dtype)
        st_s[hh] = st

        @pl.when(t == nt - 1)
        def _(st=st, hh=hh):
            s_ref[hh] = st.T


def _hgrn_prompt(z, lb, g_out, *, seq, tb, hb, n_heads=HG_HEADS, col_hq=COL_HQ, col_hf=COL_HF, col_hi=COL_HI,
                 col_hog=COL_HOG):
    nt = seq // tb
    bw = hb * HG_KDIM
    blk = lambda c0: pl.BlockSpec((tb, bw), lambda h, t, c=c0 // bw: (t, c + h))
    body = functools.partial(_hgrn_prompt_body, tb=tb, nt=nt, hb=hb)
    return pl.pallas_call(
        body,
        grid=(n_heads // hb, nt),
        in_specs=[blk(col_hq), blk(col_hf), blk(col_hi), blk(col_hog),
                  pl.BlockSpec((1, 1, bw), lambda h, t: (h, 0, 0)),
                  pl.BlockSpec((1, HG_VDIM), lambda h, t: (0, 0))],
        out_specs=(pl.BlockSpec((tb, bw), lambda h, t: (t, h)),
                   pl.BlockSpec((hb, HG_KDIM, HG_VDIM), lambda h, t: (h, 0, 0))),
        out_shape=(jax.ShapeDtypeStruct((seq, n_heads * HG_VDIM), BF16),
                   jax.ShapeDtypeStruct((n_heads, HG_KDIM, HG_VDIM), F32)),
        scratch_shapes=[pltpu.VMEM((hb, HG_VDIM, HG_KDIM), F32)],
        compiler_params=pltpu.CompilerParams(dimension_semantics=("arbitrary", "arbitrary"),
                                             vmem_limit_bytes=VMEM_LIMIT),
        name="hgrn_prompt",
    )(z, z, z, z, lb.reshape(n_heads // hb, 1, bw), g_out.reshape(1, HG_VDIM))


def _hgrn_sample_body(hq_ref, hf_ref, hi_ref, hog_ref, lb_ref, go_ref, s0_ref, o_ref, s_ref, *, t_new, n_heads):
    pad = jnp.zeros((HG_CHUNK - t_new, HG_KDIM), F32)
    for h in range(n_heads):
        cs = slice(h * HG_KDIM, (h + 1) * HG_KDIM)
        ext = lambda ref: jnp.concatenate([ref[:, cs], pad], axis=0)
        o, st = _hgrn_block(ext(hq_ref), ext(hf_ref), ext(hi_ref), ext(hog_ref), lb_ref[:, cs], go_ref[...],
                            s0_ref[0, h].T, tb=HG_CHUNK, n_valid=t_new)
        o_ref[:, cs] = o[:t_new]
        s_ref[0, h] = st.T


def _hgrn_sample(hq, hf, hi, hog, lb, g_out, s0, *, n_batch, t_new, n_heads=HG_HEADS):
    width = n_heads * HG_KDIM
    row = lambda: pl.BlockSpec((t_new, width), lambda b: (b, 0))
    st = lambda: pl.BlockSpec((1, n_heads, HG_KDIM, HG_VDIM), lambda b: (b, 0, 0, 0))
    body = functools.partial(_hgrn_sample_body, t_new=t_new, n_heads=n_heads)
    return pl.pallas_call(
        body,
        grid=(n_batch,),
        in_specs=[row(), row(), row(), row(),
                  pl.BlockSpec((1, width), lambda b: (0, 0)),
                  pl.BlockSpec((1, HG_VDIM), lambda b: (0, 0)),
                  st()],
        out_specs=(row(), st()),
        out_shape=(jax.ShapeDtypeStruct((n_batch * t_new, width), F32),
                   jax.ShapeDtypeStruct(s0.shape, F32)),
        compiler_params=pltpu.CompilerParams(dimension_semantics=("arbitrary",),
                                             vmem_limit_bytes=VMEM_LIMIT),
        name="hgrn_sample",
    )(hq, hf, hi, hog, lb.reshape(1, width), g_out.reshape(1, HG_VDIM), s0)


def _tile_spec(bm, bn, col0):
    cb = col0 // bn
    return pl.BlockSpec((bm, bn), lambda j, i, *_: (i, cb + j))


def kernel(x_prompt, x_sample, cache_kv_w128, cache_kv_w512, cache_kv_w2048, state_hgrn, p_prompt, p_sample,
           g_mix, w_in, g_q, g_k, hg_lb_raw, g_hg_out, w_up_attn, w_up_hgrn, w_out, g_ffn, w_ff_up, w_ff_down,
           w_ple, w_ple_gate):
    bm = 768
    bm_big = 1408
    x_all = jnp.concatenate([x_prompt.reshape(SEQ, D_MODEL), x_sample.reshape(N_SAMPLE, D_MODEL)], axis=0)
    p_all = jnp.concatenate([p_prompt.reshape(SEQ, PLE_DIM), p_sample.reshape(N_SAMPLE, PLE_DIM)], axis=0)
    lb = jnp.cumsum(jax.nn.softmax(hg_lb_raw.astype(F32), axis=0), axis=0)[0]

    n_mix = _rmsnorm_rows(x_all, g_mix[0], name="norm_mix")
    z, = _matmul(n_mix, w_in[0], bm=bm_big, bn=512, name="in_proj")

    o_attn_p, kn0, kn1, kn2 = _attn_prompt(z, g_q[0], g_k[0], seq=SEQ, sb=2048,
                                           dils=tuple(d for _, d in DIL_GROUPS))
    zs = z[SEQ:]
    def by_group(col0):
        a = zs[:, col0:col0 + N_GROUPS * A_WIDTH].reshape(DEC_BATCH, DEC_SEQ, N_GROUPS, A_HEADS, HEAD_DIM)
        return a.transpose(0, 2, 1, 3, 4).reshape(DEC_BATCH, N_GROUPS, DEC_SEQ * A_HEADS, HEAD_DIM)

    caches = [c[0] for c in (cache_kv_w128, cache_kv_w512, cache_kv_w2048)]
    o_attn_s, kn_s = _attn_sample(by_group(COL_Q), by_group(COL_K), by_group(COL_V), caches,
                                  g_q[0], g_k[0], n_batch=DEC_BATCH, t_new=DEC_SEQ, groups=DIL_GROUPS)
    o_attn = jnp.concatenate([o_attn_p, o_attn_s.reshape(N_SAMPLE, A_WIDTH).astype(BF16)], axis=0)

    o_hg_p, st_p = _hgrn_prompt(z, lb, g_hg_out[0], seq=SEQ, tb=256, hb=4)
    o_hg_s, st_s = _hgrn_sample(zs[:, COL_HQ:COL_HF], zs[:, COL_HF:COL_HI], zs[:, COL_HI:COL_HOG],
                                zs[:, COL_HOG:COL_GA], lb, g_hg_out[0], state_hgrn[0],
                                n_batch=DEC_BATCH, t_new=DEC_SEQ)
    o_hg = jnp.concatenate([o_hg_p, o_hg_s.astype(BF16)], axis=0)

    bn = 512
    t_attn, = _matmul(o_attn, w_up_attn[0], bm=bm_big, bn=bn, extras=(z,),
                      extra_specs=(_tile_spec(bm_big, bn, COL_GA),),
                      epilogue=lambda acc, ga: (_sigmoid(ga[...]) * acc,), name="up_attn")
    merged, = _matmul(o_hg, w_up_hgrn[0], bm=bm_big, bn=bn, extras=(z, t_attn),
                      extra_specs=(_tile_spec(bm_big, bn, COL_GB), _tile_spec(bm_big, bn, 0)),
                      epilogue=lambda acc, gb, ta: (ta[...] + _sigmoid(gb[...]) * acc,),
                      out_dtypes=(BF16,), name="up_hgrn")
    x1, = _matmul(merged, w_out[0], bm=bm, bn=bn, extras=(x_all,), extra_specs=(_tile_spec(bm, bn, 0),),
                  epilogue=lambda acc, xr: (xr[...] + acc,), name="out_proj")

    h_ffn = _rmsnorm_rows(x1, g_ffn[0], name="norm_ffn")
    hid, = _matmul(h_ffn, w_ff_up[0], bm=bm_big, bn=bn, out_dtypes=(BF16,),
                   epilogue=lambda acc: (jnp.square(jnp.maximum(acc, 0.0)),), name="ffn_up")
    bm_d, bn_d, bk_d = 1408, 1024, 1024
    x2, x2b = _matmul(hid, w_ff_down[0], bm=bm_d, bn=bn_d, bk=bk_d, extras=(x1,),
                      extra_specs=(_tile_spec(bm_d, bn_d, 0),),
                      epilogue=lambda acc, xr: (xr[...] + acc,) * 2, out_dtypes=(F32, BF16), name="ffn_down")

    def ple_epilogue(acc, xr, p_ref, wp_ref):
        pe = _dot(p_ref[...].astype(BF16), wp_ref[...].astype(BF16))
        return (xr[...] + _sigmoid(acc) * pe,)

    x3, = _matmul(x2b, w_ple_gate[0], bm=bm, bn=bn, extras=(x2, p_all, w_ple[0]),
                  extra_specs=(_tile_spec(bm, bn, 0),
                               pl.BlockSpec((bm, PLE_DIM), lambda j, i: (i, 0)),
                               pl.BlockSpec((PLE_DIM, bn), lambda j, i: (0, j))),
                  epilogue=ple_epilogue, name="ple")

    y_prompt = x3[:SEQ].reshape(1, SEQ, D_MODEL)
    y_sample = x3[SEQ:].reshape(DEC_BATCH, DEC_SEQ, D_MODEL)
    kv_p, kv_s = [], []
    for g, (window, _) in enumerate(DIL_GROUPS):
        length = min(window, SEQ)
        kn_g = (kn0, kn1, kn2)[g][SEQ - length:].reshape(length, A_HEADS, HEAD_DIM)
        v_g = z[SEQ - length:SEQ, COL_V + g * A_WIDTH:COL_V + (g + 1) * A_WIDTH].reshape(length, A_HEADS, HEAD_DIM)
        kv_p.append(jnp.stack([kn_g, v_g], axis=1)[None, None])
        ks_g = kn_s[:, g].reshape(DEC_BATCH, DEC_SEQ, A_HEADS, HEAD_DIM)
        vs_g = zs[:, COL_V + g * A_WIDTH:COL_V + (g + 1) * A_WIDTH].reshape(DEC_BATCH, DEC_SEQ, A_HEADS, HEAD_DIM)
        kv_s.append(jnp.stack([ks_g, vs_g], axis=2)[None])
    return (y_prompt, y_sample, kv_p[0], kv_p[1], kv_p[2], st_p[None, None],
            kv_s[0], kv_s[1], kv_s[2], st_s[None])
```

```python
import functools

import numpy as np
import jax
import jax.numpy as jnp
from jax import lax
from jax.experimental import pallas as pl
from jax.experimental.pallas import tpu as pltpu

F32 = jnp.float32
BF16 = jnp.bfloat16

D_MODEL = 4096
SEQ = 8192
DEC_BATCH = 32
DEC_SEQ = 8
N_SAMPLE = DEC_BATCH * DEC_SEQ
M_ALL = SEQ + N_SAMPLE
DIL_GROUPS = ((128, 1), (512, 4), (2048, 16))
N_GROUPS = 3
A_HEADS = 8
HEAD_DIM = 128
A_WIDTH = A_HEADS * HEAD_DIM
BAND = 128
HG_HEADS = 16
HG_KDIM = 128
HG_VDIM = 128
HG_WIDTH = HG_HEADS * HG_VDIM
HG_CHUNK = 32
FFN_HIDDEN = 4 * D_MODEL
PLE_DIM = 256
NORM_EPS = 1e-6
ATTN_SCALE = HEAD_DIM ** -0.5

COL_Q = 0
COL_K = COL_Q + N_GROUPS * A_WIDTH
COL_V = COL_K + N_GROUPS * A_WIDTH
COL_HQ = COL_V + N_GROUPS * A_WIDTH
COL_HF = COL_HQ + HG_HEADS * HG_KDIM
COL_HI = COL_HF + HG_HEADS * HG_KDIM
COL_HOG = COL_HI + HG_WIDTH
COL_GA = COL_HOG + HG_WIDTH
COL_GB = COL_GA + D_MODEL
IN_WIDTH = COL_GB + D_MODEL

VMEM_LIMIT = 56 * 1024 * 1024


def _dot(a, b):
    return jnp.dot(a, b, preferred_element_type=F32)


def _dot_nt(a, b):
    return lax.dot_general(a, b, (((1,), (1,)), ((), ())), preferred_element_type=F32)


def _dot_tn(a, b):
    return lax.dot_general(a, b, (((0,), (0,)), ((), ())), preferred_element_type=F32)


def _rms(x, g):
    return x * lax.rsqrt(jnp.mean(x * x, axis=-1, keepdims=True) + NORM_EPS) * g


def _sigmoid(x):
    return 1.0 / (1.0 + jnp.exp(-x))


def _mm_body_fullk(*refs, has_tail, tail_split, n_extra, n_out, epilogue, k_chunk):
    n_a = 2 if has_tail else 1
    a_ref = refs[0]
    w_ref = refs[n_a]
    extra = refs[n_a + 1:n_a + 1 + n_extra]
    outs = refs[n_a + 1 + n_extra:n_a + 1 + n_extra + n_out]
    wb_ref, = refs[n_a + 1 + n_extra + n_out:]
    kdim = w_ref.shape[0]
    i = pl.program_id(1)
    last = pl.num_programs(1) - 1

    @pl.when(i == 0)
    def _():
        def cast(c, carry):
            r0 = pl.multiple_of(c * k_chunk, k_chunk)
            wb_ref[pl.ds(r0, k_chunk), :] = w_ref[pl.ds(r0, k_chunk), :].astype(BF16)
            return carry
        lax.fori_loop(0, kdim // k_chunk, cast, 0)

    def run(a):
        epilogue(_dot(a.astype(BF16), wb_ref[...]), i, extra, outs)

    if not has_tail:
        run(a_ref[...])
    else:
        at_ref = refs[1]

        @pl.when(i < last)
        def _():
            run(a_ref[...])

        @pl.when(i == last)
        def _():
            run(jnp.concatenate([a_ref[:tail_split, :], at_ref[...]], axis=0))


def _mm_body_ktiled(*refs, n_extra, n_out, epilogue, nk):
    a_ref, w_ref = refs[:2]
    extra = refs[2:2 + n_extra]
    outs = refs[2 + n_extra:2 + n_extra + n_out]
    acc_ref, = refs[2 + n_extra + n_out:]
    k = pl.program_id(2)

    @pl.when(k == 0)
    def _():
        acc_ref[...] = jnp.zeros_like(acc_ref)

    acc_ref[...] += _dot(a_ref[...].astype(BF16), w_ref[...].astype(BF16))

    @pl.when(k == nk - 1)
    def _():
        epilogue(acc_ref[...], pl.program_id(1), extra, outs)


def _store(outs, *vals):
    for o_ref, v in zip(outs, vals):
        o_ref[...] = v.astype(o_ref.dtype)


def _with_tail(i, main_ref, tail_ref):
    x = main_ref[...]
    split = x.shape[0] - tail_ref.shape[0]
    tail = jnp.concatenate([jnp.zeros((split, x.shape[1]), x.dtype), tail_ref[...]], axis=0)
    rows = lax.broadcasted_iota(jnp.int32, x.shape, 0)
    return jnp.where((rows >= split) & (i == pl.num_programs(1) - 1), tail, x)


def _matmul(a, w, *, bm, bn, bk=None, a_tail=None, extras=(), extra_specs=(), out_shapes, out_specs=None,
            epilogue, name):
    m = a.shape[0] + (0 if a_tail is None else a_tail.shape[0])
    kdim, n = w.shape
    assert m % bm == 0 and n % bn == 0
    nb = m // bm
    n_out = len(out_shapes)
    if bk is None or bk == kdim:
        grid = (n // bn, nb)
        in_specs = [pl.BlockSpec((bm, kdim), lambda j, i: (i, 0))]
        operands = [a]
        tail_split = None
        if a_tail is not None:
            tail_split = a.shape[0] - (nb - 1) * bm
            assert tail_split + a_tail.shape[0] == bm
            in_specs.append(pl.BlockSpec(a_tail.shape, lambda j, i: (0, 0)))
            operands.append(a_tail)
        in_specs += [pl.BlockSpec((kdim, bn), lambda j, i: (0, j))] + list(extra_specs)
        if out_specs is None:
            out_specs = tuple(pl.BlockSpec((bm, bn), lambda j, i: (i, j)) for _ in out_shapes)
        body = functools.partial(_mm_body_fullk, has_tail=a_tail is not None, tail_split=tail_split,
                                 n_extra=len(extras), n_out=n_out, epilogue=epilogue, k_chunk=min(512, kdim))
        scratch = [pltpu.VMEM((kdim, bn), BF16)]
        sem = ("arbitrary", "arbitrary")
    else:
        assert kdim % bk == 0 and a_tail is None
        nk = kdim // bk
        grid = (n // bn, nb, nk)
        in_specs = [pl.BlockSpec((bm, bk), lambda j, i, k: (i, k)),
                    pl.BlockSpec((bk, bn), lambda j, i, k: (k, j))] + list(extra_specs)
        operands = [a]
        if out_specs is None:
            out_specs = tuple(pl.BlockSpec((bm, bn), lambda j, i, k: (i, j)) for _ in out_shapes)
        body = functools.partial(_mm_body_ktiled, n_extra=len(extras), n_out=n_out, epilogue=epilogue, nk=nk)
        scratch = [pltpu.VMEM((bm, bn), F32)]
        sem = ("arbitrary", "arbitrary", "arbitrary")
    return pl.pallas_call(
        body,
        grid=grid,
        in_specs=in_specs,
        out_specs=tuple(out_specs),
        out_shape=tuple(out_shapes),
        scratch_shapes=scratch,
        compiler_params=pltpu.CompilerParams(dimension_semantics=sem, vmem_limit_bytes=VMEM_LIMIT),
        name=name,
    )(*operands, w, *extras)


def _rmsnorm_body(*refs, n_main):
    g_ref, o_ref = refs[-2:]
    if n_main is None:
        o_ref[...] = _rms(refs[0][...], g_ref[...]).astype(o_ref.dtype)
        return
    i = pl.program_id(0)

    @pl.when(i < n_main)
    def _():
        o_ref[...] = _rms(refs[0][...], g_ref[...]).astype(o_ref.dtype)

    @pl.when(i >= n_main)
    def _():
        o_ref[...] = _rms(refs[1][...], g_ref[...]).astype(o_ref.dtype)


def _rmsnorm_rows(x, g, *, x_tail=None, bm=256, name="rmsnorm"):
    m, d = x.shape
    assert m % bm == 0
    n_main = m // bm
    in_specs = [pl.BlockSpec((bm, d), lambda i: (jnp.minimum(i, n_main - 1), 0))]
    operands = [x]
    if x_tail is not None:
        assert x_tail.shape[0] % bm == 0
        in_specs.append(pl.BlockSpec((bm, d), lambda i: (jnp.maximum(i - n_main, 0), 0)))
        operands.append(x_tail)
        m += x_tail.shape[0]
    in_specs.append(pl.BlockSpec((1, d), lambda i: (0, 0)))
    return pl.pallas_call(
        functools.partial(_rmsnorm_body, n_main=None if x_tail is None else n_main),
        grid=(m // bm,),
        in_specs=in_specs,
        out_specs=pl.BlockSpec((bm, d), lambda i: (i, 0)),
        out_shape=jax.ShapeDtypeStruct((m, d), BF16),
        compiler_params=pltpu.CompilerParams(dimension_semantics=("arbitrary",)),
        name=name,
    )(*operands, g.reshape(1, d))


def _ld_rows(ref, start, dil):
    if dil == 1:
        return ref[pl.ds(start, BAND), :]
    return ref[pl.ds(start, BAND, stride=dil), :]


def _st_rows(ref, g, start, dil, val):
    if dil == 1:
        ref[g, pl.ds(start, BAND), :] = val
    else:
        ref[g, pl.ds(start, BAND, stride=dil), :] = val


def _attn_prompt_body(slope_ref, gq_ref, gk_ref,
                      q0, k0, v0, kp0, vp0, q1, k1, v1, kp1, vp1, q2, k2, v2, kp2, vp2,
                      o_ref, kn0, kn1, kn2,
                      qn_s, kn_s, vv_s, og_s, lse_s, *, sb, dils):
    first_block = pl.program_id(1) == 0
    groups = ((q0, k0, v0, kp0, vp0, kn0), (q1, k1, v1, kp1, vp1, kn1), (q2, k2, v2, kp2, vp2, kn2))
    ii = lax.broadcasted_iota(jnp.int32, (BAND, BAND), 0)
    jj = lax.broadcasted_iota(jnp.int32, (BAND, BAND), 1)
    dist_cur = (ii - jj).astype(F32)
    dist_prev = (BAND + ii - jj).astype(F32)
    ok_cur = jj <= ii
    ok_prev = jj >= ii
    neg_inf = jnp.float32(-jnp.inf)
    base = kn_s.shape[0] - sb

    for g, dil in enumerate(dils):
        q_ref, k_ref, v_ref, kp_ref, vp_ref, kn_out = groups[g]
        pr = BAND * dil
        gq = gq_ref[g:g + 1, :]
        gk = gk_ref[g:g + 1, :]
        slope = slope_ref[0, :, g * HEAD_DIM:(g + 1) * HEAD_DIM]
        qn_s[...] = _rms(q_ref[...], gq)
        kn = _rms(k_ref[...], gk)
        kn_out[...] = kn
        kn_s[pl.ds(base, sb), :] = kn
        kn_s[pl.ds(base - pr, pr), :] = _rms(kp_ref[...], gk)
        vv_s[pl.ds(base, sb), :] = v_ref[...]
        vv_s[pl.ds(base - pr, pr), :] = vp_ref[...]
        bias_cur = jnp.where(ok_cur, -slope * (dist_cur * dil), neg_inf)
        bias_prev = jnp.where(ok_prev, -slope * (dist_prev * dil), neg_inf)
        bias_prev_first = jnp.where(first_block, neg_inf, bias_prev)
        for r in range(dil):
            for s in range(sb // pr):
                c0 = s * pr + r
                qs = _ld_rows(qn_s, c0, dil).astype(BF16)
                kc = _ld_rows(kn_s, base + c0, dil).astype(BF16)
                kp = _ld_rows(kn_s, base + c0 - pr, dil).astype(BF16)
                vc = _ld_rows(vv_s, base + c0, dil).astype(BF16)
                vp = _ld_rows(vv_s, base + c0 - pr, dil).astype(BF16)
                lc = _dot_nt(qs, kc) * ATTN_SCALE + bias_cur
                lp = _dot_nt(qs, kp) * ATTN_SCALE + (bias_prev_first if s == 0 else bias_prev)
                mx = jnp.maximum(jnp.max(lc, axis=-1, keepdims=True), jnp.max(lp, axis=-1, keepdims=True))
                pc = jnp.exp(lc - mx)
                pp = jnp.exp(lp - mx)
                ssum = jnp.sum(pc, axis=-1, keepdims=True) + jnp.sum(pp, axis=-1, keepdims=True)
                o = (_dot(pc.astype(BF16), vc) + _dot(pp.astype(BF16), vp)) / ssum
                lse = mx + jnp.log(ssum)
                _st_rows(og_s, g, c0, dil, o)
                _st_rows(lse_s, g, c0, dil, jnp.broadcast_to(lse, (BAND, HEAD_DIM)))

    l0, l1, l2 = lse_s[0], lse_s[1], lse_s[2]
    mx = jnp.maximum(jnp.maximum(l0, l1), l2)
    w0, w1, w2 = jnp.exp(l0 - mx), jnp.exp(l1 - mx), jnp.exp(l2 - mx)
    o = (w0 * og_s[0] + w1 * og_s[1] + w2 * og_s[2]) / (w0 + w1 + w2)
    o_ref[...] = o.astype(o_ref.dtype)


def _alibi_slopes():
    n = N_GROUPS * A_HEADS
    e = jnp.arange(1, n + 1, dtype=F32)
    return jnp.exp2(-8.0 * e / n).reshape(N_GROUPS, A_HEADS)


def _attn_prompt(z, g_q, g_k, *, seq, sb, dils, n_heads=A_HEADS, col_q=COL_Q, col_k=COL_K, col_v=COL_V):
    nb = seq // sb
    slopes = _alibi_slopes()
    slope_arr = jnp.broadcast_to(slopes.T[:, None, :, None], (n_heads, 1, N_GROUPS, HEAD_DIM))
    slope_arr = slope_arr.reshape(n_heads, 1, N_GROUPS * HEAD_DIM)
    a_width = n_heads * HEAD_DIM
    in_specs = [pl.BlockSpec((1, 1, N_GROUPS * HEAD_DIM), lambda h, i: (h, 0, 0)),
                pl.BlockSpec((N_GROUPS, HEAD_DIM), lambda h, i: (0, 0)),
                pl.BlockSpec((N_GROUPS, HEAD_DIM), lambda h, i: (0, 0))]
    operands = [slope_arr, g_q, g_k]
    max_pr = BAND * max(dils)
    for g, dil in enumerate(dils):
        pr = BAND * dil
        ratio = sb // pr
        cq = (col_q + g * a_width) // HEAD_DIM
        ck = (col_k + g * a_width) // HEAD_DIM
        cv = (col_v + g * a_width) // HEAD_DIM
        cur = lambda c: pl.BlockSpec((sb, HEAD_DIM), lambda h, i, c=c: (i, c + h))
        prev = lambda c: pl.BlockSpec((pr, HEAD_DIM),
                                      lambda h, i, c=c, ratio=ratio: (jnp.maximum(i * ratio - 1, 0), c + h))
        in_specs += [cur(cq), cur(ck), cur(cv), prev(ck), prev(cv)]
        operands += [z, z, z, z, z]
    out_block = pl.BlockSpec((sb, HEAD_DIM), lambda h, i: (i, h))
    out_shape = (jax.ShapeDtypeStruct((seq, a_width), BF16),) + tuple(
        jax.ShapeDtypeStruct((seq, a_width), F32) for _ in dils)
    body = functools.partial(_attn_prompt_body, sb=sb, dils=tuple(dils))
    return pl.pallas_call(
        body,
        grid=(n_heads, nb),
        in_specs=in_specs,
        out_specs=(out_block,) * 4,
        out_shape=out_shape,
        scratch_shapes=[pltpu.VMEM((sb, HEAD_DIM), F32),
                        pltpu.VMEM((sb + max_pr, HEAD_DIM), F32),
                        pltpu.VMEM((sb + max_pr, HEAD_DIM), F32),
                        pltpu.VMEM((N_GROUPS, sb, HEAD_DIM), F32),
                        pltpu.VMEM((N_GROUPS, sb, HEAD_DIM), F32)],
        compiler_params=pltpu.CompilerParams(dimension_semantics=("arbitrary", "arbitrary"),
                                             vmem_limit_bytes=VMEM_LIMIT),
        name="attn_prompt",
    )(*operands)


def _sample_problems(dil, t_new):
    n_prob = min(dil, t_new)
    return n_prob, t_new // n_prob


def _sample_bias_tables(groups, t_new, n_heads):
    n_soft = len(groups) * n_heads
    tabs_c, tabs_n = [], []
    for g, (window, dil) in enumerate(groups):
        n_prob, tok_per = _sample_problems(dil, t_new)
        slopes = 2.0 ** (-8.0 * (g * n_heads + np.arange(n_heads) + 1) / n_soft)
        rows = np.arange(tok_per * n_heads)
        i, hp = rows // n_heads, rows % n_heads
        cols = np.arange(BAND * n_heads)
        m, h = cols // n_heads, cols % n_heads
        dist = window + i[:, None] * n_prob - m[None, :] * dil
        ok = (h[None, :] == hp[:, None]) & (dist <= window) & (dist > 0) & (dist % dil == 0)
        tabs_c.append(np.where(ok, -slopes[hp][:, None] * dist, -np.inf).astype(np.float32))
        cols = np.arange(t_new * n_heads)
        s, h = cols // n_heads, cols % n_heads
        per_p = []
        for p in range(n_prob):
            dist = (p + i * n_prob)[:, None] - s[None, :]
            ok = (h[None, :] == hp[:, None]) & (dist >= 0) & (dist % dil == 0)
            per_p.append(np.where(ok, -slopes[hp][:, None] * dist, -np.inf).astype(np.float32))
        tabs_n.append(np.stack(per_p))
    return tabs_c, tabs_n


def _attn_sample_body(gq_ref, gk_ref, q_ref, k_ref, v_ref, c0_ref, c1_ref, c2_ref,
                      bc0, bc1, bc2, bn0, bn1, bn2, o_ref, kn_ref, og_s, lse_s, *, t_new, groups, n_heads):
    caches = (c0_ref, c1_ref, c2_ref)
    bias_c = (bc0, bc1, bc2)
    bias_n = (bn0, bn1, bn2)
    kv_rows = 2 * n_heads
    for g, (window, dil) in enumerate(groups):
        c_ref = caches[g]
        n_prob, tok_per = _sample_problems(dil, t_new)
        qn = _rms(q_ref[0, g], gq_ref[g:g + 1, :])
        kn = _rms(k_ref[0, g], gk_ref[g:g + 1, :])
        kn_ref[0, g] = kn
        knb = kn.astype(BF16)
        vnb = v_ref[0, g].astype(BF16)
        for p in range(n_prob):
            toks = [p + i * n_prob for i in range(tok_per)]
            parts = [qn[t * n_heads:(t + 1) * n_heads] for t in toks]
            qp = (parts[0] if tok_per == 1 else jnp.concatenate(parts, axis=0)).astype(BF16)
            kc = c_ref[:, p * kv_rows:p * kv_rows + n_heads, :].reshape(BAND * n_heads, HEAD_DIM).astype(BF16)
            vc = c_ref[:, p * kv_rows + n_heads:(p + 1) * kv_rows, :].reshape(BAND * n_heads, HEAD_DIM).astype(BF16)
            lc = _dot_nt(qp, kc) * ATTN_SCALE + bias_c[g][...]
            ln = _dot_nt(qp, knb) * ATTN_SCALE + bias_n[g][p]
            mx = jnp.maximum(jnp.max(lc, axis=-1, keepdims=True), jnp.max(ln, axis=-1, keepdims=True))
            pc = jnp.exp(lc - mx)
            pn = jnp.exp(ln - mx)
            ssum = jnp.sum(pc, axis=-1, keepdims=True) + jnp.sum(pn, axis=-1, keepdims=True)
            o = (_dot(pc.astype(BF16), vc) + _dot(pn.astype(BF16), vnb)) / ssum
            lse = jnp.broadcast_to(mx + jnp.log(ssum), o.shape)
            for i, t in enumerate(toks):
                og_s[g, t * n_heads:(t + 1) * n_heads, :] = o[i * n_heads:(i + 1) * n_heads]
                lse_s[g, t * n_heads:(t + 1) * n_heads, :] = lse[i * n_heads:(i + 1) * n_heads]
    l0, l1, l2 = lse_s[0], lse_s[1], lse_s[2]
    mx = jnp.maximum(jnp.maximum(l0, l1), l2)
    w0, w1, w2 = jnp.exp(l0 - mx), jnp.exp(l1 - mx), jnp.exp(l2 - mx)
    o_ref[0] = (w0 * og_s[0] + w1 * og_s[1] + w2 * og_s[2]) / (w0 + w1 + w2)


def _attn_sample(qs, ks, vs, caches, g_q, g_k, *, n_batch, t_new, groups, n_heads=A_HEADS):
    n_g = len(groups)
    rows = t_new * n_heads
    tabs_c, tabs_n = _sample_bias_tables(groups, t_new, n_heads)
    new_spec = lambda: pl.BlockSpec((1, n_g, rows, HEAD_DIM), lambda b: (b, 0, 0, 0))
    in_specs = [pl.BlockSpec((n_g, HEAD_DIM), lambda b: (0, 0)),
                pl.BlockSpec((n_g, HEAD_DIM), lambda b: (0, 0)),
                new_spec(), new_spec(), new_spec()]
    operands = [g_q, g_k, qs, ks, vs]
    for c, (window, dil) in zip(caches, groups):
        assert c.shape[1] == window and window == BAND * dil
        n_prob, _ = _sample_problems(dil, t_new)
        c3 = c.reshape(n_batch * BAND, dil * 2 * n_heads, HEAD_DIM)
        in_specs.append(pl.BlockSpec((BAND, n_prob * 2 * n_heads, HEAD_DIM), lambda b: (b, 0, 0)))
        operands.append(c3)
    for tab in tabs_c:
        in_specs.append(pl.BlockSpec(tab.shape, lambda b: (0, 0)))
        operands.append(jnp.asarray(tab))
    for tab in tabs_n:
        in_specs.append(pl.BlockSpec(tab.shape, lambda b: (0, 0, 0)))
        operands.append(jnp.asarray(tab))
    body = functools.partial(_attn_sample_body, t_new=t_new, groups=tuple(groups), n_heads=n_heads)
    return pl.pallas_call(
        body,
        grid=(n_batch,),
        in_specs=in_specs,
        out_specs=(pl.BlockSpec((1, rows, HEAD_DIM), lambda b: (b, 0, 0)), new_spec()),
        out_shape=(jax.ShapeDtypeStruct((n_batch, rows, HEAD_DIM), F32),
                   jax.ShapeDtypeStruct((n_batch, n_g, rows, HEAD_DIM), F32)),
        scratch_shapes=[pltpu.VMEM((n_g, rows, HEAD_DIM), F32), pltpu.VMEM((n_g, rows, HEAD_DIM), F32)],
        compiler_params=pltpu.CompilerParams(dimension_semantics=("arbitrary",),
                                             vmem_limit_bytes=VMEM_LIMIT),
        name="attn_sample",
    )(*operands)


def _split3(x):
    hi = x.astype(BF16)
    r1 = x - hi.astype(F32)
    mid = r1.astype(BF16)
    lo = (r1 - mid.astype(F32)).astype(BF16)
    return hi, mid, lo


def _hgrn_heads(hq, hf, hi, hog, lb, g_out, sts, *, tb, n_valid=None):
    c = HG_CHUNK
    nh = len(sts)
    sts = list(sts)
    q = hq * _sigmoid(hq) * (HG_KDIM ** -0.5)
    gate = lb + (1.0 - lb) * _sigmoid(hf)
    log_g = jnp.log(gate)
    k = (1.0 - lb) * _sigmoid(-hf)
    if n_valid is not None:
        rows = lax.broadcasted_iota(jnp.int32, hq.shape, 0)
        log_g = jnp.where(rows < n_valid, log_g, 0.0)
        k = jnp.where(rows < n_valid, k, 0.0)
        q = jnp.where(rows < n_valid, q, 0.0)
    row = lax.broadcasted_iota(jnp.int32, (tb, tb), 0)
    col = lax.broadcasted_iota(jnp.int32, (tb, tb), 1)
    same = (row // c) == (col // c)
    tri_ok = same & (col <= row)
    tri = jnp.where(tri_ok, 1.0, 0.0).astype(BF16)
    blk = jnp.where(same, 1.0, 0.0).astype(BF16)
    p_hi, p_mid, p_lo = _split3(log_g)
    b = _dot(tri, p_hi) + _dot(tri, p_mid) + _dot(tri, p_lo)
    b_last = _dot(blk, p_hi) + _dot(blk, p_mid) + _dot(blk, p_lo)
    q_dec = (q * jnp.exp(b)).astype(BF16)
    k_dec = (k * jnp.exp(-b)).astype(BF16)
    k_end = (k * jnp.exp(b_last - b)).astype(BF16)
    decay = jnp.exp(b_last)
    vb = hi.astype(BF16)
    head = lambda x, h: x[:, h * HG_KDIM:(h + 1) * HG_KDIM]
    o_intra = []
    for h in range(nh):
        a = jnp.where(tri_ok, _dot_nt(head(q_dec, h), head(k_dec, h)), 0.0)
        o_intra.append(_dot(a.astype(BF16), head(vb, h)))
    parts = [[] for _ in range(nh)]
    for ci in range(tb // c):
        sl = slice(ci * c, (ci + 1) * c)
        for h in range(nh):
            parts[h].append(_dot_nt(head(q_dec, h)[sl], sts[h].astype(BF16)))
            sts[h] = sts[h] * head(decay, h)[ci * c:ci * c + 1, :] + _dot_tn(head(vb, h)[sl], head(k_end, h)[sl])
    outs = []
    for h in range(nh):
        o = o_intra[h] + (jnp.concatenate(parts[h], axis=0) if len(parts[h]) > 1 else parts[h][0])
        outs.append(_rms(o, g_out) * _sigmoid(head(hog, h)))
    return outs, sts


def _hgrn_prompt_body(hq_ref, hf_ref, hi_ref, hog_ref, lb_ref, go_ref, o_ref, s_ref, st_s, *, tb, nt, hb):
    t = pl.program_id(1)

    @pl.when(t == 0)
    def _():
        st_s[...] = jnp.zeros_like(st_s)

    outs, sts = _hgrn_heads(hq_ref[...], hf_ref[...], hi_ref[...], hog_ref[...], lb_ref[0], go_ref[...],
                            [st_s[hh] for hh in range(hb)], tb=tb)
    for hh in range(hb):
        o_ref[:, hh * HG_VDIM:(hh + 1) * HG_VDIM] = outs[hh].astype(o_ref.dtype)
        st_s[hh] = sts[hh]

    @pl.when(t == nt - 1)
    def _():
        for hh in range(hb):
            s_ref[hh] = sts[hh].T


def _hgrn_prompt(z, lb, g_out, *, seq, tb, hb, n_heads=HG_HEADS, col_hq=COL_HQ, col_hf=COL_HF, col_hi=COL_HI,
                 col_hog=COL_HOG):
    nt = seq // tb
    bw = hb * HG_KDIM
    blk = lambda c0: pl.BlockSpec((tb, bw), lambda h, t, c=c0 // bw: (t, c + h))
    body = functools.partial(_hgrn_prompt_body, tb=tb, nt=nt, hb=hb)
    return pl.pallas_call(
        body,
        grid=(n_heads // hb, nt),
        in_specs=[blk(col_hq), blk(col_hf), blk(col_hi), blk(col_hog),
                  pl.BlockSpec((1, 1, bw), lambda h, t: (h, 0, 0)),
                  pl.BlockSpec((1, HG_VDIM), lambda h, t: (0, 0))],
        out_specs=(pl.BlockSpec((tb, bw), lambda h, t: (t, h)),
                   pl.BlockSpec((hb, HG_KDIM, HG_VDIM), lambda h, t: (h, 0, 0))),
        out_shape=(jax.ShapeDtypeStruct((seq, n_heads * HG_VDIM), BF16),
                   jax.ShapeDtypeStruct((n_heads, HG_KDIM, HG_VDIM), F32)),
        scratch_shapes=[pltpu.VMEM((hb, HG_VDIM, HG_KDIM), F32)],
        compiler_params=pltpu.CompilerParams(dimension_semantics=("arbitrary", "arbitrary"),
                                             vmem_limit_bytes=VMEM_LIMIT),
        name="hgrn_prompt",
    )(z, z, z, z, lb.reshape(n_heads // hb, 1, bw), g_out.reshape(1, HG_VDIM))


def _hgrn_sample_body(hq_ref, hf_ref, hi_ref, hog_ref, lb_ref, go_ref, s0_ref, o_ref, s_ref, *, t_new, n_heads):
    pad = jnp.zeros((HG_CHUNK - t_new, n_heads * HG_KDIM), F32)
    ext = lambda ref: jnp.concatenate([ref[...], pad], axis=0)
    outs, sts = _hgrn_heads(ext(hq_ref), ext(hf_ref), ext(hi_ref), ext(hog_ref), lb_ref[...], go_ref[...],
                            [s0_ref[0, h].T for h in range(n_heads)], tb=HG_CHUNK, n_valid=t_new)
    for h in range(n_heads):
        o_ref[:, h * HG_VDIM:(h + 1) * HG_VDIM] = outs[h][:t_new]
        s_ref[0, h] = sts[h].T


def _hgrn_sample(hq, hf, hi, hog, lb, g_out, s0, *, n_batch, t_new, n_heads=HG_HEADS):
    width = n_heads * HG_KDIM
    row = lambda: pl.BlockSpec((t_new, width), lambda b: (b, 0))
    st = lambda: pl.BlockSpec((1, n_heads, HG_KDIM, HG_VDIM), lambda b: (b, 0, 0, 0))
    body = functools.partial(_hgrn_sample_body, t_new=t_new, n_heads=n_heads)
    return pl.pallas_call(
        body,
        grid=(n_batch,),
        in_specs=[row(), row(), row(), row(),
                  pl.BlockSpec((1, width), lambda b: (0, 0)),
                  pl.BlockSpec((1, HG_VDIM), lambda b: (0, 0)),
                  st()],
        out_specs=(row(), st()),
        out_shape=(jax.ShapeDtypeStruct((n_batch * t_new, width), F32),
                   jax.ShapeDtypeStruct(s0.shape, F32)),
        compiler_params=pltpu.CompilerParams(dimension_semantics=("arbitrary",),
                                             vmem_limit_bytes=VMEM_LIMIT),
        name="hgrn_sample",
    )(hq, hf, hi, hog, lb.reshape(1, width), g_out.reshape(1, HG_VDIM), s0)


def _tile_spec(bm, bn, col0):
    cb = col0 // bn
    return pl.BlockSpec((bm, bn), lambda j, i, *_: (i, cb + j))


def kernel(x_prompt, x_sample, cache_kv_w128, cache_kv_w512, cache_kv_w2048, state_hgrn, p_prompt, p_sample,
           g_mix, w_in, g_q, g_k, hg_lb_raw, g_hg_out, w_up_attn, w_up_hgrn, w_out, g_ffn, w_ff_up, w_ff_down,
           w_ple, w_ple_gate):
    bm = 768
    bm_big = 1408
    bn = 512
    xp, xs = x_prompt.reshape(SEQ, D_MODEL), x_sample.reshape(N_SAMPLE, D_MODEL)
    pp, ps = p_prompt.reshape(SEQ, PLE_DIM), p_sample.reshape(N_SAMPLE, PLE_DIM)
    lb = jnp.cumsum(jax.nn.softmax(hg_lb_raw.astype(F32), axis=0), axis=0)[0]
    rows = lambda width, dt: jax.ShapeDtypeStruct((M_ALL, width), dt)
    plain = lambda acc, i, extra, outs: _store(outs, acc)

    n_mix = _rmsnorm_rows(xp, g_mix[0], x_tail=xs, name="norm_mix")
    z, = _matmul(n_mix, w_in[0], bm=bm_big, bn=bn, out_shapes=(rows(IN_WIDTH, F32),), epilogue=plain,
                 name="in_proj")

    o_attn_p, kn0, kn1, kn2 = _attn_prompt(z, g_q[0], g_k[0], seq=SEQ, sb=2048,
                                           dils=tuple(d for _, d in DIL_GROUPS))
    zs = z[SEQ:]
    def by_group(col0):
        a = zs[:, col0:col0 + N_GROUPS * A_WIDTH].reshape(DEC_BATCH, DEC_SEQ, N_GROUPS, A_HEADS, HEAD_DIM)
        return a.transpose(0, 2, 1, 3, 4).reshape(DEC_BATCH, N_GROUPS, DEC_SEQ * A_HEADS, HEAD_DIM)

    caches = [c[0] for c in (cache_kv_w128, cache_kv_w512, cache_kv_w2048)]
    o_attn_s, kn_s = _attn_sample(by_group(COL_Q), by_group(COL_K), by_group(COL_V), caches,
                                  g_q[0], g_k[0], n_batch=DEC_BATCH, t_new=DEC_SEQ, groups=DIL_GROUPS)

    o_hg_p, st_p = _hgrn_prompt(z, lb, g_hg_out[0], seq=SEQ, tb=256, hb=4)
    o_hg_s, st_s = _hgrn_sample(zs[:, COL_HQ:COL_HF], zs[:, COL_HF:COL_HI], zs[:, COL_HI:COL_HOG],
                                zs[:, COL_HOG:COL_GA], lb, g_hg_out[0], state_hgrn[0],
                                n_batch=DEC_BATCH, t_new=DEC_SEQ)

    def up_attn_epilogue(acc, i, extra, outs):
        _store(outs, _sigmoid(extra[0][...]) * acc)

    def up_hgrn_epilogue(acc, i, extra, outs):
        _store(outs, extra[1][...] + _sigmoid(extra[0][...]) * acc)

    def out_proj_epilogue(acc, i, extra, outs):
        _store(outs, _with_tail(i, extra[0], extra[1]) + acc)

    t_attn, = _matmul(o_attn_p, w_up_attn[0], bm=bm_big, bn=bn,
                      a_tail=o_attn_s.reshape(N_SAMPLE, A_WIDTH).astype(BF16), extras=(z,),
                      extra_specs=(_tile_spec(bm_big, bn, COL_GA),), out_shapes=(rows(D_MODEL, F32),),
                      epilogue=up_attn_epilogue, name="up_attn")
    merged, = _matmul(o_hg_p, w_up_hgrn[0], bm=bm_big, bn=bn, a_tail=o_hg_s.astype(BF16), extras=(z, t_attn),
                      extra_specs=(_tile_spec(bm_big, bn, COL_GB), _tile_spec(bm_big, bn, 0)),
                      out_shapes=(rows(D_MODEL, BF16),), epilogue=up_hgrn_epilogue, name="up_hgrn")
    tail_spec = lambda width: pl.BlockSpec((N_SAMPLE, width), lambda j, i: (0, j))
    x1, = _matmul(merged, w_out[0], bm=bm, bn=bn, extras=(xp, xs),
                  extra_specs=(_tile_spec(bm, bn, 0), tail_spec(bn)), out_shapes=(rows(D_MODEL, F32),),
                  epilogue=out_proj_epilogue, name="out_proj")

    def ffn_up_epilogue(acc, i, extra, outs):
        _store(outs, jnp.square(jnp.maximum(acc, 0.0)))

    def ffn_down_epilogue(acc, i, extra, outs):
        x2 = extra[0][...] + acc
        _store(outs, x2, x2)

    h_ffn = _rmsnorm_rows(x1, g_ffn[0], name="norm_ffn")
    hid, = _matmul(h_ffn, w_ff_up[0], bm=bm_big, bn=bn, out_shapes=(rows(FFN_HIDDEN, BF16),),
                   epilogue=ffn_up_epilogue, name="ffn_up")
    bm_d, bn_d, bk_d = 1408, 1024, 1024
    x2, x2b = _matmul(hid, w_ff_down[0], bm=bm_d, bn=bn_d, bk=bk_d, extras=(x1,),
                      extra_specs=(_tile_spec(bm_d, bn_d, 0),),
                      out_shapes=(rows(D_MODEL, F32), rows(D_MODEL, BF16)),
                      epilogue=ffn_down_epilogue, name="ffn_down")

    def ple_epilogue(acc, i, extra, outs):
        x2_ref, pp_ref, ps_ref, wp_ref = extra
        yp_ref, ys_ref = outs
        pe = _dot(_with_tail(i, pp_ref, ps_ref).astype(BF16), wp_ref[...].astype(BF16))
        y = x2_ref[...] + _sigmoid(acc) * pe
        yp_ref[...] = y

        @pl.when(i == pl.num_programs(1) - 1)
        def _():
            ys_ref[...] = y[bm - N_SAMPLE:, :]

    y_p, y_s = _matmul(x2b, w_ple_gate[0], bm=bm, bn=bn, extras=(x2, pp, ps, w_ple[0]),
                       extra_specs=(_tile_spec(bm, bn, 0),
                                    pl.BlockSpec((bm, PLE_DIM), lambda j, i: (i, 0)),
                                    pl.BlockSpec((N_SAMPLE, PLE_DIM), lambda j, i: (0, 0)),
                                    pl.BlockSpec((PLE_DIM, bn), lambda j, i: (0, j))),
                       out_shapes=(jax.ShapeDtypeStruct((SEQ, D_MODEL), F32),
                                   jax.ShapeDtypeStruct((N_SAMPLE, D_MODEL), F32)),
                       out_specs=(pl.BlockSpec((bm, bn), lambda j, i: (i, j)),
                                  pl.BlockSpec((N_SAMPLE, bn), lambda j, i: (0, j))),
                       epilogue=ple_epilogue, name="ple")

    y_prompt = y_p.reshape(1, SEQ, D_MODEL)
    y_sample = y_s.reshape(DEC_BATCH, DEC_SEQ, D_MODEL)
    kv_p, kv_s = [], []
    for g, (window, _) in enumerate(DIL_GROUPS):
        length = min(window, SEQ)
        kn_g = (kn0, kn1, kn2)[g][SEQ - length:].reshape(length, A_HEADS, HEAD_DIM)
        v_g = z[SEQ - length:SEQ, COL_V + g * A_WIDTH:COL_V + (g + 1) * A_WIDTH].reshape(length, A_HEADS, HEAD_DIM)
        kv_p.append(jnp.stack([kn_g, v_g], axis=1)[None, None])
        ks_g = kn_s[:, g].reshape(DEC_BATCH, DEC_SEQ, A_HEADS, HEAD_DIM)
        vs_g = zs[:, COL_V + g * A_WIDTH:COL_V + (g + 1) * A_WIDTH].reshape(DEC_BATCH, DEC_SEQ, A_HEADS, HEAD_DIM)
        kv_s.append(jnp.stack([ks_g, vs_g], axis=2)[None])
    return (y_prompt, y_sample, kv_p[0], kv_p[1], kv_p[2], st_p[None, None],
            kv_s[0], kv_s[1], kv_s[2], st_s[None])
```

```python
import functools

import numpy as np
import jax
import jax.numpy as jnp
from jax import lax
from jax.experimental import pallas as pl
from jax.experimental.pallas import tpu as pltpu

F32 = jnp.float32
BF16 = jnp.bfloat16

D_MODEL = 4096
SEQ = 8192
DEC_BATCH = 32
DEC_SEQ = 8
N_SAMPLE = DEC_BATCH * DEC_SEQ
M_ALL = SEQ + N_SAMPLE
DIL_GROUPS = ((128, 1), (512, 4), (2048, 16))
N_GROUPS = 3
A_HEADS = 8
HEAD_DIM = 128
A_WIDTH = A_HEADS * HEAD_DIM
BAND = 128
HG_HEADS = 16
HG_KDIM = 128
HG_VDIM = 128
HG_WIDTH = HG_HEADS * HG_VDIM
HG_CHUNK = 32
FFN_HIDDEN = 4 * D_MODEL
PLE_DIM = 256
NORM_EPS = 1e-6
ATTN_SCALE = HEAD_DIM ** -0.5

COL_Q = 0
COL_K = COL_Q + N_GROUPS * A_WIDTH
COL_V = COL_K + N_GROUPS * A_WIDTH
COL_HQ = COL_V + N_GROUPS * A_WIDTH
COL_HF = COL_HQ + HG_HEADS * HG_KDIM
COL_HI = COL_HF + HG_HEADS * HG_KDIM
COL_HOG = COL_HI + HG_WIDTH
COL_GA = COL_HOG + HG_WIDTH
COL_GB = COL_GA + D_MODEL
IN_WIDTH = COL_GB + D_MODEL

VMEM_LIMIT = 56 * 1024 * 1024


def _dot(a, b):
    return jnp.dot(a, b, preferred_element_type=F32)


def _dot_nt(a, b):
    return lax.dot_general(a, b, (((1,), (1,)), ((), ())), preferred_element_type=F32)


def _dot_tn(a, b):
    return lax.dot_general(a, b, (((0,), (0,)), ((), ())), preferred_element_type=F32)


def _rms(x, g):
    return x * lax.rsqrt(jnp.mean(x * x, axis=-1, keepdims=True) + NORM_EPS) * g


def _sigmoid(x):
    return 1.0 / (1.0 + jnp.exp(-x))


def _store(outs, *vals):
    for o_ref, v in zip(outs, vals):
        o_ref[...] = v.astype(o_ref.dtype)


_CAST_ROWS = 128


def _mm_rows_body(*refs, needs_cast, n_extra, n_out, epilogue):
    n_pairs = len(needs_cast)
    a_refs = refs[0:2 * n_pairs:2]
    w_refs = refs[1:2 * n_pairs:2]
    pos = 2 * n_pairs
    extra = refs[pos:pos + n_extra]
    outs = refs[pos + n_extra:pos + n_extra + n_out]
    scratch = list(refs[pos + n_extra + n_out:])
    i, j = pl.program_id(0), pl.program_id(1)
    accs = []
    for a_ref, w_ref, cast in zip(a_refs, w_refs, needs_cast):
        if cast:
            ab_ref = scratch.pop(0)

            @pl.when(j == 0)
            def _(a_ref=a_ref, ab_ref=ab_ref):
                def rows(c, carry):
                    r0 = pl.multiple_of(c * _CAST_ROWS, _CAST_ROWS)
                    ab_ref[pl.ds(r0, _CAST_ROWS), :] = a_ref[pl.ds(r0, _CAST_ROWS), :].astype(BF16)
                    return carry
                lax.fori_loop(0, a_ref.shape[0] // _CAST_ROWS, rows, 0)

            a_val = ab_ref[...]
        else:
            a_val = a_ref[...]
        accs.append(_dot(a_val, w_ref[...].astype(BF16)))
    epilogue(accs, i, j, a_refs, extra, outs)


def _matmul_rows(pairs, *, bm, bn, extras=(), extra_specs=(), out_shapes, out_specs=None, epilogue, name):
    m = pairs[0][0].shape[0]
    n = pairs[0][1].shape[1]
    assert m % bm == 0 and n % bn == 0 and bm % _CAST_ROWS == 0
    in_specs, operands, scratch, needs_cast = [], [], [], []
    for a, w in pairs:
        kdim = w.shape[0]
        assert a.shape == (m, kdim) and w.shape[1] == n
        in_specs += [pl.BlockSpec((bm, kdim), lambda i, j: (i, 0), pipeline_mode=pl.Buffered(1)),
                     pl.BlockSpec((kdim, bn), lambda i, j: (0, j))]
        operands += [a, w]
        needs_cast.append(a.dtype != BF16)
        if needs_cast[-1]:
            scratch.append(pltpu.VMEM((bm, kdim), BF16))
    if out_specs is None:
        out_specs = tuple(pl.BlockSpec((bm, bn), lambda i, j: (i, j)) for _ in out_shapes)
    body = functools.partial(_mm_rows_body, needs_cast=tuple(needs_cast), n_extra=len(extras),
                             n_out=len(out_shapes), epilogue=epilogue)
    return pl.pallas_call(
        body,
        grid=(m // bm, n // bn),
        in_specs=in_specs + list(extra_specs),
        out_specs=tuple(out_specs),
        out_shape=tuple(out_shapes),
        scratch_shapes=scratch,
        compiler_params=pltpu.CompilerParams(dimension_semantics=("arbitrary", "arbitrary"),
                                             vmem_limit_bytes=VMEM_LIMIT),
        name=name,
    )(*operands, *extras)


def _mm_ksplit_body(a_ref, w_ref, r_ref, o_ref):
    @pl.when(pl.program_id(2) == 0)
    def _():
        o_ref[...] = r_ref[...]

    o_ref[...] += _dot(a_ref[...], w_ref[...].astype(BF16))


def _matmul_ksplit(a, w, resid, *, bm, bn, bk, name):
    m, kdim = a.shape
    n = w.shape[1]
    assert m % bm == 0 and n % bn == 0 and kdim % bk == 0 and a.dtype == BF16
    return pl.pallas_call(
        _mm_ksplit_body,
        grid=(n // bn, m // bm, kdim // bk),
        in_specs=[pl.BlockSpec((bm, bk), lambda j, i, k: (i, k)),
                  pl.BlockSpec((bk, bn), lambda j, i, k: (k, j)),
                  pl.BlockSpec((bm, bn), lambda j, i, k: (i, j), pipeline_mode=pl.Buffered(1))],
        out_specs=pl.BlockSpec((bm, bn), lambda j, i, k: (i, j)),
        out_shape=jax.ShapeDtypeStruct((m, n), F32),
        compiler_params=pltpu.CompilerParams(dimension_semantics=("arbitrary", "arbitrary", "arbitrary"),
                                             vmem_limit_bytes=VMEM_LIMIT),
        name=name,
    )(a, w, resid)


def _rmsnorm_body(*refs, n_main):
    g_ref, o_ref = refs[-2:]
    if n_main is None:
        o_ref[...] = _rms(refs[0][...], g_ref[...]).astype(o_ref.dtype)
        return
    i = pl.program_id(0)

    @pl.when(i < n_main)
    def _():
        o_ref[...] = _rms(refs[0][...], g_ref[...]).astype(o_ref.dtype)

    @pl.when(i >= n_main)
    def _():
        o_ref[...] = _rms(refs[1][...], g_ref[...]).astype(o_ref.dtype)


def _rmsnorm_rows(x, g, *, x_tail=None, bm=256, name="rmsnorm"):
    m, d = x.shape
    assert m % bm == 0
    n_main = m // bm
    in_specs = [pl.BlockSpec((bm, d), lambda i: (jnp.minimum(i, n_main - 1), 0))]
    operands = [x]
    if x_tail is not None:
        assert x_tail.shape[0] % bm == 0
        in_specs.append(pl.BlockSpec((bm, d), lambda i: (jnp.maximum(i - n_main, 0), 0)))
        operands.append(x_tail)
        m += x_tail.shape[0]
    in_specs.append(pl.BlockSpec((1, d), lambda i: (0, 0)))
    return pl.pallas_call(
        functools.partial(_rmsnorm_body, n_main=None if x_tail is None else n_main),
        grid=(m // bm,),
        in_specs=in_specs,
        out_specs=pl.BlockSpec((bm, d), lambda i: (i, 0)),
        out_shape=jax.ShapeDtypeStruct((m, d), BF16),
        compiler_params=pltpu.CompilerParams(dimension_semantics=("arbitrary",)),
        name=name,
    )(*operands, g.reshape(1, d))


def _ld_rows(ref, start, dil):
    if dil == 1:
        return ref[pl.ds(start, BAND), :]
    return ref[pl.ds(start, BAND, stride=dil), :]


def _st_rows(ref, g, start, dil, val):
    if dil == 1:
        ref[g, pl.ds(start, BAND), :] = val
    else:
        ref[g, pl.ds(start, BAND, stride=dil), :] = val


def _attn_prompt_body(slope_ref, gq_ref, gk_ref,
                      q0, k0, v0, kp0, vp0, q1, k1, v1, kp1, vp1, q2, k2, v2, kp2, vp2,
                      o_ref, kn0, kn1, kn2,
                      qn_s, kn_s, vv_s, og_s, lse_s, *, sb, dils):
    first_block = pl.program_id(1) == 0
    groups = ((q0, k0, v0, kp0, vp0, kn0), (q1, k1, v1, kp1, vp1, kn1), (q2, k2, v2, kp2, vp2, kn2))
    ii = lax.broadcasted_iota(jnp.int32, (BAND, BAND), 0)
    jj = lax.broadcasted_iota(jnp.int32, (BAND, BAND), 1)
    dist_cur = (ii - jj).astype(F32)
    dist_prev = (BAND + ii - jj).astype(F32)
    ok_cur = jj <= ii
    ok_prev = jj >= ii
    neg_inf = jnp.float32(-jnp.inf)
    base = kn_s.shape[0] - sb

    for g, dil in enumerate(dils):
        q_ref, k_ref, v_ref, kp_ref, vp_ref, kn_out = groups[g]
        pr = BAND * dil
        gq = gq_ref[g:g + 1, :]
        gk = gk_ref[g:g + 1, :]
        slope = slope_ref[0, :, g * HEAD_DIM:(g + 1) * HEAD_DIM]
        qn_s[...] = _rms(q_ref[...], gq)
        kn = _rms(k_ref[...], gk)
        kn_out[...] = kn
        kn_s[pl.ds(base, sb), :] = kn
        kn_s[pl.ds(base - pr, pr), :] = _rms(kp_ref[...], gk)
        vv_s[pl.ds(base, sb), :] = v_ref[...]
        vv_s[pl.ds(base - pr, pr), :] = vp_ref[...]
        bias_cur = jnp.where(ok_cur, -slope * (dist_cur * dil), neg_inf)
        bias_prev = jnp.where(ok_prev, -slope * (dist_prev * dil), neg_inf)
        bias_prev_first = jnp.where(first_block, neg_inf, bias_prev)
        for r in range(dil):
            for s in range(sb // pr):
                c0 = s * pr + r
                qs = _ld_rows(qn_s, c0, dil).astype(BF16)
                kc = _ld_rows(kn_s, base + c0, dil).astype(BF16)
                kp = _ld_rows(kn_s, base + c0 - pr, dil).astype(BF16)
                vc = _ld_rows(vv_s, base + c0, dil).astype(BF16)
                vp = _ld_rows(vv_s, base + c0 - pr, dil).astype(BF16)
                lc = _dot_nt(qs, kc) * ATTN_SCALE + bias_cur
                lp = _dot_nt(qs, kp) * ATTN_SCALE + (bias_prev_first if s == 0 else bias_prev)
                mx = jnp.maximum(jnp.max(lc, axis=-1, keepdims=True), jnp.max(lp, axis=-1, keepdims=True))
                pc = jnp.exp(lc - mx)
                pp = jnp.exp(lp - mx)
                ssum = jnp.sum(pc, axis=-1, keepdims=True) + jnp.sum(pp, axis=-1, keepdims=True)
                o = (_dot(pc.astype(BF16), vc) + _dot(pp.astype(BF16), vp)) / ssum
                lse = mx + jnp.log(ssum)
                _st_rows(og_s, g, c0, dil, o)
                _st_rows(lse_s, g, c0, dil, jnp.broadcast_to(lse, (BAND, HEAD_DIM)))

    l0, l1, l2 = lse_s[0], lse_s[1], lse_s[2]
    mx = jnp.maximum(jnp.maximum(l0, l1), l2)
    w0, w1, w2 = jnp.exp(l0 - mx), jnp.exp(l1 - mx), jnp.exp(l2 - mx)
    o = (w0 * og_s[0] + w1 * og_s[1] + w2 * og_s[2]) / (w0 + w1 + w2)
    o_ref[...] = o.astype(o_ref.dtype)


def _alibi_slopes():
    n = N_GROUPS * A_HEADS
    e = jnp.arange(1, n + 1, dtype=F32)
    return jnp.exp2(-8.0 * e / n).reshape(N_GROUPS, A_HEADS)


def _attn_prompt(z, g_q, g_k, *, seq, sb, dils, out_rows=None, n_heads=A_HEADS, col_q=COL_Q, col_k=COL_K,
                 col_v=COL_V):
    out_rows = seq if out_rows is None else out_rows
    nb = seq // sb
    slopes = _alibi_slopes()
    slope_arr = jnp.broadcast_to(slopes.T[:, None, :, None], (n_heads, 1, N_GROUPS, HEAD_DIM))
    slope_arr = slope_arr.reshape(n_heads, 1, N_GROUPS * HEAD_DIM)
    a_width = n_heads * HEAD_DIM
    in_specs = [pl.BlockSpec((1, 1, N_GROUPS * HEAD_DIM), lambda h, i: (h, 0, 0)),
                pl.BlockSpec((N_GROUPS, HEAD_DIM), lambda h, i: (0, 0)),
                pl.BlockSpec((N_GROUPS, HEAD_DIM), lambda h, i: (0, 0))]
    operands = [slope_arr, g_q, g_k]
    max_pr = BAND * max(dils)
    for g, dil in enumerate(dils):
        pr = BAND * dil
        ratio = sb // pr
        cq = (col_q + g * a_width) // HEAD_DIM
        ck = (col_k + g * a_width) // HEAD_DIM
        cv = (col_v + g * a_width) // HEAD_DIM
        cur = lambda c: pl.BlockSpec((sb, HEAD_DIM), lambda h, i, c=c: (i, c + h))
        prev = lambda c: pl.BlockSpec((pr, HEAD_DIM),
                                      lambda h, i, c=c, ratio=ratio: (jnp.maximum(i * ratio - 1, 0), c + h))
        in_specs += [cur(cq), cur(ck), cur(cv), prev(ck), prev(cv)]
        operands += [z, z, z, z, z]
    out_block = pl.BlockSpec((sb, HEAD_DIM), lambda h, i: (i, h))
    out_shape = (jax.ShapeDtypeStruct((out_rows, a_width), BF16),) + tuple(
        jax.ShapeDtypeStruct((seq, a_width), F32) for _ in dils)
    body = functools.partial(_attn_prompt_body, sb=sb, dils=tuple(dils))
    return pl.pallas_call(
        body,
        grid=(n_heads, nb),
        in_specs=in_specs,
        out_specs=(out_block,) * 4,
        out_shape=out_shape,
        scratch_shapes=[pltpu.VMEM((sb, HEAD_DIM), F32),
                        pltpu.VMEM((sb + max_pr, HEAD_DIM), F32),
                        pltpu.VMEM((sb + max_pr, HEAD_DIM), F32),
                        pltpu.VMEM((N_GROUPS, sb, HEAD_DIM), F32),
                        pltpu.VMEM((N_GROUPS, sb, HEAD_DIM), F32)],
        compiler_params=pltpu.CompilerParams(dimension_semantics=("arbitrary", "arbitrary"),
                                             vmem_limit_bytes=VMEM_LIMIT),
        name="attn_prompt",
    )(*operands)


def _sample_problems(dil, t_new):
    n_prob = min(dil, t_new)
    return n_prob, t_new // n_prob


def _sample_bias_tables(groups, t_new, n_heads):
    n_soft = len(groups) * n_heads
    tabs_c, tabs_n = [], []
    for g, (window, dil) in enumerate(groups):
        n_prob, tok_per = _sample_problems(dil, t_new)
        slopes = 2.0 ** (-8.0 * (g * n_heads + np.arange(n_heads) + 1) / n_soft)
        rows = np.arange(tok_per * n_heads)
        i, hp = rows // n_heads, rows % n_heads
        cols = np.arange(BAND * n_heads)
        m, h = cols // n_heads, cols % n_heads
        dist = window + i[:, None] * n_prob - m[None, :] * dil
        ok = (h[None, :] == hp[:, None]) & (dist <= window) & (dist > 0) & (dist % dil == 0)
        tabs_c.append(np.where(ok, -slopes[hp][:, None] * dist, -np.inf).astype(np.float32))
        cols = np.arange(t_new * n_heads)
        s, h = cols // n_heads, cols % n_heads
        per_p = []
        for p in range(n_prob):
            dist = (p + i * n_prob)[:, None] - s[None, :]
            ok = (h[None, :] == hp[:, None]) & (dist >= 0) & (dist % dil == 0)
            per_p.append(np.where(ok, -slopes[hp][:, None] * dist, -np.inf).astype(np.float32))
        tabs_n.append(np.stack(per_p))
    return tabs_c, tabs_n


def _attn_sample_body(gq_ref, gk_ref, q_ref, k_ref, v_ref, c0_ref, c1_ref, c2_ref,
                      bc0, bc1, bc2, bn0, bn1, bn2, o_ref, kn_ref, og_s, lse_s, *, t_new, groups, n_heads):
    caches = (c0_ref, c1_ref, c2_ref)
    bias_c = (bc0, bc1, bc2)
    bias_n = (bn0, bn1, bn2)
    kv_rows = 2 * n_heads
    for g, (window, dil) in enumerate(groups):
        c_ref = caches[g]
        n_prob, tok_per = _sample_problems(dil, t_new)
        qn = _rms(q_ref[0, g], gq_ref[g:g + 1, :])
        kn = _rms(k_ref[0, g], gk_ref[g:g + 1, :])
        kn_ref[0, g] = kn
        knb = kn.astype(BF16)
        vnb = v_ref[0, g].astype(BF16)
        for p in range(n_prob):
            toks = [p + i * n_prob for i in range(tok_per)]
            parts = [qn[t * n_heads:(t + 1) * n_heads] for t in toks]
            qp = (parts[0] if tok_per == 1 else jnp.concatenate(parts, axis=0)).astype(BF16)
            kc = c_ref[:, p * kv_rows:p * kv_rows + n_heads, :].reshape(BAND * n_heads, HEAD_DIM).astype(BF16)
            vc = c_ref[:, p * kv_rows + n_heads:(p + 1) * kv_rows, :].reshape(BAND * n_heads, HEAD_DIM).astype(BF16)
            lc = _dot_nt(qp, kc) * ATTN_SCALE + bias_c[g][...]
            ln = _dot_nt(qp, knb) * ATTN_SCALE + bias_n[g][p]
            mx = jnp.maximum(jnp.max(lc, axis=-1, keepdims=True), jnp.max(ln, axis=-1, keepdims=True))
            pc = jnp.exp(lc - mx)
            pn = jnp.exp(ln - mx)
            ssum = jnp.sum(pc, axis=-1, keepdims=True) + jnp.sum(pn, axis=-1, keepdims=True)
            o = (_dot(pc.astype(BF16), vc) + _dot(pn.astype(BF16), vnb)) / ssum
            lse = jnp.broadcast_to(mx + jnp.log(ssum), o.shape)
            for i, t in enumerate(toks):
                og_s[g, t * n_heads:(t + 1) * n_heads, :] = o[i * n_heads:(i + 1) * n_heads]
                lse_s[g, t * n_heads:(t + 1) * n_heads, :] = lse[i * n_heads:(i + 1) * n_heads]
    l0, l1, l2 = lse_s[0], lse_s[1], lse_s[2]
    mx = jnp.maximum(jnp.maximum(l0, l1), l2)
    w0, w1, w2 = jnp.exp(l0 - mx), jnp.exp(l1 - mx), jnp.exp(l2 - mx)
    o_ref[0] = (w0 * og_s[0] + w1 * og_s[1] + w2 * og_s[2]) / (w0 + w1 + w2)


def _attn_sample(qs, ks, vs, caches, g_q, g_k, *, n_batch, t_new, groups, n_heads=A_HEADS):
    n_g = len(groups)
    rows = t_new * n_heads
    tabs_c, tabs_n = _sample_bias_tables(groups, t_new, n_heads)
    new_spec = lambda: pl.BlockSpec((1, n_g, rows, HEAD_DIM), lambda b: (b, 0, 0, 0))
    in_specs = [pl.BlockSpec((n_g, HEAD_DIM), lambda b: (0, 0)),
                pl.BlockSpec((n_g, HEAD_DIM), lambda b: (0, 0)),
                new_spec(), new_spec(), new_spec()]
    operands = [g_q, g_k, qs, ks, vs]
    for c, (window, dil) in zip(caches, groups):
        assert c.shape[1] == window and window == BAND * dil
        n_prob, _ = _sample_problems(dil, t_new)
        c3 = c.reshape(n_batch * BAND, dil * 2 * n_heads, HEAD_DIM)
        in_specs.append(pl.BlockSpec((BAND, n_prob * 2 * n_heads, HEAD_DIM), lambda b: (b, 0, 0)))
        operands.append(c3)
    for tab in tabs_c:
        in_specs.append(pl.BlockSpec(tab.shape, lambda b: (0, 0)))
        operands.append(jnp.asarray(tab))
    for tab in tabs_n:
        in_specs.append(pl.BlockSpec(tab.shape, lambda b: (0, 0, 0)))
        operands.append(jnp.asarray(tab))
    body = functools.partial(_attn_sample_body, t_new=t_new, groups=tuple(groups), n_heads=n_heads)
    return pl.pallas_call(
        body,
        grid=(n_batch,),
        in_specs=in_specs,
        out_specs=(pl.BlockSpec((1, rows, HEAD_DIM), lambda b: (b, 0, 0)), new_spec()),
        out_shape=(jax.ShapeDtypeStruct((n_batch, rows, HEAD_DIM), F32),
                   jax.ShapeDtypeStruct((n_batch, n_g, rows, HEAD_DIM), F32)),
        scratch_shapes=[pltpu.VMEM((n_g, rows, HEAD_DIM), F32), pltpu.VMEM((n_g, rows, HEAD_DIM), F32)],
        compiler_params=pltpu.CompilerParams(dimension_semantics=("arbitrary",),
                                             vmem_limit_bytes=VMEM_LIMIT),
        name="attn_sample",
    )(*operands)


def _split3(x):
    hi = x.astype(BF16)
    r1 = x - hi.astype(F32)
    mid = r1.astype(BF16)
    lo = (r1 - mid.astype(F32)).astype(BF16)
    return hi, mid, lo


def _hgrn_heads(hq, hf, hi, hog, lb, g_out, sts, *, tb, n_valid=None):
    c = HG_CHUNK
    nh = len(sts)
    sts = list(sts)
    q = hq * _sigmoid(hq) * (HG_KDIM ** -0.5)
    gate = lb + (1.0 - lb) * _sigmoid(hf)
    log_g = jnp.log(gate)
    k = (1.0 - lb) * _sigmoid(-hf)
    if n_valid is not None:
        rows = lax.broadcasted_iota(jnp.int32, hq.shape, 0)
        log_g = jnp.where(rows < n_valid, log_g, 0.0)
        k = jnp.where(rows < n_valid, k, 0.0)
        q = jnp.where(rows < n_valid, q, 0.0)
    row = lax.broadcasted_iota(jnp.int32, (tb, tb), 0)
    col = lax.broadcasted_iota(jnp.int32, (tb, tb), 1)
    same = (row // c) == (col // c)
    tri_ok = same & (col <= row)
    tri = jnp.where(tri_ok, 1.0, 0.0).astype(BF16)
    blk = jnp.where(same, 1.0, 0.0).astype(BF16)
    p_hi, p_mid, p_lo = _split3(log_g)
    b = _dot(tri, p_hi) + _dot(tri, p_mid) + _dot(tri, p_lo)
    b_last = _dot(blk, p_hi) + _dot(blk, p_mid) + _dot(blk, p_lo)
    q_dec = (q * jnp.exp(b)).astype(BF16)
    k_dec = (k * jnp.exp(-b)).astype(BF16)
    k_end = (k * jnp.exp(b_last - b)).astype(BF16)
    decay = jnp.exp(b_last)
    vb = hi.astype(BF16)
    head = lambda x, h: x[:, h * HG_KDIM:(h + 1) * HG_KDIM]
    o_intra = []
    for h in range(nh):
        a = jnp.where(tri_ok, _dot_nt(head(q_dec, h), head(k_dec, h)), 0.0)
        o_intra.append(_dot(a.astype(BF16), head(vb, h)))
    parts = [[] for _ in range(nh)]
    for ci in range(tb // c):
        sl = slice(ci * c, (ci + 1) * c)
        for h in range(nh):
            parts[h].append(_dot_nt(head(q_dec, h)[sl], sts[h].astype(BF16)))
            sts[h] = sts[h] * head(decay, h)[ci * c:ci * c + 1, :] + _dot_tn(head(vb, h)[sl], head(k_end, h)[sl])
    outs = []
    for h in range(nh):
        o = o_intra[h] + (jnp.concatenate(parts[h], axis=0) if len(parts[h]) > 1 else parts[h][0])
        outs.append(_rms(o, g_out) * _sigmoid(head(hog, h)))
    return outs, sts


def _hgrn_prompt_body(hq_ref, hf_ref, hi_ref, hog_ref, lb_ref, go_ref, o_ref, s_ref, st_s, *, tb, nt, hb):
    t = pl.program_id(1)

    @pl.when(t == 0)
    def _():
        st_s[...] = jnp.zeros_like(st_s)

    outs, sts = _hgrn_heads(hq_ref[...], hf_ref[...], hi_ref[...], hog_ref[...], lb_ref[0], go_ref[...],
                            [st_s[hh] for hh in range(hb)], tb=tb)
    for hh in range(hb):
        o_ref[:, hh * HG_VDIM:(hh + 1) * HG_VDIM] = outs[hh].astype(o_ref.dtype)
        st_s[hh] = sts[hh]

    @pl.when(t == nt - 1)
    def _():
        for hh in range(hb):
            s_ref[hh] = sts[hh].T


def _hgrn_prompt(z, lb, g_out, *, seq, tb, hb, out_rows=None, n_heads=HG_HEADS, col_hq=COL_HQ, col_hf=COL_HF,
                 col_hi=COL_HI, col_hog=COL_HOG):
    out_rows = seq if out_rows is None else out_rows
    nt = seq // tb
    bw = hb * HG_KDIM
    blk = lambda c0: pl.BlockSpec((tb, bw), lambda h, t, c=c0 // bw: (t, c + h))
    body = functools.partial(_hgrn_prompt_body, tb=tb, nt=nt, hb=hb)
    return pl.pallas_call(
        body,
        grid=(n_heads // hb, nt),
        in_specs=[blk(col_hq), blk(col_hf), blk(col_hi), blk(col_hog),
                  pl.BlockSpec((1, 1, bw), lambda h, t: (h, 0, 0)),
                  pl.BlockSpec((1, HG_VDIM), lambda h, t: (0, 0))],
        out_specs=(pl.BlockSpec((tb, bw), lambda h, t: (t, h)),
                   pl.BlockSpec((hb, HG_KDIM, HG_VDIM), lambda h, t: (h, 0, 0))),
        out_shape=(jax.ShapeDtypeStruct((out_rows, n_heads * HG_VDIM), BF16),
                   jax.ShapeDtypeStruct((n_heads, HG_KDIM, HG_VDIM), F32)),
        scratch_shapes=[pltpu.VMEM((hb, HG_VDIM, HG_KDIM), F32)],
        compiler_params=pltpu.CompilerParams(dimension_semantics=("arbitrary", "arbitrary"),
                                             vmem_limit_bytes=VMEM_LIMIT),
        name="hgrn_prompt",
    )(z, z, z, z, lb.reshape(n_heads // hb, 1, bw), g_out.reshape(1, HG_VDIM))


def _hgrn_sample_body(hq_ref, hf_ref, hi_ref, hog_ref, lb_ref, go_ref, s0_ref, o_ref, s_ref, *, t_new, n_heads):
    pad = jnp.zeros((HG_CHUNK - t_new, n_heads * HG_KDIM), F32)
    ext = lambda ref: jnp.concatenate([ref[...], pad], axis=0)
    outs, sts = _hgrn_heads(ext(hq_ref), ext(hf_ref), ext(hi_ref), ext(hog_ref), lb_ref[...], go_ref[...],
                            [s0_ref[0, h].T for h in range(n_heads)], tb=HG_CHUNK, n_valid=t_new)
    for h in range(n_heads):
        o_ref[:, h * HG_VDIM:(h + 1) * HG_VDIM] = outs[h][:t_new]
        s_ref[0, h] = sts[h].T


def _hgrn_sample(hq, hf, hi, hog, lb, g_out, s0, *, n_batch, t_new, n_heads=HG_HEADS):
    width = n_heads * HG_KDIM
    row = lambda: pl.BlockSpec((t_new, width), lambda b: (b, 0))
    st = lambda: pl.BlockSpec((1, n_heads, HG_KDIM, HG_VDIM), lambda b: (b, 0, 0, 0))
    body = functools.partial(_hgrn_sample_body, t_new=t_new, n_heads=n_heads)
    return pl.pallas_call(
        body,
        grid=(n_batch,),
        in_specs=[row(), row(), row(), row(),
                  pl.BlockSpec((1, width), lambda b: (0, 0)),
                  pl.BlockSpec((1, HG_VDIM), lambda b: (0, 0)),
                  st()],
        out_specs=(row(), st()),
        out_shape=(jax.ShapeDtypeStruct((n_batch * t_new, width), F32),
                   jax.ShapeDtypeStruct(s0.shape, F32)),
        compiler_params=pltpu.CompilerParams(dimension_semantics=("arbitrary",),
                                             vmem_limit_bytes=VMEM_LIMIT),
        name="hgrn_sample",
    )(hq, hf, hi, hog, lb.reshape(1, width), g_out.reshape(1, HG_VDIM), s0)


def _tile_spec(bm, bn, col0=0):
    cb = col0 // bn
    return pl.BlockSpec((bm, bn), lambda i, j: (i, cb + j))


def kernel(x_prompt, x_sample, cache_kv_w128, cache_kv_w512, cache_kv_w2048, state_hgrn, p_prompt, p_sample,
           g_mix, w_in, g_q, g_k, hg_lb_raw, g_hg_out, w_up_attn, w_up_hgrn, w_out, g_ffn, w_ff_up, w_ff_down,
           w_ple, w_ple_gate):
    bm, bn = 2816, 256
    bm_ple = 1408
    bm_res = 1408
    xp, xs = x_prompt.reshape(SEQ, D_MODEL), x_sample.reshape(N_SAMPLE, D_MODEL)
    pp, ps = p_prompt.reshape(SEQ, PLE_DIM), p_sample.reshape(N_SAMPLE, PLE_DIM)
    lb = jnp.cumsum(jax.nn.softmax(hg_lb_raw.astype(F32), axis=0), axis=0)[0]
    rows = lambda width, dt: jax.ShapeDtypeStruct((M_ALL, width), dt)

    def plain(accs, i, j, a_refs, extra, outs):
        _store(outs, accs[0])

    n_mix = _rmsnorm_rows(xp, g_mix[0], x_tail=xs, name="norm_mix")
    z, = _matmul_rows([(n_mix, w_in[0])], bm=bm, bn=bn, out_shapes=(rows(IN_WIDTH, F32),), epilogue=plain,
                      name="in_proj")

    o_attn, kn0, kn1, kn2 = _attn_prompt(z, g_q[0], g_k[0], seq=SEQ, sb=2048, out_rows=M_ALL,
                                         dils=tuple(d for _, d in DIL_GROUPS))
    zs = z[SEQ:]
    def by_group(col0):
        a = zs[:, col0:col0 + N_GROUPS * A_WIDTH].reshape(DEC_BATCH, DEC_SEQ, N_GROUPS, A_HEADS, HEAD_DIM)
        return a.transpose(0, 2, 1, 3, 4).reshape(DEC_BATCH, N_GROUPS, DEC_SEQ * A_HEADS, HEAD_DIM)

    caches = [c[0] for c in (cache_kv_w128, cache_kv_w512, cache_kv_w2048)]
    o_attn_s, kn_s = _attn_sample(by_group(COL_Q), by_group(COL_K), by_group(COL_V), caches,
                                  g_q[0], g_k[0], n_batch=DEC_BATCH, t_new=DEC_SEQ, groups=DIL_GROUPS)
    o_attn = lax.dynamic_update_slice(o_attn, o_attn_s.reshape(N_SAMPLE, A_WIDTH).astype(BF16), (SEQ, 0))

    o_hg, st_p = _hgrn_prompt(z, lb, g_hg_out[0], seq=SEQ, tb=256, hb=4, out_rows=M_ALL)
    o_hg_s, st_s = _hgrn_sample(zs[:, COL_HQ:COL_HF], zs[:, COL_HF:COL_HI], zs[:, COL_HI:COL_HOG],
                                zs[:, COL_HOG:COL_GA], lb, g_hg_out[0], state_hgrn[0],
                                n_batch=DEC_BATCH, t_new=DEC_SEQ)
    o_hg = lax.dynamic_update_slice(o_hg, o_hg_s.astype(BF16), (SEQ, 0))

    def merge_epilogue(accs, i, j, a_refs, extra, outs):
        _store(outs, _sigmoid(extra[0][...]) * accs[0] + _sigmoid(extra[1][...]) * accs[1])

    def out_proj_epilogue(accs, i, j, a_refs, extra, outs):
        xp_ref, xs_ref = extra
        outs[0][...] = xp_ref[...] + accs[0]

        @pl.when(i == pl.num_programs(0) - 1)
        def _():
            outs[0][bm_res - N_SAMPLE:, :] = xs_ref[...] + accs[0][bm_res - N_SAMPLE:, :]

    merged, = _matmul_rows([(o_attn, w_up_attn[0]), (o_hg, w_up_hgrn[0])], bm=bm, bn=bn, extras=(z, z),
                           extra_specs=(_tile_spec(bm, bn, COL_GA), _tile_spec(bm, bn, COL_GB)),
                           out_shapes=(rows(D_MODEL, BF16),), epilogue=merge_epilogue, name="up_merge")
    x1, = _matmul_rows([(merged, w_out[0])], bm=bm_res, bn=bn, extras=(xp, xs),
                       extra_specs=(_tile_spec(bm_res, bn), pl.BlockSpec((N_SAMPLE, bn), lambda i, j: (0, j))),
                       out_shapes=(rows(D_MODEL, F32),), epilogue=out_proj_epilogue, name="out_proj")

    def ffn_up_epilogue(accs, i, j, a_refs, extra, outs):
        _store(outs, jnp.square(jnp.maximum(accs[0], 0.0)))

    h_ffn = _rmsnorm_rows(x1, g_ffn[0], name="norm_ffn")
    hid, = _matmul_rows([(h_ffn, w_ff_up[0])], bm=bm, bn=bn, out_shapes=(rows(FFN_HIDDEN, BF16),),
                        epilogue=ffn_up_epilogue, name="ffn_up")
    x2 = _matmul_ksplit(hid, w_ff_down[0], x1, bm=1408, bn=2048, bk=512, name="ffn_down")

    n_ple_blocks = M_ALL // bm_ple

    def ple_epilogue(accs, i, j, a_refs, extra, outs):
        pp_ref, ps_ref, wp_ref = extra
        yp_ref, ys_ref = outs
        x2_tile = a_refs[0][:, pl.ds(pl.multiple_of(j * bn, bn), bn)]
        gate = _sigmoid(accs[0])
        wp = wp_ref[...].astype(BF16)
        yp_ref[...] = x2_tile + gate * _dot(pp_ref[...].astype(BF16), wp)

        @pl.when(i == n_ple_blocks - 1)
        def _():
            s0 = bm_ple - N_SAMPLE
            ys_ref[...] = x2_tile[s0:, :] + gate[s0:, :] * _dot(ps_ref[...].astype(BF16), wp)

    y_p, y_s = _matmul_rows(
        [(x2, w_ple_gate[0])], bm=bm_ple, bn=bn, extras=(pp, ps, w_ple[0]),
        extra_specs=(pl.BlockSpec((bm_ple, PLE_DIM), lambda i, j: (i, 0)),
                     pl.BlockSpec((N_SAMPLE, PLE_DIM), lambda i, j: (0, 0)),
                     pl.BlockSpec((PLE_DIM, bn), lambda i, j: (0, j))),
        out_shapes=(jax.ShapeDtypeStruct((SEQ, D_MODEL), F32), jax.ShapeDtypeStruct((N_SAMPLE, D_MODEL), F32)),
        out_specs=(pl.BlockSpec((bm_ple, bn), lambda i, j: (i, j)),
                   pl.BlockSpec((N_SAMPLE, bn), lambda i, j: (0, jnp.where(i == n_ple_blocks - 1, j, 0)))),
        epilogue=ple_epilogue, name="ple")

    y_prompt = y_p.reshape(1, SEQ, D_MODEL)
    y_sample = y_s.reshape(DEC_BATCH, DEC_SEQ, D_MODEL)
    kv_p, kv_s = [], []
    for g, (window, _) in enumerate(DIL_GROUPS):
        length = min(window, SEQ)
        kn_g = (kn0, kn1, kn2)[g][SEQ - length:].reshape(length, A_HEADS, HEAD_DIM)
        v_g = z[SEQ - length:SEQ, COL_V + g * A_WIDTH:COL_V + (g + 1) * A_WIDTH].reshape(length, A_HEADS, HEAD_DIM)
        kv_p.append(jnp.stack([kn_g, v_g], axis=1)[None, None])
        ks_g = kn_s[:, g].reshape(DEC_BATCH, DEC_SEQ, A_HEADS, HEAD_DIM)
        vs_g = zs[:, COL_V + g * A_WIDTH:COL_V + (g + 1) * A_WIDTH].reshape(DEC_BATCH, DEC_SEQ, A_HEADS, HEAD_DIM)
        kv_s.append(jnp.stack([ks_g, vs_g], axis=2)[None])
    return (y_prompt, y_sample, kv_p[0], kv_p[1], kv_p[2], st_p[None, None],
            kv_s[0], kv_s[1], kv_s[2], st_s[None])
```

```python
import functools

import numpy as np
import jax
import jax.numpy as jnp
from jax import lax
from jax.experimental import pallas as pl
from jax.experimental.pallas import tpu as pltpu

F32 = jnp.float32
BF16 = jnp.bfloat16

D_MODEL = 4096
SEQ = 8192
DEC_BATCH = 32
DEC_SEQ = 8
N_SAMPLE = DEC_BATCH * DEC_SEQ
M_ALL = SEQ + N_SAMPLE
DIL_GROUPS = ((128, 1), (512, 4), (2048, 16))
N_GROUPS = 3
A_HEADS = 8
HEAD_DIM = 128
A_WIDTH = A_HEADS * HEAD_DIM
BAND = 128
HG_HEADS = 16
HG_KDIM = 128
HG_VDIM = 128
HG_WIDTH = HG_HEADS * HG_VDIM
HG_CHUNK = 32
FFN_HIDDEN = 4 * D_MODEL
PLE_DIM = 256
NORM_EPS = 1e-6
ATTN_SCALE = HEAD_DIM ** -0.5

COL_Q = 0
COL_K = COL_Q + N_GROUPS * A_WIDTH
COL_V = COL_K + N_GROUPS * A_WIDTH
COL_HQ = COL_V + N_GROUPS * A_WIDTH
COL_HF = COL_HQ + HG_HEADS * HG_KDIM
COL_HI = COL_HF + HG_HEADS * HG_KDIM
COL_HOG = COL_HI + HG_WIDTH
COL_GA = COL_HOG + HG_WIDTH
COL_GB = COL_GA + D_MODEL
IN_WIDTH = COL_GB + D_MODEL

VMEM_LIMIT = 56 * 1024 * 1024


def _dot(a, b):
    return jnp.dot(a, b, preferred_element_type=F32)


def _dot_nt(a, b):
    return lax.dot_general(a, b, (((1,), (1,)), ((), ())), preferred_element_type=F32)


def _dot_tn(a, b):
    return lax.dot_general(a, b, (((0,), (0,)), ((), ())), preferred_element_type=F32)


def _rms(x, g):
    return x * lax.rsqrt(jnp.mean(x * x, axis=-1, keepdims=True) + NORM_EPS) * g


def _sigmoid(x):
    return 1.0 / (1.0 + jnp.exp(-x))


def _store(outs, *vals):
    for o_ref, v in zip(outs, vals):
        o_ref[...] = v.astype(o_ref.dtype)


_CAST_ROWS = 128


def _mm_rows_body(*refs, needs_cast, n_extra, n_out, epilogue):
    n_pairs = len(needs_cast)
    a_refs = refs[0:2 * n_pairs:2]
    w_refs = refs[1:2 * n_pairs:2]
    pos = 2 * n_pairs
    extra = refs[pos:pos + n_extra]
    outs = refs[pos + n_extra:pos + n_extra + n_out]
    scratch = list(refs[pos + n_extra + n_out:])
    i, j = pl.program_id(0), pl.program_id(1)
    accs = []
    for a_ref, w_ref, cast in zip(a_refs, w_refs, needs_cast):
        if cast:
            ab_ref = scratch.pop(0)

            @pl.when(j == 0)
            def _(a_ref=a_ref, ab_ref=ab_ref):
                def rows(c, carry):
                    r0 = pl.multiple_of(c * _CAST_ROWS, _CAST_ROWS)
                    ab_ref[pl.ds(r0, _CAST_ROWS), :] = a_ref[pl.ds(r0, _CAST_ROWS), :].astype(BF16)
                    return carry
                lax.fori_loop(0, a_ref.shape[0] // _CAST_ROWS, rows, 0)

            a_val = ab_ref[...]
        else:
            a_val = a_ref[...]
        accs.append(_dot(a_val, w_ref[...].astype(BF16)))
    epilogue(accs, i, j, a_refs, extra, outs)


def _matmul_rows(pairs, *, bm, bn, extras=(), extra_specs=(), out_shapes, out_specs=None, epilogue, name):
    m = pairs[0][0].shape[0]
    n = pairs[0][1].shape[1]
    assert m % bm == 0 and n % bn == 0 and bm % _CAST_ROWS == 0
    in_specs, operands, scratch, needs_cast = [], [], [], []
    for a, w in pairs:
        kdim = w.shape[0]
        assert a.shape == (m, kdim) and w.shape[1] == n
        in_specs += [pl.BlockSpec((bm, kdim), lambda i, j: (i, 0), pipeline_mode=pl.Buffered(1)),
                     pl.BlockSpec((kdim, bn), lambda i, j: (0, j))]
        operands += [a, w]
        needs_cast.append(a.dtype != BF16)
        if needs_cast[-1]:
            scratch.append(pltpu.VMEM((bm, kdim), BF16))
    if out_specs is None:
        out_specs = tuple(pl.BlockSpec((bm, bn), lambda i, j: (i, j)) for _ in out_shapes)
    body = functools.partial(_mm_rows_body, needs_cast=tuple(needs_cast), n_extra=len(extras),
                             n_out=len(out_shapes), epilogue=epilogue)
    return pl.pallas_call(
        body,
        grid=(m // bm, n // bn),
        in_specs=in_specs + list(extra_specs),
        out_specs=tuple(out_specs),
        out_shape=tuple(out_shapes),
        scratch_shapes=scratch,
        compiler_params=pltpu.CompilerParams(dimension_semantics=("arbitrary", "arbitrary"),
                                             vmem_limit_bytes=VMEM_LIMIT),
        name=name,
    )(*operands, *extras)


def _mm_ksplit_body(a_ref, w_ref, r_ref, o_ref):
    @pl.when(pl.program_id(2) == 0)
    def _():
        o_ref[...] = r_ref[...]

    o_ref[...] += _dot(a_ref[...], w_ref[...].astype(BF16))


def _matmul_ksplit(a, w, resid, *, bm, bn, bk, name):
    m, kdim = a.shape
    n = w.shape[1]
    assert m % bm == 0 and n % bn == 0 and kdim % bk == 0 and a.dtype == BF16
    return pl.pallas_call(
        _mm_ksplit_body,
        grid=(n // bn, m // bm, kdim // bk),
        in_specs=[pl.BlockSpec((bm, bk), lambda j, i, k: (i, k)),
                  pl.BlockSpec((bk, bn), lambda j, i, k: (k, j)),
                  pl.BlockSpec((bm, bn), lambda j, i, k: (i, j), pipeline_mode=pl.Buffered(1))],
        out_specs=pl.BlockSpec((bm, bn), lambda j, i, k: (i, j), pipeline_mode=pl.Buffered(1)),
        out_shape=jax.ShapeDtypeStruct((m, n), F32),
        compiler_params=pltpu.CompilerParams(dimension_semantics=("arbitrary", "arbitrary", "arbitrary"),
                                             vmem_limit_bytes=VMEM_LIMIT),
        name=name,
    )(a, w, resid)


def _rmsnorm_body(*refs, n_main):
    g_ref, o_ref = refs[-2:]
    if n_main is None:
        o_ref[...] = _rms(refs[0][...], g_ref[...]).astype(o_ref.dtype)
        return
    i = pl.program_id(0)

    @pl.when(i < n_main)
    def _():
        o_ref[...] = _rms(refs[0][...], g_ref[...]).astype(o_ref.dtype)

    @pl.when(i >= n_main)
    def _():
        o_ref[...] = _rms(refs[1][...], g_ref[...]).astype(o_ref.dtype)


def _rmsnorm_rows(x, g, *, x_tail=None, bm=256, name="rmsnorm"):
    m, d = x.shape
    assert m % bm == 0
    n_main = m // bm
    in_specs = [pl.BlockSpec((bm, d), lambda i: (jnp.minimum(i, n_main - 1), 0))]
    operands = [x]
    if x_tail is not None:
        assert x_tail.shape[0] % bm == 0
        in_specs.append(pl.BlockSpec((bm, d), lambda i: (jnp.maximum(i - n_main, 0), 0)))
        operands.append(x_tail)
        m += x_tail.shape[0]
    in_specs.append(pl.BlockSpec((1, d), lambda i: (0, 0)))
    return pl.pallas_call(
        functools.partial(_rmsnorm_body, n_main=None if x_tail is None else n_main),
        grid=(m // bm,),
        in_specs=in_specs,
        out_specs=pl.BlockSpec((bm, d), lambda i: (i, 0)),
        out_shape=jax.ShapeDtypeStruct((m, d), BF16),
        compiler_params=pltpu.CompilerParams(dimension_semantics=("arbitrary",)),
        name=name,
    )(*operands, g.reshape(1, d))


def _ld_rows(ref, start, dil, n=BAND):
    if dil == 1:
        return ref[pl.ds(start, n), :]
    return ref[pl.ds(start, n, stride=dil), :]


def _st_rows(ref, g, start, dil, val):
    if dil == 1:
        ref[g, pl.ds(start, BAND), :] = val
    else:
        ref[g, pl.ds(start, BAND, stride=dil), :] = val


def _attn_prompt_body(slope_ref, gq_ref, gk_ref,
                      q0, k0, v0, kp0, vp0, q1, k1, v1, kp1, vp1, q2, k2, v2, kp2, vp2,
                      o_ref, kn0, kn1, kn2,
                      qn_s, kn_s, vv_s, og_s, lse_s, *, sb, dils):
    first_block = pl.program_id(1) == 0
    groups = ((q0, k0, v0, kp0, vp0, kn0), (q1, k1, v1, kp1, vp1, kn1), (q2, k2, v2, kp2, vp2, kn2))
    ii = lax.broadcasted_iota(jnp.int32, (BAND, BAND), 0)
    jj = lax.broadcasted_iota(jnp.int32, (BAND, BAND), 1)
    dist_cur = (ii - jj).astype(F32)
    dist_prev = (BAND + ii - jj).astype(F32)
    ok_cur = jj <= ii
    ok_prev = jj >= ii
    neg_inf = jnp.float32(-jnp.inf)
    base = kn_s.shape[0] - sb

    for g, dil in enumerate(dils):
        q_ref, k_ref, v_ref, kp_ref, vp_ref, kn_out = groups[g]
        pr = BAND * dil
        gq = gq_ref[g:g + 1, :]
        gk = gk_ref[g:g + 1, :]
        slope = slope_ref[0, :, g * HEAD_DIM:(g + 1) * HEAD_DIM]
        qn_s[...] = _rms(q_ref[...], gq)
        kn = _rms(k_ref[...], gk)
        kn_out[...] = kn
        kn_s[pl.ds(base, sb), :] = kn
        kn_s[pl.ds(base - pr, pr), :] = _rms(kp_ref[...], gk)
        vv_s[pl.ds(base, sb), :] = v_ref[...]
        vv_s[pl.ds(base - pr, pr), :] = vp_ref[...]
        bias_cur = jnp.where(ok_cur, -slope * (dist_cur * dil), neg_inf)
        bias_prev = jnp.where(ok_prev, -slope * (dist_prev * dil), neg_inf)
        bias = jnp.concatenate([bias_prev, bias_cur], axis=1)
        bias_first = jnp.concatenate([jnp.where(first_block, neg_inf, bias_prev), bias_cur], axis=1)
        ones = jnp.ones((2 * BAND, HEAD_DIM), BF16)
        for r in range(dil):
            for s in range(sb // pr):
                c0 = s * pr + r
                qs = _ld_rows(qn_s, c0, dil).astype(BF16)
                kk = _ld_rows(kn_s, base + c0 - pr, dil, 2 * BAND).astype(BF16)
                vv = _ld_rows(vv_s, base + c0 - pr, dil, 2 * BAND).astype(BF16)
                lg = _dot_nt(qs, kk) * ATTN_SCALE + (bias_first if s == 0 else bias)
                mx = jnp.max(lg, axis=-1, keepdims=True)
                p = jnp.exp(lg - mx).astype(BF16)
                oe = _dot(p, jnp.concatenate([vv, ones], axis=1))
                den = oe[:, HEAD_DIM:]
                _st_rows(og_s, g, c0, dil, oe[:, :HEAD_DIM] / den)
                _st_rows(lse_s, g, c0, dil, mx + jnp.log(den))

    l0, l1, l2 = lse_s[0], lse_s[1], lse_s[2]
    mx = jnp.maximum(jnp.maximum(l0, l1), l2)
    w0, w1, w2 = jnp.exp(l0 - mx), jnp.exp(l1 - mx), jnp.exp(l2 - mx)
    o = (w0 * og_s[0] + w1 * og_s[1] + w2 * og_s[2]) / (w0 + w1 + w2)
    o_ref[...] = o.astype(o_ref.dtype)


def _alibi_slopes():
    n = N_GROUPS * A_HEADS
    e = jnp.arange(1, n + 1, dtype=F32)
    return jnp.exp2(-8.0 * e / n).reshape(N_GROUPS, A_HEADS)


def _attn_prompt(z, g_q, g_k, *, seq, sb, dils, out_rows=None, n_heads=A_HEADS, col_q=COL_Q, col_k=COL_K,
                 col_v=COL_V):
    out_rows = seq if out_rows is None else out_rows
    nb = seq // sb
    slopes = _alibi_slopes()
    slope_arr = jnp.broadcast_to(slopes.T[:, None, :, None], (n_heads, 1, N_GROUPS, HEAD_DIM))
    slope_arr = slope_arr.reshape(n_heads, 1, N_GROUPS * HEAD_DIM)
    a_width = n_heads * HEAD_DIM
    in_specs = [pl.BlockSpec((1, 1, N_GROUPS * HEAD_DIM), lambda h, i: (h, 0, 0)),
                pl.BlockSpec((N_GROUPS, HEAD_DIM), lambda h, i: (0, 0)),
                pl.BlockSpec((N_GROUPS, HEAD_DIM), lambda h, i: (0, 0))]
    operands = [slope_arr, g_q, g_k]
    max_pr = BAND * max(dils)
    for g, dil in enumerate(dils):
        pr = BAND * dil
        ratio = sb // pr
        cq = (col_q + g * a_width) // HEAD_DIM
        ck = (col_k + g * a_width) // HEAD_DIM
        cv = (col_v + g * a_width) // HEAD_DIM
        cur = lambda c: pl.BlockSpec((sb, HEAD_DIM), lambda h, i, c=c: (i, c + h))
        prev = lambda c: pl.BlockSpec((pr, HEAD_DIM),
                                      lambda h, i, c=c, ratio=ratio: (jnp.maximum(i * ratio - 1, 0), c + h))
        in_specs += [cur(cq), cur(ck), cur(cv), prev(ck), prev(cv)]
        operands += [z, z, z, z, z]
    out_block = pl.BlockSpec((sb, HEAD_DIM), lambda h, i: (i, h))
    out_shape = (jax.ShapeDtypeStruct((out_rows, a_width), BF16),) + tuple(
        jax.ShapeDtypeStruct((seq, a_width), F32) for _ in dils)
    body = functools.partial(_attn_prompt_body, sb=sb, dils=tuple(dils))
    return pl.pallas_call(
        body,
        grid=(n_heads, nb),
        in_specs=in_specs,
        out_specs=(out_block,) * 4,
        out_shape=out_shape,
        scratch_shapes=[pltpu.VMEM((sb, HEAD_DIM), F32),
                        pltpu.VMEM((sb + max_pr, HEAD_DIM), F32),
                        pltpu.VMEM((sb + max_pr, HEAD_DIM), F32),
                        pltpu.VMEM((N_GROUPS, sb, HEAD_DIM), F32),
                        pltpu.VMEM((N_GROUPS, sb, HEAD_DIM), F32)],
        compiler_params=pltpu.CompilerParams(dimension_semantics=("arbitrary", "arbitrary"),
                                             vmem_limit_bytes=VMEM_LIMIT),
        name="attn_prompt",
    )(*operands)


def _sample_problems(dil, t_new):
    n_prob = min(dil, t_new)
    return n_prob, t_new // n_prob


def _sample_bias_tables(groups, t_new, n_heads):
    n_soft = len(groups) * n_heads
    tabs_c, tabs_n = [], []
    for g, (window, dil) in enumerate(groups):
        n_prob, tok_per = _sample_problems(dil, t_new)
        slopes = 2.0 ** (-8.0 * (g * n_heads + np.arange(n_heads) + 1) / n_soft)
        rows = np.arange(tok_per * n_heads)
        i, hp = rows // n_heads, rows % n_heads
        cols = np.arange(BAND * n_heads)
        m, h = cols // n_heads, cols % n_heads
        dist = window + i[:, None] * n_prob - m[None, :] * dil
        ok = (h[None, :] == hp[:, None]) & (dist <= window) & (dist > 0) & (dist % dil == 0)
        tabs_c.append(np.where(ok, -slopes[hp][:, None] * dist, -np.inf).astype(np.float32))
        cols = np.arange(t_new * n_heads)
        s, h = cols // n_heads, cols % n_heads
        per_p = []
        for p in range(n_prob):
            dist = (p + i * n_prob)[:, None] - s[None, :]
            ok = (h[None, :] == hp[:, None]) & (dist >= 0) & (dist % dil == 0)
            per_p.append(np.where(ok, -slopes[hp][:, None] * dist, -np.inf).astype(np.float32))
        tabs_n.append(np.stack(per_p))
    return tabs_c, tabs_n


def _attn_sample_body(gq_ref, gk_ref, q_ref, k_ref, v_ref, c0_ref, c1_ref, c2_ref,
                      bc0, bc1, bc2, bn0, bn1, bn2, o_ref, kn_ref, og_s, lse_s, *, t_new, groups, n_heads):
    caches = (c0_ref, c1_ref, c2_ref)
    bias_c = (bc0, bc1, bc2)
    bias_n = (bn0, bn1, bn2)
    kv_rows = 2 * n_heads
    for g, (window, dil) in enumerate(groups):
        c_ref = caches[g]
        n_prob, tok_per = _sample_problems(dil, t_new)
        qn = _rms(q_ref[0, g], gq_ref[g:g + 1, :])
        kn = _rms(k_ref[0, g], gk_ref[g:g + 1, :])
        kn_ref[0, g] = kn
        knb = kn.astype(BF16)
        vnb = v_ref[0, g].astype(BF16)
        for p in range(n_prob):
            toks = [p + i * n_prob for i in range(tok_per)]
            parts = [qn[t * n_heads:(t + 1) * n_heads] for t in toks]
            qp = (parts[0] if tok_per == 1 else jnp.concatenate(parts, axis=0)).astype(BF16)
            kc = c_ref[:, p * kv_rows:p * kv_rows + n_heads, :].reshape(BAND * n_heads, HEAD_DIM).astype(BF16)
            vc = c_ref[:, p * kv_rows + n_heads:(p + 1) * kv_rows, :].reshape(BAND * n_heads, HEAD_DIM).astype(BF16)
            lc = _dot_nt(qp, kc) * ATTN_SCALE + bias_c[g][...]
            ln = _dot_nt(qp, knb) * ATTN_SCALE + bias_n[g][p]
            mx = jnp.maximum(jnp.max(lc, axis=-1, keepdims=True), jnp.max(ln, axis=-1, keepdims=True))
            pc = jnp.exp(lc - mx)
            pn = jnp.exp(ln - mx)
            ssum = jnp.sum(pc, axis=-1, keepdims=True) + jnp.sum(pn, axis=-1, keepdims=True)
            o = (_dot(pc.astype(BF16), vc) + _dot(pn.astype(BF16), vnb)) / ssum
            lse = jnp.broadcast_to(mx + jnp.log(ssum), o.shape)
            for i, t in enumerate(toks):
                og_s[g, t * n_heads:(t + 1) * n_heads, :] = o[i * n_heads:(i + 1) * n_heads]
                lse_s[g, t * n_heads:(t + 1) * n_heads, :] = lse[i * n_heads:(i + 1) * n_heads]
    l0, l1, l2 = lse_s[0], lse_s[1], lse_s[2]
    mx = jnp.maximum(jnp.maximum(l0, l1), l2)
    w0, w1, w2 = jnp.exp(l0 - mx), jnp.exp(l1 - mx), jnp.exp(l2 - mx)
    o_ref[0] = (w0 * og_s[0] + w1 * og_s[1] + w2 * og_s[2]) / (w0 + w1 + w2)


def _attn_sample(qs, ks, vs, caches, g_q, g_k, *, n_batch, t_new, groups, n_heads=A_HEADS):
    n_g = len(groups)
    rows = t_new * n_heads
    tabs_c, tabs_n = _sample_bias_tables(groups, t_new, n_heads)
    new_spec = lambda: pl.BlockSpec((1, n_g, rows, HEAD_DIM), lambda b: (b, 0, 0, 0))
    in_specs = [pl.BlockSpec((n_g, HEAD_DIM), lambda b: (0, 0)),
                pl.BlockSpec((n_g, HEAD_DIM), lambda b: (0, 0)),
                new_spec(), new_spec(), new_spec()]
    operands = [g_q, g_k, qs, ks, vs]
    for c, (window, dil) in zip(caches, groups):
        assert c.shape[1] == window and window == BAND * dil
        n_prob, _ = _sample_problems(dil, t_new)
        c3 = c.reshape(n_batch * BAND, dil * 2 * n_heads, HEAD_DIM)
        in_specs.append(pl.BlockSpec((BAND, n_prob * 2 * n_heads, HEAD_DIM), lambda b: (b, 0, 0)))
        operands.append(c3)
    for tab in tabs_c:
        in_specs.append(pl.BlockSpec(tab.shape, lambda b: (0, 0)))
        operands.append(jnp.asarray(tab))
    for tab in tabs_n:
        in_specs.append(pl.BlockSpec(tab.shape, lambda b: (0, 0, 0)))
        operands.append(jnp.asarray(tab))
    body = functools.partial(_attn_sample_body, t_new=t_new, groups=tuple(groups), n_heads=n_heads)
    return pl.pallas_call(
        body,
        grid=(n_batch,),
        in_specs=in_specs,
        out_specs=(pl.BlockSpec((1, rows, HEAD_DIM), lambda b: (b, 0, 0)), new_spec()),
        out_shape=(jax.ShapeDtypeStruct((n_batch, rows, HEAD_DIM), F32),
                   jax.ShapeDtypeStruct((n_batch, n_g, rows, HEAD_DIM), F32)),
        scratch_shapes=[pltpu.VMEM((n_g, rows, HEAD_DIM), F32), pltpu.VMEM((n_g, rows, HEAD_DIM), F32)],
        compiler_params=pltpu.CompilerParams(dimension_semantics=("arbitrary",),
                                             vmem_limit_bytes=VMEM_LIMIT),
        name="attn_sample",
    )(*operands)


def _split3(x):
    hi = x.astype(BF16)
    r1 = x - hi.astype(F32)
    mid = r1.astype(BF16)
    lo = (r1 - mid.astype(F32)).astype(BF16)
    return hi, mid, lo


def _hgrn_heads(hq, hf, hi, hog, lb, g_out, sts, *, tb, n_valid=None):
    c = HG_CHUNK
    nh = len(sts)
    sts = list(sts)
    q = hq * _sigmoid(hq) * (HG_KDIM ** -0.5)
    gate = lb + (1.0 - lb) * _sigmoid(hf)
    log_g = jnp.log(gate)
    k = (1.0 - lb) * _sigmoid(-hf)
    if n_valid is not None:
        rows = lax.broadcasted_iota(jnp.int32, hq.shape, 0)
        log_g = jnp.where(rows < n_valid, log_g, 0.0)
        k = jnp.where(rows < n_valid, k, 0.0)
        q = jnp.where(rows < n_valid, q, 0.0)
    row = lax.broadcasted_iota(jnp.int32, (tb, tb), 0)
    col = lax.broadcasted_iota(jnp.int32, (tb, tb), 1)
    same = (row // c) == (col // c)
    tri_ok = same & (col <= row)
    tri = jnp.where(tri_ok, 1.0, 0.0).astype(BF16)
    blk = jnp.where(same, 1.0, 0.0).astype(BF16)
    p_hi, p_mid, p_lo = _split3(log_g)
    b = _dot(tri, p_hi) + _dot(tri, p_mid) + _dot(tri, p_lo)
    b_last = _dot(blk, p_hi) + _dot(blk, p_mid) + _dot(blk, p_lo)
    q_dec = (q * jnp.exp(b)).astype(BF16)
    k_dec = (k * jnp.exp(-b)).astype(BF16)
    k_end = (k * jnp.exp(b_last - b)).astype(BF16)
    decay = jnp.exp(b_last)
    vb = hi.astype(BF16)
    head = lambda x, h: x[:, h * HG_KDIM:(h + 1) * HG_KDIM]
    o_intra = []
    for h in range(nh):
        a = jnp.where(tri_ok, _dot_nt(head(q_dec, h), head(k_dec, h)), 0.0)
        o_intra.append(_dot(a.astype(BF16), head(vb, h)))
    parts = [[] for _ in range(nh)]
    for ci in range(tb // c):
        sl = slice(ci * c, (ci + 1) * c)
        for h in range(nh):
            parts[h].append(_dot_nt(head(q_dec, h)[sl], sts[h].astype(BF16)))
            sts[h] = sts[h] * head(decay, h)[ci * c:ci * c + 1, :] + _dot_tn(head(vb, h)[sl], head(k_end, h)[sl])
    outs = []
    for h in range(nh):
        o = o_intra[h] + (jnp.concatenate(parts[h], axis=0) if len(parts[h]) > 1 else parts[h][0])
        outs.append(_rms(o, g_out) * _sigmoid(head(hog, h)))
    return outs, sts


def _hgrn_prompt_body(hq_ref, hf_ref, hi_ref, hog_ref, lb_ref, go_ref, o_ref, s_ref, st_s, *, tb, nt, hb):
    t = pl.program_id(1)

    @pl.when(t == 0)
    def _():
        st_s[...] = jnp.zeros_like(st_s)

    outs, sts = _hgrn_heads(hq_ref[...], hf_ref[...], hi_ref[...], hog_ref[...], lb_ref[0], go_ref[...],
                            [st_s[hh] for hh in range(hb)], tb=tb)
    for hh in range(hb):
        o_ref[:, hh * HG_VDIM:(hh + 1) * HG_VDIM] = outs[hh].astype(o_ref.dtype)
        st_s[hh] = sts[hh]

    @pl.when(t == nt - 1)
    def _():
        for hh in range(hb):
            s_ref[hh] = sts[hh].T


def _hgrn_prompt(z, lb, g_out, *, seq, tb, hb, out_rows=None, n_heads=HG_HEADS, col_hq=COL_HQ, col_hf=COL_HF,
                 col_hi=COL_HI, col_hog=COL_HOG):
    out_rows = seq if out_rows is None else out_rows
    nt = seq // tb
    bw = hb * HG_KDIM
    blk = lambda c0: pl.BlockSpec((tb, bw), lambda h, t, c=c0 // bw: (t, c + h))
    body = functools.partial(_hgrn_prompt_body, tb=tb, nt=nt, hb=hb)
    return pl.pallas_call(
        body,
        grid=(n_heads // hb, nt),
        in_specs=[blk(col_hq), blk(col_hf), blk(col_hi), blk(col_hog),
                  pl.BlockSpec((1, 1, bw), lambda h, t: (h, 0, 0)),
                  pl.BlockSpec((1, HG_VDIM), lambda h, t: (0, 0))],
        out_specs=(pl.BlockSpec((tb, bw), lambda h, t: (t, h)),
                   pl.BlockSpec((hb, HG_KDIM, HG_VDIM), lambda h, t: (h, 0, 0))),
        out_shape=(jax.ShapeDtypeStruct((out_rows, n_heads * HG_VDIM), BF16),
                   jax.ShapeDtypeStruct((n_heads, HG_KDIM, HG_VDIM), F32)),
        scratch_shapes=[pltpu.VMEM((hb, HG_VDIM, HG_KDIM), F32)],
        compiler_params=pltpu.CompilerParams(dimension_semantics=("arbitrary", "arbitrary"),
                                             vmem_limit_bytes=VMEM_LIMIT),
        name="hgrn_prompt",
    )(z, z, z, z, lb.reshape(n_heads // hb, 1, bw), g_out.reshape(1, HG_VDIM))


def _hgrn_sample_body(hq_ref, hf_ref, hi_ref, hog_ref, lb_ref, go_ref, s0_ref, o_ref, s_ref, *, t_new, n_heads):
    pad = jnp.zeros((HG_CHUNK - t_new, n_heads * HG_KDIM), F32)
    ext = lambda ref: jnp.concatenate([ref[...], pad], axis=0)
    outs, sts = _hgrn_heads(ext(hq_ref), ext(hf_ref), ext(hi_ref), ext(hog_ref), lb_ref[...], go_ref[...],
                            [s0_ref[0, h].T for h in range(n_heads)], tb=HG_CHUNK, n_valid=t_new)
    for h in range(n_heads):
        o_ref[:, h * HG_VDIM:(h + 1) * HG_VDIM] = outs[h][:t_new]
        s_ref[0, h] = sts[h].T


def _hgrn_sample(hq, hf, hi, hog, lb, g_out, s0, *, n_batch, t_new, n_heads=HG_HEADS):
    width = n_heads * HG_KDIM
    row = lambda: pl.BlockSpec((t_new, width), lambda b: (b, 0))
    st = lambda: pl.BlockSpec((1, n_heads, HG_KDIM, HG_VDIM), lambda b: (b, 0, 0, 0))
    body = functools.partial(_hgrn_sample_body, t_new=t_new, n_heads=n_heads)
    return pl.pallas_call(
        body,
        grid=(n_batch,),
        in_specs=[row(), row(), row(), row(),
                  pl.BlockSpec((1, width), lambda b: (0, 0)),
                  pl.BlockSpec((1, HG_VDIM), lambda b: (0, 0)),
                  st()],
        out_specs=(row(), st()),
        out_shape=(jax.ShapeDtypeStruct((n_batch * t_new, width), F32),
                   jax.ShapeDtypeStruct(s0.shape, F32)),
        compiler_params=pltpu.CompilerParams(dimension_semantics=("arbitrary",),
                                             vmem_limit_bytes=VMEM_LIMIT),
        name="hgrn_sample",
    )(hq, hf, hi, hog, lb.reshape(1, width), g_out.reshape(1, HG_VDIM), s0)


def _tile_spec(bm, bn, col0=0):
    cb = col0 // bn
    return pl.BlockSpec((bm, bn), lambda i, j: (i, cb + j))


def kernel(x_prompt, x_sample, cache_kv_w128, cache_kv_w512, cache_kv_w2048, state_hgrn, p_prompt, p_sample,
           g_mix, w_in, g_q, g_k, hg_lb_raw, g_hg_out, w_up_attn, w_up_hgrn, w_out, g_ffn, w_ff_up, w_ff_down,
           w_ple, w_ple_gate):
    bm, bn = 2816, 256
    bm_ple = 1408
    bm_res = 1408
    xp, xs = x_prompt.reshape(SEQ, D_MODEL), x_sample.reshape(N_SAMPLE, D_MODEL)
    pp, ps = p_prompt.reshape(SEQ, PLE_DIM), p_sample.reshape(N_SAMPLE, PLE_DIM)
    lb = jnp.cumsum(jax.nn.softmax(hg_lb_raw.astype(F32), axis=0), axis=0)[0]
    rows = lambda width, dt: jax.ShapeDtypeStruct((M_ALL, width), dt)

    def plain(accs, i, j, a_refs, extra, outs):
        _store(outs, accs[0])

    n_mix = _rmsnorm_rows(xp, g_mix[0], x_tail=xs, name="norm_mix")
    z, = _matmul_rows([(n_mix, w_in[0])], bm=bm, bn=bn, out_shapes=(rows(IN_WIDTH, F32),), epilogue=plain,
                      name="in_proj")

    o_attn, kn0, kn1, kn2 = _attn_prompt(z, g_q[0], g_k[0], seq=SEQ, sb=2048, out_rows=M_ALL,
                                         dils=tuple(d for _, d in DIL_GROUPS))
    zs = z[SEQ:]
    def by_group(col0):
        a = zs[:, col0:col0 + N_GROUPS * A_WIDTH].reshape(DEC_BATCH, DEC_SEQ, N_GROUPS, A_HEADS, HEAD_DIM)
        return a.transpose(0, 2, 1, 3, 4).reshape(DEC_BATCH, N_GROUPS, DEC_SEQ * A_HEADS, HEAD_DIM)

    caches = [c[0] for c in (cache_kv_w128, cache_kv_w512, cache_kv_w2048)]
    o_attn_s, kn_s = _attn_sample(by_group(COL_Q), by_group(COL_K), by_group(COL_V), caches,
                                  g_q[0], g_k[0], n_batch=DEC_BATCH, t_new=DEC_SEQ, groups=DIL_GROUPS)
    o_attn = lax.dynamic_update_slice(o_attn, o_attn_s.reshape(N_SAMPLE, A_WIDTH).astype(BF16), (SEQ, 0))

    o_hg, st_p = _hgrn_prompt(z, lb, g_hg_out[0], seq=SEQ, tb=256, hb=4, out_rows=M_ALL)
    o_hg_s, st_s = _hgrn_sample(zs[:, COL_HQ:COL_HF], zs[:, COL_HF:COL_HI], zs[:, COL_HI:COL_HOG],
                                zs[:, COL_HOG:COL_GA], lb, g_hg_out[0], state_hgrn[0],
                                n_batch=DEC_BATCH, t_new=DEC_SEQ)
    o_hg = lax.dynamic_update_slice(o_hg, o_hg_s.astype(BF16), (SEQ, 0))

    def merge_epilogue(accs, i, j, a_refs, extra, outs):
        _store(outs, _sigmoid(extra[0][...]) * accs[0] + _sigmoid(extra[1][...]) * accs[1])

    def out_proj_epilogue(accs, i, j, a_refs, extra, outs):
        xp_ref, xs_ref = extra
        outs[0][...] = xp_ref[...] + accs[0]

        @pl.when(i == pl.num_programs(0) - 1)
        def _():
            outs[0][bm_res - N_SAMPLE:, :] = xs_ref[...] + accs[0][bm_res - N_SAMPLE:, :]

    merged, = _matmul_rows([(o_attn, w_up_attn[0]), (o_hg, w_up_hgrn[0])], bm=bm, bn=bn, extras=(z, z),
                           extra_specs=(_tile_spec(bm, bn, COL_GA), _tile_spec(bm, bn, COL_GB)),
                           out_shapes=(rows(D_MODEL, BF16),), epilogue=merge_epilogue, name="up_merge")
    x1, = _matmul_rows([(merged, w_out[0])], bm=bm_res, bn=bn, extras=(xp, xs),
                       extra_specs=(_tile_spec(bm_res, bn), pl.BlockSpec((N_SAMPLE, bn), lambda i, j: (0, j))),
                       out_shapes=(rows(D_MODEL, F32),), epilogue=out_proj_epilogue, name="out_proj")

    def ffn_up_epilogue(accs, i, j, a_refs, extra, outs):
        _store(outs, jnp.square(jnp.maximum(accs[0], 0.0)))

    h_ffn = _rmsnorm_rows(x1, g_ffn[0], name="norm_ffn")
    hid, = _matmul_rows([(h_ffn, w_ff_up[0])], bm=bm, bn=bn, out_shapes=(rows(FFN_HIDDEN, BF16),),
                        epilogue=ffn_up_epilogue, name="ffn_up")
    x2 = _matmul_ksplit(hid, w_ff_down[0], x1, bm=1408, bn=2048, bk=1024, name="ffn_down")

    n_ple_blocks = M_ALL // bm_ple

    def ple_epilogue(accs, i, j, a_refs, extra, outs):
        pp_ref, ps_ref, wp_ref = extra
        yp_ref, ys_ref = outs
        x2_tile = a_refs[0][:, pl.ds(pl.multiple_of(j * bn, bn), bn)]
        gate = _sigmoid(accs[0])
        wp = wp_ref[...].astype(BF16)
        yp_ref[...] = x2_tile + gate * _dot(pp_ref[...].astype(BF16), wp)

        @pl.when(i == n_ple_blocks - 1)
        def _():
            s0 = bm_ple - N_SAMPLE
            ys_ref[...] = x2_tile[s0:, :] + gate[s0:, :] * _dot(ps_ref[...].astype(BF16), wp)

    y_p, y_s = _matmul_rows(
        [(x2, w_ple_gate[0])], bm=bm_ple, bn=bn, extras=(pp, ps, w_ple[0]),
        extra_specs=(pl.BlockSpec((bm_ple, PLE_DIM), lambda i, j: (i, 0)),
                     pl.BlockSpec((N_SAMPLE, PLE_DIM), lambda i, j: (0, 0)),
                     pl.BlockSpec((PLE_DIM, bn), lambda i, j: (0, j))),
        out_shapes=(jax.ShapeDtypeStruct((SEQ, D_MODEL), F32), jax.ShapeDtypeStruct((N_SAMPLE, D_MODEL), F32)),
        out_specs=(pl.BlockSpec((bm_ple, bn), lambda i, j: (i, j)),
                   pl.BlockSpec((N_SAMPLE, bn), lambda i, j: (0, jnp.where(i == n_ple_blocks - 1, j, 0)))),
        epilogue=ple_epilogue, name="ple")

    y_prompt = y_p.reshape(1, SEQ, D_MODEL)
    y_sample = y_s.reshape(DEC_BATCH, DEC_SEQ, D_MODEL)
    kv_p, kv_s = [], []
    for g, (window, _) in enumerate(DIL_GROUPS):
        length = min(window, SEQ)
        kn_g = (kn0, kn1, kn2)[g][SEQ - length:].reshape(length, A_HEADS, HEAD_DIM)
        v_g = z[SEQ - length:SEQ, COL_V + g * A_WIDTH:COL_V + (g + 1) * A_WIDTH].reshape(length, A_HEADS, HEAD_DIM)
        kv_p.append(jnp.stack([kn_g, v_g], axis=1)[None, None])
        ks_g = kn_s[:, g].reshape(DEC_BATCH, DEC_SEQ, A_HEADS, HEAD_DIM)
        vs_g = zs[:, COL_V + g * A_WIDTH:COL_V + (g + 1) * A_WIDTH].reshape(DEC_BATCH, DEC_SEQ, A_HEADS, HEAD_DIM)
        kv_s.append(jnp.stack([ks_g, vs_g], axis=2)[None])
    return (y_prompt, y_sample, kv_p[0], kv_p[1], kv_p[2], st_p[None, None],
            kv_s[0], kv_s[1], kv_s[2], st_s[None])
```

```python
import functools

import numpy as np
import jax
import jax.numpy as jnp
from jax import lax
from jax.experimental import pallas as pl
from jax.experimental.pallas import tpu as pltpu

F32 = jnp.float32
BF16 = jnp.bfloat16

D_MODEL = 4096
SEQ = 8192
DEC_BATCH = 32
DEC_SEQ = 8
N_SAMPLE = DEC_BATCH * DEC_SEQ
M_ALL = SEQ + N_SAMPLE
DIL_GROUPS = ((128, 1), (512, 4), (2048, 16))
N_GROUPS = 3
A_HEADS = 8
HEAD_DIM = 128
A_WIDTH = A_HEADS * HEAD_DIM
BAND = 128
HG_HEADS = 16
HG_KDIM = 128
HG_VDIM = 128
HG_WIDTH = HG_HEADS * HG_VDIM
HG_CHUNK = 32
FFN_HIDDEN = 4 * D_MODEL
PLE_DIM = 256
NORM_EPS = 1e-6
ATTN_SCALE = HEAD_DIM ** -0.5

COL_Q = 0
COL_K = COL_Q + N_GROUPS * A_WIDTH
COL_V = COL_K + N_GROUPS * A_WIDTH
COL_HQ = COL_V + N_GROUPS * A_WIDTH
COL_HF = COL_HQ + HG_HEADS * HG_KDIM
COL_HI = COL_HF + HG_HEADS * HG_KDIM
COL_HOG = COL_HI + HG_WIDTH
COL_GA = COL_HOG + HG_WIDTH
COL_GB = COL_GA + D_MODEL
IN_WIDTH = COL_GB + D_MODEL

VMEM_LIMIT = 56 * 1024 * 1024


def _dot(a, b):
    return jnp.dot(a, b, preferred_element_type=F32)


def _dot_nt(a, b):
    return lax.dot_general(a, b, (((1,), (1,)), ((), ())), preferred_element_type=F32)


def _dot_tn(a, b):
    return lax.dot_general(a, b, (((0,), (0,)), ((), ())), preferred_element_type=F32)


def _rms(x, g):
    return x * lax.rsqrt(jnp.mean(x * x, axis=-1, keepdims=True) + NORM_EPS) * g


def _sigmoid(x):
    return 1.0 / (1.0 + jnp.exp(-x))


def _store(outs, *vals):
    for o_ref, v in zip(outs, vals):
        o_ref[...] = v.astype(o_ref.dtype)


_CAST_ROWS = 128


def _mm_rows_body(*refs, needs_cast, n_extra, n_out, epilogue):
    n_pairs = len(needs_cast)
    a_refs = refs[0:2 * n_pairs:2]
    w_refs = refs[1:2 * n_pairs:2]
    pos = 2 * n_pairs
    extra = refs[pos:pos + n_extra]
    outs = refs[pos + n_extra:pos + n_extra + n_out]
    scratch = list(refs[pos + n_extra + n_out:])
    i, j = pl.program_id(0), pl.program_id(1)
    accs = []
    for a_ref, w_ref, cast in zip(a_refs, w_refs, needs_cast):
        if cast:
            ab_ref = scratch.pop(0)

            @pl.when(j == 0)
            def _(a_ref=a_ref, ab_ref=ab_ref):
                def rows(c, carry):
                    r0 = pl.multiple_of(c * _CAST_ROWS, _CAST_ROWS)
                    ab_ref[pl.ds(r0, _CAST_ROWS), :] = a_ref[pl.ds(r0, _CAST_ROWS), :].astype(BF16)
                    return carry
                lax.fori_loop(0, a_ref.shape[0] // _CAST_ROWS, rows, 0)

            a_val = ab_ref[...]
        else:
            a_val = a_ref[...]
        accs.append(_dot(a_val, w_ref[...].astype(BF16)))
    epilogue(accs, i, j, a_refs, extra, outs)


def _matmul_rows(pairs, *, bm, bn, extras=(), extra_specs=(), out_shapes, out_specs=None, epilogue, name):
    m = pairs[0][0].shape[0]
    n = pairs[0][1].shape[1]
    assert m % bm == 0 and n % bn == 0 and bm % _CAST_ROWS == 0
    in_specs, operands, scratch, needs_cast = [], [], [], []
    for a, w in pairs:
        kdim = w.shape[0]
        assert a.shape == (m, kdim) and w.shape[1] == n
        in_specs += [pl.BlockSpec((bm, kdim), lambda i, j: (i, 0), pipeline_mode=pl.Buffered(1)),
                     pl.BlockSpec((kdim, bn), lambda i, j: (0, j))]
        operands += [a, w]
        needs_cast.append(a.dtype != BF16)
        if needs_cast[-1]:
            scratch.append(pltpu.VMEM((bm, kdim), BF16))
    if out_specs is None:
        out_specs = tuple(pl.BlockSpec((bm, bn), lambda i, j: (i, j)) for _ in out_shapes)
    body = functools.partial(_mm_rows_body, needs_cast=tuple(needs_cast), n_extra=len(extras),
                             n_out=len(out_shapes), epilogue=epilogue)
    return pl.pallas_call(
        body,
        grid=(m // bm, n // bn),
        in_specs=in_specs + list(extra_specs),
        out_specs=tuple(out_specs),
        out_shape=tuple(out_shapes),
        scratch_shapes=scratch,
        compiler_params=pltpu.CompilerParams(dimension_semantics=("arbitrary", "arbitrary"),
                                             vmem_limit_bytes=VMEM_LIMIT),
        name=name,
    )(*operands, *extras)


def _mm_ksplit_body(a_ref, w_ref, r_ref, o_ref):
    @pl.when(pl.program_id(2) == 0)
    def _():
        o_ref[...] = r_ref[...]

    o_ref[...] += _dot(a_ref[...], w_ref[...].astype(BF16))


def _matmul_ksplit(a, w, resid, *, bm, bn, bk, name):
    m, kdim = a.shape
    n = w.shape[1]
    assert m % bm == 0 and n % bn == 0 and kdim % bk == 0 and a.dtype == BF16
    return pl.pallas_call(
        _mm_ksplit_body,
        grid=(n // bn, m // bm, kdim // bk),
        in_specs=[pl.BlockSpec((bm, bk), lambda j, i, k: (i, k)),
                  pl.BlockSpec((bk, bn), lambda j, i, k: (k, j)),
                  pl.BlockSpec((bm, bn), lambda j, i, k: (i, j), pipeline_mode=pl.Buffered(1))],
        out_specs=pl.BlockSpec((bm, bn), lambda j, i, k: (i, j), pipeline_mode=pl.Buffered(1)),
        out_shape=jax.ShapeDtypeStruct((m, n), F32),
        compiler_params=pltpu.CompilerParams(dimension_semantics=("arbitrary", "arbitrary", "arbitrary"),
                                             vmem_limit_bytes=VMEM_LIMIT),
        name=name,
    )(a, w, resid)


def _rmsnorm_body(*refs, n_main):
    g_ref, o_ref = refs[-2:]
    if n_main is None:
        o_ref[...] = _rms(refs[0][...], g_ref[...]).astype(o_ref.dtype)
        return
    i = pl.program_id(0)

    @pl.when(i < n_main)
    def _():
        o_ref[...] = _rms(refs[0][...], g_ref[...]).astype(o_ref.dtype)

    @pl.when(i >= n_main)
    def _():
        o_ref[...] = _rms(refs[1][...], g_ref[...]).astype(o_ref.dtype)


def _rmsnorm_rows(x, g, *, x_tail=None, bm=256, name="rmsnorm"):
    m, d = x.shape
    assert m % bm == 0
    n_main = m // bm
    in_specs = [pl.BlockSpec((bm, d), lambda i: (jnp.minimum(i, n_main - 1), 0))]
    operands = [x]
    if x_tail is not None:
        assert x_tail.shape[0] % bm == 0
        in_specs.append(pl.BlockSpec((bm, d), lambda i: (jnp.maximum(i - n_main, 0), 0)))
        operands.append(x_tail)
        m += x_tail.shape[0]
    in_specs.append(pl.BlockSpec((1, d), lambda i: (0, 0)))
    return pl.pallas_call(
        functools.partial(_rmsnorm_body, n_main=None if x_tail is None else n_main),
        grid=(m // bm,),
        in_specs=in_specs,
        out_specs=pl.BlockSpec((bm, d), lambda i: (i, 0)),
        out_shape=jax.ShapeDtypeStruct((m, d), BF16),
        compiler_params=pltpu.CompilerParams(dimension_semantics=("arbitrary",)),
        name=name,
    )(*operands, g.reshape(1, d))


def _ld_rows(ref, start, dil, n=BAND):
    if dil == 1:
        return ref[pl.ds(start, n), :]
    return ref[pl.ds(start, n, stride=dil), :]


def _st_rows(ref, g, start, dil, val):
    if dil == 1:
        ref[g, pl.ds(start, BAND), :] = val
    else:
        ref[g, pl.ds(start, BAND, stride=dil), :] = val


def _attn_prompt_body(slope_ref, gq_ref, gk_ref,
                      q0, k0, v0, kp0, vp0, q1, k1, v1, kp1, vp1, q2, k2, v2, kp2, vp2,
                      o_ref, kn0, kn1, kn2,
                      qn_s, kn_s, vv_s, og_s, lse_s, *, sb, dils):
    first_block = pl.program_id(1) == 0
    groups = ((q0, k0, v0, kp0, vp0, kn0), (q1, k1, v1, kp1, vp1, kn1), (q2, k2, v2, kp2, vp2, kn2))
    ii = lax.broadcasted_iota(jnp.int32, (BAND, BAND), 0)
    jj = lax.broadcasted_iota(jnp.int32, (BAND, BAND), 1)
    dist_cur = (ii - jj).astype(F32)
    dist_prev = (BAND + ii - jj).astype(F32)
    ok_cur = jj <= ii
    ok_prev = jj >= ii
    neg_inf = jnp.float32(-jnp.inf)
    base = kn_s.shape[0] - sb

    for g, dil in enumerate(dils):
        q_ref, k_ref, v_ref, kp_ref, vp_ref, kn_out = groups[g]
        pr = BAND * dil
        gq = gq_ref[g:g + 1, :]
        gk = gk_ref[g:g + 1, :]
        slope = slope_ref[0, :, g * HEAD_DIM:(g + 1) * HEAD_DIM]
        qn_s[...] = _rms(q_ref[...], gq)
        kn = _rms(k_ref[...], gk)
        kn_out[...] = kn
        kn_s[pl.ds(base, sb), :] = kn
        kn_s[pl.ds(base - pr, pr), :] = _rms(kp_ref[...], gk)
        vv_s[pl.ds(base, sb), :] = v_ref[...]
        vv_s[pl.ds(base - pr, pr), :] = vp_ref[...]
        bias_cur = jnp.where(ok_cur, -slope * (dist_cur * dil), neg_inf)
        bias_prev = jnp.where(ok_prev, -slope * (dist_prev * dil), neg_inf)
        bias = jnp.concatenate([bias_prev, bias_cur], axis=1)
        bias_first = jnp.concatenate([jnp.where(first_block, neg_inf, bias_prev), bias_cur], axis=1)
        ones = jnp.ones((2 * BAND, HEAD_DIM), BF16)
        for r in range(dil):
            for s in range(sb // pr):
                c0 = s * pr + r
                qs = _ld_rows(qn_s, c0, dil).astype(BF16)
                kk = _ld_rows(kn_s, base + c0 - pr, dil, 2 * BAND).astype(BF16)
                vv = _ld_rows(vv_s, base + c0 - pr, dil, 2 * BAND).astype(BF16)
                lg = _dot_nt(qs, kk) * ATTN_SCALE + (bias_first if s == 0 else bias)
                mx = jnp.max(lg, axis=-1, keepdims=True)
                p = jnp.exp(lg - mx).astype(BF16)
                oe = _dot(p, jnp.concatenate([vv, ones], axis=1))
                den = oe[:, HEAD_DIM:]
                _st_rows(og_s, g, c0, dil, oe[:, :HEAD_DIM] / den)
                _st_rows(lse_s, g, c0, dil, mx + jnp.log(den))

    l0, l1, l2 = lse_s[0], lse_s[1], lse_s[2]
    mx = jnp.maximum(jnp.maximum(l0, l1), l2)
    w0, w1, w2 = jnp.exp(l0 - mx), jnp.exp(l1 - mx), jnp.exp(l2 - mx)
    o = (w0 * og_s[0] + w1 * og_s[1] + w2 * og_s[2]) / (w0 + w1 + w2)
    o_ref[...] = o.astype(o_ref.dtype)


def _alibi_slopes():
    n = N_GROUPS * A_HEADS
    e = jnp.arange(1, n + 1, dtype=F32)
    return jnp.exp2(-8.0 * e / n).reshape(N_GROUPS, A_HEADS)


def _attn_prompt(z, g_q, g_k, *, seq, sb, dils, out_rows=None, n_heads=A_HEADS, col_q=COL_Q, col_k=COL_K,
                 col_v=COL_V):
    out_rows = seq if out_rows is None else out_rows
    nb = seq // sb
    slopes = _alibi_slopes()
    slope_arr = jnp.broadcast_to(slopes.T[:, None, :, None], (n_heads, 1, N_GROUPS, HEAD_DIM))
    slope_arr = slope_arr.reshape(n_heads, 1, N_GROUPS * HEAD_DIM)
    a_width = n_heads * HEAD_DIM
    in_specs = [pl.BlockSpec((1, 1, N_GROUPS * HEAD_DIM), lambda h, i: (h, 0, 0)),
                pl.BlockSpec((N_GROUPS, HEAD_DIM), lambda h, i: (0, 0)),
                pl.BlockSpec((N_GROUPS, HEAD_DIM), lambda h, i: (0, 0))]
    operands = [slope_arr, g_q, g_k]
    max_pr = BAND * max(dils)
    for g, dil in enumerate(dils):
        pr = BAND * dil
        ratio = sb // pr
        cq = (col_q + g * a_width) // HEAD_DIM
        ck = (col_k + g * a_width) // HEAD_DIM
        cv = (col_v + g * a_width) // HEAD_DIM
        cur = lambda c: pl.BlockSpec((sb, HEAD_DIM), lambda h, i, c=c: (i, c + h))
        prev = lambda c: pl.BlockSpec((pr, HEAD_DIM),
                                      lambda h, i, c=c, ratio=ratio: (jnp.maximum(i * ratio - 1, 0), c + h))
        in_specs += [cur(cq), cur(ck), cur(cv), prev(ck), prev(cv)]
        operands += [z, z, z, z, z]
    out_block = pl.BlockSpec((sb, HEAD_DIM), lambda h, i: (i, h))
    out_shape = (jax.ShapeDtypeStruct((out_rows, a_width), BF16),) + tuple(
        jax.ShapeDtypeStruct((seq, a_width), F32) for _ in dils)
    body = functools.partial(_attn_prompt_body, sb=sb, dils=tuple(dils))
    return pl.pallas_call(
        body,
        grid=(n_heads, nb),
        in_specs=in_specs,
        out_specs=(out_block,) * 4,
        out_shape=out_shape,
        scratch_shapes=[pltpu.VMEM((sb, HEAD_DIM), F32),
                        pltpu.VMEM((sb + max_pr, HEAD_DIM), F32),
                        pltpu.VMEM((sb + max_pr, HEAD_DIM), F32),
                        pltpu.VMEM((N_GROUPS, sb, HEAD_DIM), F32),
                        pltpu.VMEM((N_GROUPS, sb, HEAD_DIM), F32)],
        compiler_params=pltpu.CompilerParams(dimension_semantics=("arbitrary", "arbitrary"),
                                             vmem_limit_bytes=VMEM_LIMIT),
        name="attn_prompt",
    )(*operands)


def _sample_problems(dil, t_new):
    n_prob = min(dil, t_new)
    return n_prob, t_new // n_prob


def _sample_bias_tables(groups, t_new, n_heads):
    n_soft = len(groups) * n_heads
    tabs_c, tabs_n = [], []
    for g, (window, dil) in enumerate(groups):
        n_prob, tok_per = _sample_problems(dil, t_new)
        slopes = 2.0 ** (-8.0 * (g * n_heads + np.arange(n_heads) + 1) / n_soft)
        rows = np.arange(tok_per * n_heads)
        i, hp = rows // n_heads, rows % n_heads
        cols = np.arange(BAND * n_heads)
        m, h = cols // n_heads, cols % n_heads
        dist = window + i[:, None] * n_prob - m[None, :] * dil
        ok = (h[None, :] == hp[:, None]) & (dist <= window) & (dist > 0) & (dist % dil == 0)
        tabs_c.append(np.where(ok, -slopes[hp][:, None] * dist, -np.inf).astype(np.float32))
        cols = np.arange(t_new * n_heads)
        s, h = cols // n_heads, cols % n_heads
        per_p = []
        for p in range(n_prob):
            dist = (p + i * n_prob)[:, None] - s[None, :]
            ok = (h[None, :] == hp[:, None]) & (dist >= 0) & (dist % dil == 0)
            per_p.append(np.where(ok, -slopes[hp][:, None] * dist, -np.inf).astype(np.float32))
        tabs_n.append(np.stack(per_p))
    return tabs_c, tabs_n


def _attn_sample_body(gq_ref, gk_ref, q_ref, k_ref, v_ref, c0_ref, c1_ref, c2_ref,
                      bc0, bc1, bc2, bn0, bn1, bn2, o_ref, kn_ref, og_s, lse_s, *, t_new, groups, n_heads):
    caches = (c0_ref, c1_ref, c2_ref)
    bias_c = (bc0, bc1, bc2)
    bias_n = (bn0, bn1, bn2)
    kv_rows = 2 * n_heads
    for g, (window, dil) in enumerate(groups):
        c_ref = caches[g]
        n_prob, tok_per = _sample_problems(dil, t_new)
        qn = _rms(q_ref[0, g], gq_ref[g:g + 1, :])
        kn = _rms(k_ref[0, g], gk_ref[g:g + 1, :])
        kn_ref[0, g] = kn
        knb = kn.astype(BF16)
        vnb = v_ref[0, g].astype(BF16)
        for p in range(n_prob):
            toks = [p + i * n_prob for i in range(tok_per)]
            parts = [qn[t * n_heads:(t + 1) * n_heads] for t in toks]
            qp = (parts[0] if tok_per == 1 else jnp.concatenate(parts, axis=0)).astype(BF16)
            kc = c_ref[:, p * kv_rows:p * kv_rows + n_heads, :].reshape(BAND * n_heads, HEAD_DIM).astype(BF16)
            vc = c_ref[:, p * kv_rows + n_heads:(p + 1) * kv_rows, :].reshape(BAND * n_heads, HEAD_DIM).astype(BF16)
            lc = _dot_nt(qp, kc) * ATTN_SCALE + bias_c[g][...]
            ln = _dot_nt(qp, knb) * ATTN_SCALE + bias_n[g][p]
            mx = jnp.maximum(jnp.max(lc, axis=-1, keepdims=True), jnp.max(ln, axis=-1, keepdims=True))
            pc = jnp.exp(lc - mx)
            pn = jnp.exp(ln - mx)
            ssum = jnp.sum(pc, axis=-1, keepdims=True) + jnp.sum(pn, axis=-1, keepdims=True)
            o = (_dot(pc.astype(BF16), vc) + _dot(pn.astype(BF16), vnb)) / ssum
            lse = jnp.broadcast_to(mx + jnp.log(ssum), o.shape)
            for i, t in enumerate(toks):
                og_s[g, t * n_heads:(t + 1) * n_heads, :] = o[i * n_heads:(i + 1) * n_heads]
                lse_s[g, t * n_heads:(t + 1) * n_heads, :] = lse[i * n_heads:(i + 1) * n_heads]
    l0, l1, l2 = lse_s[0], lse_s[1], lse_s[2]
    mx = jnp.maximum(jnp.maximum(l0, l1), l2)
    w0, w1, w2 = jnp.exp(l0 - mx), jnp.exp(l1 - mx), jnp.exp(l2 - mx)
    o_ref[0] = (w0 * og_s[0] + w1 * og_s[1] + w2 * og_s[2]) / (w0 + w1 + w2)


def _attn_sample(qs, ks, vs, caches, g_q, g_k, *, n_batch, t_new, groups, n_heads=A_HEADS):
    n_g = len(groups)
    rows = t_new * n_heads
    tabs_c, tabs_n = _sample_bias_tables(groups, t_new, n_heads)
    new_spec = lambda: pl.BlockSpec((1, n_g, rows, HEAD_DIM), lambda b: (b, 0, 0, 0))
    in_specs = [pl.BlockSpec((n_g, HEAD_DIM), lambda b: (0, 0)),
                pl.BlockSpec((n_g, HEAD_DIM), lambda b: (0, 0)),
                new_spec(), new_spec(), new_spec()]
    operands = [g_q, g_k, qs, ks, vs]
    for c, (window, dil) in zip(caches, groups):
        assert c.shape[1] == window and window == BAND * dil
        n_prob, _ = _sample_problems(dil, t_new)
        c3 = c.reshape(n_batch * BAND, dil * 2 * n_heads, HEAD_DIM)
        in_specs.append(pl.BlockSpec((BAND, n_prob * 2 * n_heads, HEAD_DIM), lambda b: (b, 0, 0)))
        operands.append(c3)
    for tab in tabs_c:
        in_specs.append(pl.BlockSpec(tab.shape, lambda b: (0, 0)))
        operands.append(jnp.asarray(tab))
    for tab in tabs_n:
        in_specs.append(pl.BlockSpec(tab.shape, lambda b: (0, 0, 0)))
        operands.append(jnp.asarray(tab))
    body = functools.partial(_attn_sample_body, t_new=t_new, groups=tuple(groups), n_heads=n_heads)
    return pl.pallas_call(
        body,
        grid=(n_batch,),
        in_specs=in_specs,
        out_specs=(pl.BlockSpec((1, rows, HEAD_DIM), lambda b: (b, 0, 0)), new_spec()),
        out_shape=(jax.ShapeDtypeStruct((n_batch, rows, HEAD_DIM), F32),
                   jax.ShapeDtypeStruct((n_batch, n_g, rows, HEAD_DIM), F32)),
        scratch_shapes=[pltpu.VMEM((n_g, rows, HEAD_DIM), F32), pltpu.VMEM((n_g, rows, HEAD_DIM), F32)],
        compiler_params=pltpu.CompilerParams(dimension_semantics=("arbitrary",),
                                             vmem_limit_bytes=VMEM_LIMIT),
        name="attn_sample",
    )(*operands)


def _split3(x):
    hi = x.astype(BF16)
    r1 = x - hi.astype(F32)
    mid = r1.astype(BF16)
    lo = (r1 - mid.astype(F32)).astype(BF16)
    return hi, mid, lo


def _hgrn_heads(hq, hf, hi, hog, lb, g_out, sts, *, tb, n_valid=None):
    c = HG_CHUNK
    nh = len(sts)
    sts = list(sts)
    q = hq * _sigmoid(hq) * (HG_KDIM ** -0.5)
    gate = lb + (1.0 - lb) * _sigmoid(hf)
    log_g = jnp.log(gate)
    k = (1.0 - lb) * _sigmoid(-hf)
    if n_valid is not None:
        rows = lax.broadcasted_iota(jnp.int32, hq.shape, 0)
        log_g = jnp.where(rows < n_valid, log_g, 0.0)
        k = jnp.where(rows < n_valid, k, 0.0)
        q = jnp.where(rows < n_valid, q, 0.0)
    row = lax.broadcasted_iota(jnp.int32, (tb, tb), 0)
    col = lax.broadcasted_iota(jnp.int32, (tb, tb), 1)
    same = (row // c) == (col // c)
    tri_ok = same & (col <= row)
    tri = jnp.where(tri_ok, 1.0, 0.0).astype(BF16)
    blk = jnp.where(same, 1.0, 0.0).astype(BF16)
    p_hi, p_mid, p_lo = _split3(log_g)
    b = _dot(tri, p_hi) + _dot(tri, p_mid) + _dot(tri, p_lo)
    b_last = _dot(blk, p_hi) + _dot(blk, p_mid) + _dot(blk, p_lo)
    q_dec = (q * jnp.exp(b)).astype(BF16)
    k_dec = (k * jnp.exp(-b)).astype(BF16)
    k_end = (k * jnp.exp(b_last - b)).astype(BF16)
    decay = jnp.exp(b_last)
    vb = hi.astype(BF16)
    head = lambda x, h: x[:, h * HG_KDIM:(h + 1) * HG_KDIM]
    o_intra = []
    for h in range(nh):
        a = jnp.where(tri_ok, _dot_nt(head(q_dec, h), head(k_dec, h)), 0.0)
        o_intra.append(_dot(a.astype(BF16), head(vb, h)))
    parts = [[] for _ in range(nh)]
    for ci in range(tb // c):
        sl = slice(ci * c, (ci + 1) * c)
        for h in range(nh):
            parts[h].append(_dot_nt(head(q_dec, h)[sl], sts[h].astype(BF16)))
            sts[h] = sts[h] * head(decay, h)[ci * c:ci * c + 1, :] + _dot_tn(head(vb, h)[sl], head(k_end, h)[sl])
    outs = []
    for h in range(nh):
        o = o_intra[h] + (jnp.concatenate(parts[h], axis=0) if len(parts[h]) > 1 else parts[h][0])
        outs.append(_rms(o, g_out) * _sigmoid(head(hog, h)))
    return outs, sts


def _hgrn_prompt_body(hq_ref, hf_ref, hi_ref, hog_ref, lb_ref, go_ref, o_ref, s_ref, st_s, *, tb, nt, hb):
    t = pl.program_id(1)

    @pl.when(t == 0)
    def _():
        st_s[...] = jnp.zeros_like(st_s)

    outs, sts = _hgrn_heads(hq_ref[...], hf_ref[...], hi_ref[...], hog_ref[...], lb_ref[0], go_ref[...],
                            [st_s[hh] for hh in range(hb)], tb=tb)
    for hh in range(hb):
        o_ref[:, hh * HG_VDIM:(hh + 1) * HG_VDIM] = outs[hh].astype(o_ref.dtype)
        st_s[hh] = sts[hh]

    @pl.when(t == nt - 1)
    def _():
        for hh in range(hb):
            s_ref[hh] = sts[hh].T


def _hgrn_prompt(z, lb, g_out, *, seq, tb, hb, out_rows=None, n_heads=HG_HEADS, col_hq=COL_HQ, col_hf=COL_HF,
                 col_hi=COL_HI, col_hog=COL_HOG):
    out_rows = seq if out_rows is None else out_rows
    nt = seq // tb
    bw = hb * HG_KDIM
    assert all(c0 % bw == 0 for c0 in (col_hq, col_hf, col_hi, col_hog))
    blk = lambda c0: pl.BlockSpec((tb, bw), lambda h, t, c=c0 // bw: (t, c + h))
    body = functools.partial(_hgrn_prompt_body, tb=tb, nt=nt, hb=hb)
    return pl.pallas_call(
        body,
        grid=(n_heads // hb, nt),
        in_specs=[blk(col_hq), blk(col_hf), blk(col_hi), blk(col_hog),
                  pl.BlockSpec((1, 1, bw), lambda h, t: (h, 0, 0)),
                  pl.BlockSpec((1, HG_VDIM), lambda h, t: (0, 0))],
        out_specs=(pl.BlockSpec((tb, bw), lambda h, t: (t, h)),
                   pl.BlockSpec((hb, HG_KDIM, HG_VDIM), lambda h, t: (h, 0, 0))),
        out_shape=(jax.ShapeDtypeStruct((out_rows, n_heads * HG_VDIM), BF16),
                   jax.ShapeDtypeStruct((n_heads, HG_KDIM, HG_VDIM), F32)),
        scratch_shapes=[pltpu.VMEM((hb, HG_VDIM, HG_KDIM), F32)],
        compiler_params=pltpu.CompilerParams(dimension_semantics=("arbitrary", "arbitrary"),
                                             vmem_limit_bytes=VMEM_LIMIT),
        name="hgrn_prompt",
    )(z, z, z, z, lb.reshape(n_heads // hb, 1, bw), g_out.reshape(1, HG_VDIM))


def _hgrn_sample_body(hq_ref, hf_ref, hi_ref, hog_ref, lb_ref, go_ref, s0_ref, o_ref, s_ref, *, t_new, n_heads):
    pad = jnp.zeros((HG_CHUNK - t_new, n_heads * HG_KDIM), F32)
    ext = lambda ref: jnp.concatenate([ref[...], pad], axis=0)
    outs, sts = _hgrn_heads(ext(hq_ref), ext(hf_ref), ext(hi_ref), ext(hog_ref), lb_ref[...], go_ref[...],
                            [s0_ref[0, h].T for h in range(n_heads)], tb=HG_CHUNK, n_valid=t_new)
    for h in range(n_heads):
        o_ref[:, h * HG_VDIM:(h + 1) * HG_VDIM] = outs[h][:t_new]
        s_ref[0, h] = sts[h].T


def _hgrn_sample(hq, hf, hi, hog, lb, g_out, s0, *, n_batch, t_new, n_heads=HG_HEADS):
    width = n_heads * HG_KDIM
    row = lambda: pl.BlockSpec((t_new, width), lambda b: (b, 0))
    st = lambda: pl.BlockSpec((1, n_heads, HG_KDIM, HG_VDIM), lambda b: (b, 0, 0, 0))
    body = functools.partial(_hgrn_sample_body, t_new=t_new, n_heads=n_heads)
    return pl.pallas_call(
        body,
        grid=(n_batch,),
        in_specs=[row(), row(), row(), row(),
                  pl.BlockSpec((1, width), lambda b: (0, 0)),
                  pl.BlockSpec((1, HG_VDIM), lambda b: (0, 0)),
                  st()],
        out_specs=(row(), st()),
        out_shape=(jax.ShapeDtypeStruct((n_batch * t_new, width), F32),
                   jax.ShapeDtypeStruct(s0.shape, F32)),
        compiler_params=pltpu.CompilerParams(dimension_semantics=("arbitrary",),
                                             vmem_limit_bytes=VMEM_LIMIT),
        name="hgrn_sample",
    )(hq, hf, hi, hog, lb.reshape(1, width), g_out.reshape(1, HG_VDIM), s0)


def _tile_spec(bm, bn, col0=0):
    assert col0 % bn == 0
    cb = col0 // bn
    return pl.BlockSpec((bm, bn), lambda i, j: (i, cb + j))


def kernel(x_prompt, x_sample, cache_kv_w128, cache_kv_w512, cache_kv_w2048, state_hgrn, p_prompt, p_sample,
           g_mix, w_in, g_q, g_k, hg_lb_raw, g_hg_out, w_up_attn, w_up_hgrn, w_out, g_ffn, w_ff_up, w_ff_down,
           w_ple, w_ple_gate):
    bm, bn = 2816, 256
    bm_ple = 1408
    bm_res = 1408
    xp, xs = x_prompt.reshape(SEQ, D_MODEL), x_sample.reshape(N_SAMPLE, D_MODEL)
    pp, ps = p_prompt.reshape(SEQ, PLE_DIM), p_sample.reshape(N_SAMPLE, PLE_DIM)
    lb = jnp.cumsum(jax.nn.softmax(hg_lb_raw.astype(F32), axis=0), axis=0)[0]
    rows = lambda width, dt: jax.ShapeDtypeStruct((M_ALL, width), dt)

    def plain(accs, i, j, a_refs, extra, outs):
        _store(outs, accs[0])

    n_mix = _rmsnorm_rows(xp, g_mix[0], x_tail=xs, name="norm_mix")
    z, = _matmul_rows([(n_mix, w_in[0])], bm=bm, bn=bn, out_shapes=(rows(IN_WIDTH, F32),), epilogue=plain,
                      name="in_proj")

    o_attn, kn0, kn1, kn2 = _attn_prompt(z, g_q[0], g_k[0], seq=SEQ, sb=2048, out_rows=M_ALL,
                                         dils=tuple(d for _, d in DIL_GROUPS))
    zs = z[SEQ:]
    def by_group(col0):
        a = zs[:, col0:col0 + N_GROUPS * A_WIDTH].reshape(DEC_BATCH, DEC_SEQ, N_GROUPS, A_HEADS, HEAD_DIM)
        return a.transpose(0, 2, 1, 3, 4).reshape(DEC_BATCH, N_GROUPS, DEC_SEQ * A_HEADS, HEAD_DIM)

    caches = [c[0] for c in (cache_kv_w128, cache_kv_w512, cache_kv_w2048)]
    o_attn_s, kn_s = _attn_sample(by_group(COL_Q), by_group(COL_K), by_group(COL_V), caches,
                                  g_q[0], g_k[0], n_batch=DEC_BATCH, t_new=DEC_SEQ, groups=DIL_GROUPS)
    o_attn = lax.dynamic_update_slice(o_attn, o_attn_s.reshape(N_SAMPLE, A_WIDTH).astype(BF16), (SEQ, 0))

    o_hg, st_p = _hgrn_prompt(z, lb, g_hg_out[0], seq=SEQ, tb=256, hb=8, out_rows=M_ALL)
    o_hg_s, st_s = _hgrn_sample(zs[:, COL_HQ:COL_HF], zs[:, COL_HF:COL_HI], zs[:, COL_HI:COL_HOG],
                                zs[:, COL_HOG:COL_GA], lb, g_hg_out[0], state_hgrn[0],
                                n_batch=DEC_BATCH, t_new=DEC_SEQ)
    o_hg = lax.dynamic_update_slice(o_hg, o_hg_s.astype(BF16), (SEQ, 0))

    def merge_epilogue(accs, i, j, a_refs, extra, outs):
        _store(outs, _sigmoid(extra[0][...]) * accs[0] + _sigmoid(extra[1][...]) * accs[1])

    def out_proj_epilogue(accs, i, j, a_refs, extra, outs):
        xp_ref, xs_ref = extra
        outs[0][...] = xp_ref[...] + accs[0]

        @pl.when(i == pl.num_programs(0) - 1)
        def _():
            outs[0][bm_res - N_SAMPLE:, :] = xs_ref[...] + accs[0][bm_res - N_SAMPLE:, :]

    merged, = _matmul_rows([(o_attn, w_up_attn[0]), (o_hg, w_up_hgrn[0])], bm=bm, bn=bn, extras=(z, z),
                           extra_specs=(_tile_spec(bm, bn, COL_GA), _tile_spec(bm, bn, COL_GB)),
                           out_shapes=(rows(D_MODEL, BF16),), epilogue=merge_epilogue, name="up_merge")
    x1, = _matmul_rows([(merged, w_out[0])], bm=bm_res, bn=bn, extras=(xp, xs),
                       extra_specs=(_tile_spec(bm_res, bn), pl.BlockSpec((N_SAMPLE, bn), lambda i, j: (0, j))),
                       out_shapes=(rows(D_MODEL, F32),), epilogue=out_proj_epilogue, name="out_proj")

    def ffn_up_epilogue(accs, i, j, a_refs, extra, outs):
        _store(outs, jnp.square(jnp.maximum(accs[0], 0.0)))

    h_ffn = _rmsnorm_rows(x1, g_ffn[0], name="norm_ffn")
    hid, = _matmul_rows([(h_ffn, w_ff_up[0])], bm=bm, bn=bn, out_shapes=(rows(FFN_HIDDEN, BF16),),
                        epilogue=ffn_up_epilogue, name="ffn_up")
    x2 = _matmul_ksplit(hid, w_ff_down[0], x1, bm=1408, bn=1024, bk=2048, name="ffn_down")

    n_ple_blocks = M_ALL // bm_ple

    def ple_epilogue(accs, i, j, a_refs, extra, outs):
        pp_ref, ps_ref, wp_ref = extra
        yp_ref, ys_ref = outs
        x2_tile = a_refs[0][:, pl.ds(pl.multiple_of(j * bn, bn), bn)]
        gate = _sigmoid(accs[0])
        wp = wp_ref[...].astype(BF16)
        yp_ref[...] = x2_tile + gate * _dot(pp_ref[...].astype(BF16), wp)

        @pl.when(i == n_ple_blocks - 1)
        def _():
            s0 = bm_ple - N_SAMPLE
            ys_ref[...] = x2_tile[s0:, :] + gate[s0:, :] * _dot(ps_ref[...].astype(BF16), wp)

    y_p, y_s = _matmul_rows(
        [(x2, w_ple_gate[0])], bm=bm_ple, bn=bn, extras=(pp, ps, w_ple[0]),
        extra_specs=(pl.BlockSpec((bm_ple, PLE_DIM), lambda i, j: (i, 0)),
                     pl.BlockSpec((N_SAMPLE, PLE_DIM), lambda i, j: (0, 0)),
                     pl.BlockSpec((PLE_DIM, bn), lambda i, j: (0, j))),
        out_shapes=(jax.ShapeDtypeStruct((SEQ, D_MODEL), F32), jax.ShapeDtypeStruct((N_SAMPLE, D_MODEL), F32)),
        out_specs=(pl.BlockSpec((bm_ple, bn), lambda i, j: (i, j)),
                   pl.BlockSpec((N_SAMPLE, bn), lambda i, j: (0, jnp.where(i == n_ple_blocks - 1, j, 0)))),
        epilogue=ple_epilogue, name="ple")

    y_prompt = y_p.reshape(1, SEQ, D_MODEL)
    y_sample = y_s.reshape(DEC_BATCH, DEC_SEQ, D_MODEL)
    kv_p, kv_s = [], []
    for g, (window, _) in enumerate(DIL_GROUPS):
        length = min(window, SEQ)
        kn_g = (kn0, kn1, kn2)[g][SEQ - length:].reshape(length, A_HEADS, HEAD_DIM)
        v_g = z[SEQ - length:SEQ, COL_V + g * A_WIDTH:COL_V + (g + 1) * A_WIDTH].reshape(length, A_HEADS, HEAD_DIM)
        kv_p.append(jnp.stack([kn_g, v_g], axis=1)[None, None])
        ks_g = kn_s[:, g].reshape(DEC_BATCH, DEC_SEQ, A_HEADS, HEAD_DIM)
        vs_g = zs[:, COL_V + g * A_WIDTH:COL_V + (g + 1) * A_WIDTH].reshape(DEC_BATCH, DEC_SEQ, A_HEADS, HEAD_DIM)
        kv_s.append(jnp.stack([ks_g, vs_g], axis=2)[None])
    return (y_prompt, y_sample, kv_p[0], kv_p[1], kv_p[2], st_p[None, None],
            kv_s[0], kv_s[1], kv_s[2], st_s[None])
```

```python
import functools

import numpy as np
import jax
import jax.numpy as jnp
from jax import lax
from jax.experimental import pallas as pl
from jax.experimental.pallas import tpu as pltpu

F32 = jnp.float32
BF16 = jnp.bfloat16

D_MODEL = 4096
SEQ = 8192
DEC_BATCH = 32
DEC_SEQ = 8
N_SAMPLE = DEC_BATCH * DEC_SEQ
M_ALL = SEQ + N_SAMPLE
DIL_GROUPS = ((128, 1), (512, 4), (2048, 16))
N_GROUPS = 3
A_HEADS = 8
HEAD_DIM = 128
A_WIDTH = A_HEADS * HEAD_DIM
BAND = 128
HG_HEADS = 16
HG_KDIM = 128
HG_VDIM = 128
HG_WIDTH = HG_HEADS * HG_VDIM
HG_CHUNK = 32
FFN_HIDDEN = 4 * D_MODEL
PLE_DIM = 256
NORM_EPS = 1e-6
ATTN_SCALE = HEAD_DIM ** -0.5

COL_Q = 0
COL_K = COL_Q + N_GROUPS * A_WIDTH
COL_V = COL_K + N_GROUPS * A_WIDTH
COL_HQ = COL_V + N_GROUPS * A_WIDTH
COL_HF = COL_HQ + HG_HEADS * HG_KDIM
COL_HI = COL_HF + HG_HEADS * HG_KDIM
COL_HOG = COL_HI + HG_WIDTH
COL_GA = COL_HOG + HG_WIDTH
COL_GB = COL_GA + D_MODEL
IN_WIDTH = COL_GB + D_MODEL

VMEM_LIMIT = 56 * 1024 * 1024


def _dot(a, b):
    return jnp.dot(a, b, preferred_element_type=F32)


def _dot_nt(a, b):
    return lax.dot_general(a, b, (((1,), (1,)), ((), ())), preferred_element_type=F32)


def _dot_tn(a, b):
    return lax.dot_general(a, b, (((0,), (0,)), ((), ())), preferred_element_type=F32)


def _rms(x, g):
    return x * lax.rsqrt(jnp.mean(x * x, axis=-1, keepdims=True) + NORM_EPS) * g


def _sigmoid(x):
    return 1.0 / (1.0 + jnp.exp(-x))


def _store(outs, *vals):
    for o_ref, v in zip(outs, vals):
        o_ref[...] = v.astype(o_ref.dtype)


_CAST_ROWS = 128
_DOT_ROWS = 704


def _mm_rows_body(*refs, needs_cast, n_extra, n_out, epilogue, row_chunk):
    n_pairs = len(needs_cast)
    a_refs = refs[0:2 * n_pairs:2]
    w_refs = refs[1:2 * n_pairs:2]
    pos = 2 * n_pairs
    extra = refs[pos:pos + n_extra]
    outs = refs[pos + n_extra:pos + n_extra + n_out]
    scratch = list(refs[pos + n_extra + n_out:])
    i, j = pl.program_id(0), pl.program_id(1)
    lhs_refs = []
    for a_ref, cast in zip(a_refs, needs_cast):
        if cast:
            ab_ref = scratch.pop(0)

            @pl.when(j == 0)
            def _(a_ref=a_ref, ab_ref=ab_ref):
                def rows(c, carry):
                    r0 = pl.multiple_of(c * _CAST_ROWS, _CAST_ROWS)
                    ab_ref[pl.ds(r0, _CAST_ROWS), :] = a_ref[pl.ds(r0, _CAST_ROWS), :].astype(BF16)
                    return carry
                lax.fori_loop(0, a_ref.shape[0] // _CAST_ROWS, rows, 0)

            lhs_refs.append(ab_ref)
        else:
            lhs_refs.append(a_ref)
    wbs = [w_ref[...].astype(BF16) for w_ref in w_refs]
    for r0 in range(0, lhs_refs[0].shape[0], row_chunk):
        rows = slice(r0, r0 + row_chunk)
        accs = [_dot(lhs_ref[rows, :], wb) for lhs_ref, wb in zip(lhs_refs, wbs)]
        epilogue(accs, i, j, rows, a_refs, extra, outs)


def _matmul_rows(pairs, *, bm, bn, extras=(), extra_specs=(), out_shapes, out_specs=None, epilogue, name,
                 row_chunk=_DOT_ROWS):
    m = pairs[0][0].shape[0]
    n = pairs[0][1].shape[1]
    assert m % bm == 0 and n % bn == 0 and bm % _CAST_ROWS == 0 and bm % row_chunk == 0
    in_specs, operands, scratch, needs_cast = [], [], [], []
    for a, w in pairs:
        kdim = w.shape[0]
        assert a.shape == (m, kdim) and w.shape[1] == n
        in_specs += [pl.BlockSpec((bm, kdim), lambda i, j: (i, 0), pipeline_mode=pl.Buffered(1)),
                     pl.BlockSpec((kdim, bn), lambda i, j: (0, j))]
        operands += [a, w]
        needs_cast.append(a.dtype != BF16)
        if needs_cast[-1]:
            scratch.append(pltpu.VMEM((bm, kdim), BF16))
    if out_specs is None:
        out_specs = tuple(pl.BlockSpec((bm, bn), lambda i, j: (i, j)) for _ in out_shapes)
    body = functools.partial(_mm_rows_body, needs_cast=tuple(needs_cast), n_extra=len(extras),
                             n_out=len(out_shapes), epilogue=epilogue, row_chunk=row_chunk)
    return pl.pallas_call(
        body,
        grid=(m // bm, n // bn),
        in_specs=in_specs + list(extra_specs),
        out_specs=tuple(out_specs),
        out_shape=tuple(out_shapes),
        scratch_shapes=scratch,
        compiler_params=pltpu.CompilerParams(dimension_semantics=("arbitrary", "arbitrary"),
                                             vmem_limit_bytes=VMEM_LIMIT),
        name=name,
    )(*operands, *extras)


def _mm_ksplit_body(a_ref, w_ref, r_ref, o_ref):
    @pl.when(pl.program_id(2) == 0)
    def _():
        o_ref[...] = r_ref[...]

    o_ref[...] += _dot(a_ref[...], w_ref[...].astype(BF16))


def _matmul_ksplit(a, w, resid, *, bm, bn, bk, name):
    m, kdim = a.shape
    n = w.shape[1]
    assert m % bm == 0 and n % bn == 0 and kdim % bk == 0 and a.dtype == BF16
    return pl.pallas_call(
        _mm_ksplit_body,
        grid=(n // bn, m // bm, kdim // bk),
        in_specs=[pl.BlockSpec((bm, bk), lambda j, i, k: (i, k)),
                  pl.BlockSpec((bk, bn), lambda j, i, k: (k, j)),
                  pl.BlockSpec((bm, bn), lambda j, i, k: (i, j), pipeline_mode=pl.Buffered(1))],
        out_specs=pl.BlockSpec((bm, bn), lambda j, i, k: (i, j), pipeline_mode=pl.Buffered(1)),
        out_shape=jax.ShapeDtypeStruct((m, n), F32),
        compiler_params=pltpu.CompilerParams(dimension_semantics=("arbitrary", "arbitrary", "arbitrary"),
                                             vmem_limit_bytes=VMEM_LIMIT),
        name=name,
    )(a, w, resid)


def _rmsnorm_body(*refs, n_main):
    g_ref, o_ref = refs[-2:]
    if n_main is None:
        o_ref[...] = _rms(refs[0][...], g_ref[...]).astype(o_ref.dtype)
        return
    i = pl.program_id(0)

    @pl.when(i < n_main)
    def _():
        o_ref[...] = _rms(refs[0][...], g_ref[...]).astype(o_ref.dtype)

    @pl.when(i >= n_main)
    def _():
        o_ref[...] = _rms(refs[1][...], g_ref[...]).astype(o_ref.dtype)


def _rmsnorm_rows(x, g, *, x_tail=None, bm=256, name="rmsnorm"):
    m, d = x.shape
    assert m % bm == 0
    n_main = m // bm
    in_specs = [pl.BlockSpec((bm, d), lambda i: (jnp.minimum(i, n_main - 1), 0))]
    operands = [x]
    if x_tail is not None:
        assert x_tail.shape[0] % bm == 0
        in_specs.append(pl.BlockSpec((bm, d), lambda i: (jnp.maximum(i - n_main, 0), 0)))
        operands.append(x_tail)
        m += x_tail.shape[0]
    in_specs.append(pl.BlockSpec((1, d), lambda i: (0, 0)))
    return pl.pallas_call(
        functools.partial(_rmsnorm_body, n_main=None if x_tail is None else n_main),
        grid=(m // bm,),
        in_specs=in_specs,
        out_specs=pl.BlockSpec((bm, d), lambda i: (i, 0)),
        out_shape=jax.ShapeDtypeStruct((m, d), BF16),
        compiler_params=pltpu.CompilerParams(dimension_semantics=("arbitrary",)),
        name=name,
    )(*operands, g.reshape(1, d))


def _ld_rows(ref, start, dil, n=BAND):
    if dil == 1:
        return ref[pl.ds(start, n), :]
    return ref[pl.ds(start, n, stride=dil), :]


def _st_rows(ref, g, start, dil, val):
    if dil == 1:
        ref[g, pl.ds(start, BAND), :] = val
    else:
        ref[g, pl.ds(start, BAND, stride=dil), :] = val


def _attn_prompt_body(slope_ref, gq_ref, gk_ref,
                      q0, k0, v0, kp0, vp0, q1, k1, v1, kp1, vp1, q2, k2, v2, kp2, vp2,
                      o_ref, kn0, kn1, kn2,
                      qn_s, kn_s, vv_s, og_s, lse_s, *, sb, dils):
    first_block = pl.program_id(1) == 0
    groups = ((q0, k0, v0, kp0, vp0, kn0), (q1, k1, v1, kp1, vp1, kn1), (q2, k2, v2, kp2, vp2, kn2))
    ii = lax.broadcasted_iota(jnp.int32, (BAND, BAND), 0)
    jj = lax.broadcasted_iota(jnp.int32, (BAND, BAND), 1)
    dist_cur = (ii - jj).astype(F32)
    dist_prev = (BAND + ii - jj).astype(F32)
    ok_cur = jj <= ii
    ok_prev = jj >= ii
    neg_inf = jnp.float32(-jnp.inf)
    base = kn_s.shape[0] - sb

    for g, dil in enumerate(dils):
        q_ref, k_ref, v_ref, kp_ref, vp_ref, kn_out = groups[g]
        pr = BAND * dil
        gq = gq_ref[g:g + 1, :]
        gk = gk_ref[g:g + 1, :]
        slope = slope_ref[0, :, g * HEAD_DIM:(g + 1) * HEAD_DIM]
        qn_s[...] = _rms(q_ref[...], gq)
        kn = _rms(k_ref[...], gk)
        kn_out[...] = kn
        kn_s[pl.ds(base, sb), :] = kn
        kn_s[pl.ds(base - pr, pr), :] = _rms(kp_ref[...], gk)
        vv_s[pl.ds(base, sb), :] = v_ref[...]
        vv_s[pl.ds(base - pr, pr), :] = vp_ref[...]
        bias_cur = jnp.where(ok_cur, -slope * (dist_cur * dil), neg_inf)
        bias_prev = jnp.where(ok_prev, -slope * (dist_prev * dil), neg_inf)
        bias = jnp.concatenate([bias_prev, bias_cur], axis=1)
        bias_first = jnp.concatenate([jnp.where(first_block, neg_inf, bias_prev), bias_cur], axis=1)
        ones = jnp.ones((2 * BAND, HEAD_DIM), BF16)
        for r in range(dil):
            for s in range(sb // pr):
                c0 = s * pr + r
                qs = _ld_rows(qn_s, c0, dil).astype(BF16)
                kk = _ld_rows(kn_s, base + c0 - pr, dil, 2 * BAND).astype(BF16)
                vv = _ld_rows(vv_s, base + c0 - pr, dil, 2 * BAND).astype(BF16)
                lg = _dot_nt(qs, kk) * ATTN_SCALE + (bias_first if s == 0 else bias)
                mx = jnp.max(lg, axis=-1, keepdims=True)
                p = jnp.exp(lg - mx).astype(BF16)
                oe = _dot(p, jnp.concatenate([vv, ones], axis=1))
                den = oe[:, HEAD_DIM:]
                _st_rows(og_s, g, c0, dil, oe[:, :HEAD_DIM] / den)
                _st_rows(lse_s, g, c0, dil, mx + jnp.log(den))

    l0, l1, l2 = lse_s[0], lse_s[1], lse_s[2]
    mx = jnp.maximum(jnp.maximum(l0, l1), l2)
    w0, w1, w2 = jnp.exp(l0 - mx), jnp.exp(l1 - mx), jnp.exp(l2 - mx)
    o = (w0 * og_s[0] + w1 * og_s[1] + w2 * og_s[2]) / (w0 + w1 + w2)
    o_ref[...] = o.astype(o_ref.dtype)


def _alibi_slopes():
    n = N_GROUPS * A_HEADS
    e = jnp.arange(1, n + 1, dtype=F32)
    return jnp.exp2(-8.0 * e / n).reshape(N_GROUPS, A_HEADS)


def _attn_prompt(z, g_q, g_k, *, seq, sb, dils, out_rows=None, n_heads=A_HEADS, col_q=COL_Q, col_k=COL_K,
                 col_v=COL_V):
    out_rows = seq if out_rows is None else out_rows
    nb = seq // sb
    slopes = _alibi_slopes()
    slope_arr = jnp.broadcast_to(slopes.T[:, None, :, None], (n_heads, 1, N_GROUPS, HEAD_DIM))
    slope_arr = slope_arr.reshape(n_heads, 1, N_GROUPS * HEAD_DIM)
    a_width = n_heads * HEAD_DIM
    in_specs = [pl.BlockSpec((1, 1, N_GROUPS * HEAD_DIM), lambda h, i: (h, 0, 0)),
                pl.BlockSpec((N_GROUPS, HEAD_DIM), lambda h, i: (0, 0)),
                pl.BlockSpec((N_GROUPS, HEAD_DIM), lambda h, i: (0, 0))]
    operands = [slope_arr, g_q, g_k]
    max_pr = BAND * max(dils)
    for g, dil in enumerate(dils):
        pr = BAND * dil
        ratio = sb // pr
        cq = (col_q + g * a_width) // HEAD_DIM
        ck = (col_k + g * a_width) // HEAD_DIM
        cv = (col_v + g * a_width) // HEAD_DIM
        cur = lambda c: pl.BlockSpec((sb, HEAD_DIM), lambda h, i, c=c: (i, c + h))
        prev = lambda c: pl.BlockSpec((pr, HEAD_DIM),
                                      lambda h, i, c=c, ratio=ratio: (jnp.maximum(i * ratio - 1, 0), c + h))
        in_specs += [cur(cq), cur(ck), cur(cv), prev(ck), prev(cv)]
        operands += [z, z, z, z, z]
    out_block = pl.BlockSpec((sb, HEAD_DIM), lambda h, i: (i, h))
    out_shape = (jax.ShapeDtypeStruct((out_rows, a_width), BF16),) + tuple(
        jax.ShapeDtypeStruct((seq, a_width), F32) for _ in dils)
    body = functools.partial(_attn_prompt_body, sb=sb, dils=tuple(dils))
    return pl.pallas_call(
        body,
        grid=(n_heads, nb),
        in_specs=in_specs,
        out_specs=(out_block,) * 4,
        out_shape=out_shape,
        scratch_shapes=[pltpu.VMEM((sb, HEAD_DIM), F32),
                        pltpu.VMEM((sb + max_pr, HEAD_DIM), F32),
                        pltpu.VMEM((sb + max_pr, HEAD_DIM), F32),
                        pltpu.VMEM((N_GROUPS, sb, HEAD_DIM), F32),
                        pltpu.VMEM((N_GROUPS, sb, HEAD_DIM), F32)],
        compiler_params=pltpu.CompilerParams(dimension_semantics=("arbitrary", "arbitrary"),
                                             vmem_limit_bytes=VMEM_LIMIT),
        name="attn_prompt",
    )(*operands)


def _sample_problems(dil, t_new):
    n_prob = min(dil, t_new)
    return n_prob, t_new // n_prob


def _sample_bias_tables(groups, t_new, n_heads):
    n_soft = len(groups) * n_heads
    tabs_c, tabs_n = [], []
    for g, (window, dil) in enumerate(groups):
        n_prob, tok_per = _sample_problems(dil, t_new)
        slopes = 2.0 ** (-8.0 * (g * n_heads + np.arange(n_heads) + 1) / n_soft)
        rows = np.arange(tok_per * n_heads)
        i, hp = rows // n_heads, rows % n_heads
        cols = np.arange(BAND * n_heads)
        m, h = cols // n_heads, cols % n_heads
        dist = window + i[:, None] * n_prob - m[None, :] * dil
        ok = (h[None, :] == hp[:, None]) & (dist <= window) & (dist > 0) & (dist % dil == 0)
        tabs_c.append(np.where(ok, -slopes[hp][:, None] * dist, -np.inf).astype(np.float32))
        cols = np.arange(t_new * n_heads)
        s, h = cols // n_heads, cols % n_heads
        per_p = []
        for p in range(n_prob):
            dist = (p + i * n_prob)[:, None] - s[None, :]
            ok = (h[None, :] == hp[:, None]) & (dist >= 0) & (dist % dil == 0)
            per_p.append(np.where(ok, -slopes[hp][:, None] * dist, -np.inf).astype(np.float32))
        tabs_n.append(np.stack(per_p))
    return tabs_c, tabs_n


def _attn_sample_body(gq_ref, gk_ref, q_ref, k_ref, v_ref, c0_ref, c1_ref, c2_ref,
                      bc0, bc1, bc2, bn0, bn1, bn2, o_ref, kn_ref, og_s, lse_s, *, t_new, groups, n_heads):
    caches = (c0_ref, c1_ref, c2_ref)
    bias_c = (bc0, bc1, bc2)
    bias_n = (bn0, bn1, bn2)
    kv_rows = 2 * n_heads
    for g, (window, dil) in enumerate(groups):
        c_ref = caches[g]
        n_prob, tok_per = _sample_problems(dil, t_new)
        qn = _rms(q_ref[0, g], gq_ref[g:g + 1, :])
        kn = _rms(k_ref[0, g], gk_ref[g:g + 1, :])
        kn_ref[0, g] = kn
        knb = kn.astype(BF16)
        vnb = v_ref[0, g].astype(BF16)
        for p in range(n_prob):
            toks = [p + i * n_prob for i in range(tok_per)]
            parts = [qn[t * n_heads:(t + 1) * n_heads] for t in toks]
            qp = (parts[0] if tok_per == 1 else jnp.concatenate(parts, axis=0)).astype(BF16)
            kc = c_ref[:, p * kv_rows:p * kv_rows + n_heads, :].reshape(BAND * n_heads, HEAD_DIM).astype(BF16)
            vc = c_ref[:, p * kv_rows + n_heads:(p + 1) * kv_rows, :].reshape(BAND * n_heads, HEAD_DIM).astype(BF16)
            lc = _dot_nt(qp, kc) * ATTN_SCALE + bias_c[g][...]
            ln = _dot_nt(qp, knb) * ATTN_SCALE + bias_n[g][p]
            mx = jnp.maximum(jnp.max(lc, axis=-1, keepdims=True), jnp.max(ln, axis=-1, keepdims=True))
            pc = jnp.exp(lc - mx)
            pn = jnp.exp(ln - mx)
            ssum = jnp.sum(pc, axis=-1, keepdims=True) + jnp.sum(pn, axis=-1, keepdims=True)
            o = (_dot(pc.astype(BF16), vc) + _dot(pn.astype(BF16), vnb)) / ssum
            lse = jnp.broadcast_to(mx + jnp.log(ssum), o.shape)
            for i, t in enumerate(toks):
                og_s[g, t * n_heads:(t + 1) * n_heads, :] = o[i * n_heads:(i + 1) * n_heads]
                lse_s[g, t * n_heads:(t + 1) * n_heads, :] = lse[i * n_heads:(i + 1) * n_heads]
    l0, l1, l2 = lse_s[0], lse_s[1], lse_s[2]
    mx = jnp.maximum(jnp.maximum(l0, l1), l2)
    w0, w1, w2 = jnp.exp(l0 - mx), jnp.exp(l1 - mx), jnp.exp(l2 - mx)
    o_ref[0] = (w0 * og_s[0] + w1 * og_s[1] + w2 * og_s[2]) / (w0 + w1 + w2)


def _attn_sample(qs, ks, vs, caches, g_q, g_k, *, n_batch, t_new, groups, n_heads=A_HEADS):
    n_g = len(groups)
    rows = t_new * n_heads
    tabs_c, tabs_n = _sample_bias_tables(groups, t_new, n_heads)
    new_spec = lambda: pl.BlockSpec((1, n_g, rows, HEAD_DIM), lambda b: (b, 0, 0, 0))
    in_specs = [pl.BlockSpec((n_g, HEAD_DIM), lambda b: (0, 0)),
                pl.BlockSpec((n_g, HEAD_DIM), lambda b: (0, 0)),
                new_spec(), new_spec(), new_spec()]
    operands = [g_q, g_k, qs, ks, vs]
    for c, (window, dil) in zip(caches, groups):
        assert c.shape[1] == window and window == BAND * dil
        n_prob, _ = _sample_problems(dil, t_new)
        c3 = c.reshape(n_batch * BAND, dil * 2 * n_heads, HEAD_DIM)
        in_specs.append(pl.BlockSpec((BAND, n_prob * 2 * n_heads, HEAD_DIM), lambda b: (b, 0, 0)))
        operands.append(c3)
    for tab in tabs_c:
        in_specs.append(pl.BlockSpec(tab.shape, lambda b: (0, 0)))
        operands.append(jnp.asarray(tab))
    for tab in tabs_n:
        in_specs.append(pl.BlockSpec(tab.shape, lambda b: (0, 0, 0)))
        operands.append(jnp.asarray(tab))
    body = functools.partial(_attn_sample_body, t_new=t_new, groups=tuple(groups), n_heads=n_heads)
    return pl.pallas_call(
        body,
        grid=(n_batch,),
        in_specs=in_specs,
        out_specs=(pl.BlockSpec((1, rows, HEAD_DIM), lambda b: (b, 0, 0)), new_spec()),
        out_shape=(jax.ShapeDtypeStruct((n_batch, rows, HEAD_DIM), F32),
                   jax.ShapeDtypeStruct((n_batch, n_g, rows, HEAD_DIM), F32)),
        scratch_shapes=[pltpu.VMEM((n_g, rows, HEAD_DIM), F32), pltpu.VMEM((n_g, rows, HEAD_DIM), F32)],
        compiler_params=pltpu.CompilerParams(dimension_semantics=("arbitrary",),
                                             vmem_limit_bytes=VMEM_LIMIT),
        name="attn_sample",
    )(*operands)


def _split3(x):
    hi = x.astype(BF16)
    r1 = x - hi.astype(F32)
    mid = r1.astype(BF16)
    lo = (r1 - mid.astype(F32)).astype(BF16)
    return hi, mid, lo


def _hgrn_heads(hq, hf, hi, hog, lb, g_out, sts, *, tb, n_valid=None):
    c = HG_CHUNK
    nh = len(sts)
    sts = list(sts)
    q = hq * _sigmoid(hq) * (HG_KDIM ** -0.5)
    gate = lb + (1.0 - lb) * _sigmoid(hf)
    log_g = jnp.log(gate)
    k = (1.0 - lb) * _sigmoid(-hf)
    if n_valid is not None:
        rows = lax.broadcasted_iota(jnp.int32, hq.shape, 0)
        log_g = jnp.where(rows < n_valid, log_g, 0.0)
        k = jnp.where(rows < n_valid, k, 0.0)
        q = jnp.where(rows < n_valid, q, 0.0)
    row = lax.broadcasted_iota(jnp.int32, (tb, tb), 0)
    col = lax.broadcasted_iota(jnp.int32, (tb, tb), 1)
    same = (row // c) == (col // c)
    tri_ok = same & (col <= row)
    tri = jnp.where(tri_ok, 1.0, 0.0).astype(BF16)
    blk = jnp.where(same, 1.0, 0.0).astype(BF16)
    p_hi, p_mid, p_lo = _split3(log_g)
    b = _dot(tri, p_hi) + _dot(tri, p_mid) + _dot(tri, p_lo)
    b_last = _dot(blk, p_hi) + _dot(blk, p_mid) + _dot(blk, p_lo)
    q_dec = (q * jnp.exp(b)).astype(BF16)
    k_dec = (k * jnp.exp(-b)).astype(BF16)
    k_end = (k * jnp.exp(b_last - b)).astype(BF16)
    decay = jnp.exp(b_last)
    vb = hi.astype(BF16)
    head = lambda x, h: x[:, h * HG_KDIM:(h + 1) * HG_KDIM]
    o_intra = []
    for h in range(nh):
        a = jnp.where(tri_ok, _dot_nt(head(q_dec, h), head(k_dec, h)), 0.0)
        o_intra.append(_dot(a.astype(BF16), head(vb, h)))
    parts = [[] for _ in range(nh)]
    for ci in range(tb // c):
        sl = slice(ci * c, (ci + 1) * c)
        for h in range(nh):
            parts[h].append(_dot_nt(head(q_dec, h)[sl], sts[h].astype(BF16)))
            sts[h] = sts[h] * head(decay, h)[ci * c:ci * c + 1, :] + _dot_tn(head(vb, h)[sl], head(k_end, h)[sl])
    outs = []
    for h in range(nh):
        o = o_intra[h] + (jnp.concatenate(parts[h], axis=0) if len(parts[h]) > 1 else parts[h][0])
        outs.append(_rms(o, g_out) * _sigmoid(head(hog, h)))
    return outs, sts


def _hgrn_prompt_body(hq_ref, hf_ref, hi_ref, hog_ref, lb_ref, go_ref, o_ref, s_ref, st_s, *, tb, nt, hb):
    t = pl.program_id(1)

    @pl.when(t == 0)
    def _():
        st_s[...] = jnp.zeros_like(st_s)

    outs, sts = _hgrn_heads(hq_ref[...], hf_ref[...], hi_ref[...], hog_ref[...], lb_ref[0], go_ref[...],
                            [st_s[hh] for hh in range(hb)], tb=tb)
    for hh in range(hb):
        o_ref[:, hh * HG_VDIM:(hh + 1) * HG_VDIM] = outs[hh].astype(o_ref.dtype)
        st_s[hh] = sts[hh]

    @pl.when(t == nt - 1)
    def _():
        for hh in range(hb):
            s_ref[hh] = sts[hh].T


def _hgrn_prompt(z, lb, g_out, *, seq, tb, hb, out_rows=None, n_heads=HG_HEADS, col_hq=COL_HQ, col_hf=COL_HF,
                 col_hi=COL_HI, col_hog=COL_HOG):
    out_rows = seq if out_rows is None else out_rows
    nt = seq // tb
    bw = hb * HG_KDIM
    assert all(c0 % bw == 0 for c0 in (col_hq, col_hf, col_hi, col_hog))
    blk = lambda c0: pl.BlockSpec((tb, bw), lambda h, t, c=c0 // bw: (t, c + h))
    body = functools.partial(_hgrn_prompt_body, tb=tb, nt=nt, hb=hb)
    return pl.pallas_call(
        body,
        grid=(n_heads // hb, nt),
        in_specs=[blk(col_hq), blk(col_hf), blk(col_hi), blk(col_hog),
                  pl.BlockSpec((1, 1, bw), lambda h, t: (h, 0, 0)),
                  pl.BlockSpec((1, HG_VDIM), lambda h, t: (0, 0))],
        out_specs=(pl.BlockSpec((tb, bw), lambda h, t: (t, h)),
                   pl.BlockSpec((hb, HG_KDIM, HG_VDIM), lambda h, t: (h, 0, 0))),
        out_shape=(jax.ShapeDtypeStruct((out_rows, n_heads * HG_VDIM), BF16),
                   jax.ShapeDtypeStruct((n_heads, HG_KDIM, HG_VDIM), F32)),
        scratch_shapes=[pltpu.VMEM((hb, HG_VDIM, HG_KDIM), F32)],
        compiler_params=pltpu.CompilerParams(dimension_semantics=("arbitrary", "arbitrary"),
                                             vmem_limit_bytes=VMEM_LIMIT),
        name="hgrn_prompt",
    )(z, z, z, z, lb.reshape(n_heads // hb, 1, bw), g_out.reshape(1, HG_VDIM))


def _hgrn_sample_body(hq_ref, hf_ref, hi_ref, hog_ref, lb_ref, go_ref, s0_ref, o_ref, s_ref, *, t_new, n_heads):
    pad = jnp.zeros((HG_CHUNK - t_new, n_heads * HG_KDIM), F32)
    ext = lambda ref: jnp.concatenate([ref[...], pad], axis=0)
    outs, sts = _hgrn_heads(ext(hq_ref), ext(hf_ref), ext(hi_ref), ext(hog_ref), lb_ref[...], go_ref[...],
                            [s0_ref[0, h].T for h in range(n_heads)], tb=HG_CHUNK, n_valid=t_new)
    for h in range(n_heads):
        o_ref[:, h * HG_VDIM:(h + 1) * HG_VDIM] = outs[h][:t_new]
        s_ref[0, h] = sts[h].T


def _hgrn_sample(hq, hf, hi, hog, lb, g_out, s0, *, n_batch, t_new, n_heads=HG_HEADS):
    width = n_heads * HG_KDIM
    row = lambda: pl.BlockSpec((t_new, width), lambda b: (b, 0))
    st = lambda: pl.BlockSpec((1, n_heads, HG_KDIM, HG_VDIM), lambda b: (b, 0, 0, 0))
    body = functools.partial(_hgrn_sample_body, t_new=t_new, n_heads=n_heads)
    return pl.pallas_call(
        body,
        grid=(n_batch,),
        in_specs=[row(), row(), row(), row(),
                  pl.BlockSpec((1, width), lambda b: (0, 0)),
                  pl.BlockSpec((1, HG_VDIM), lambda b: (0, 0)),
                  st()],
        out_specs=(row(), st()),
        out_shape=(jax.ShapeDtypeStruct((n_batch * t_new, width), F32),
                   jax.ShapeDtypeStruct(s0.shape, F32)),
        compiler_params=pltpu.CompilerParams(dimension_semantics=("arbitrary",),
                                             vmem_limit_bytes=VMEM_LIMIT),
        name="hgrn_sample",
    )(hq, hf, hi, hog, lb.reshape(1, width), g_out.reshape(1, HG_VDIM), s0)


def _tile_spec(bm, bn, col0=0):
    assert col0 % bn == 0
    cb = col0 // bn
    return pl.BlockSpec((bm, bn), lambda i, j: (i, cb + j))


def kernel(x_prompt, x_sample, cache_kv_w128, cache_kv_w512, cache_kv_w2048, state_hgrn, p_prompt, p_sample,
           g_mix, w_in, g_q, g_k, hg_lb_raw, g_hg_out, w_up_attn, w_up_hgrn, w_out, g_ffn, w_ff_up, w_ff_down,
           w_ple, w_ple_gate):
    bm, bn = 2816, 256
    bm_ple = 1408
    bm_res = 2816
    xp, xs = x_prompt.reshape(SEQ, D_MODEL), x_sample.reshape(N_SAMPLE, D_MODEL)
    pp, ps = p_prompt.reshape(SEQ, PLE_DIM), p_sample.reshape(N_SAMPLE, PLE_DIM)
    lb = jnp.cumsum(jax.nn.softmax(hg_lb_raw.astype(F32), axis=0), axis=0)[0]
    rows = lambda width, dt: jax.ShapeDtypeStruct((M_ALL, width), dt)

    def plain(accs, i, j, rows_sl, a_refs, extra, outs):
        outs[0][rows_sl, :] = accs[0]

    n_mix = _rmsnorm_rows(xp, g_mix[0], x_tail=xs, name="norm_mix")
    z, = _matmul_rows([(n_mix, w_in[0])], bm=bm, bn=512, out_shapes=(rows(IN_WIDTH, F32),), epilogue=plain,
                      name="in_proj")

    o_attn, kn0, kn1, kn2 = _attn_prompt(z, g_q[0], g_k[0], seq=SEQ, sb=2048, out_rows=M_ALL,
                                         dils=tuple(d for _, d in DIL_GROUPS))
    zs = z[SEQ:]
    def by_group(col0):
        a = zs[:, col0:col0 + N_GROUPS * A_WIDTH].reshape(DEC_BATCH, DEC_SEQ, N_GROUPS, A_HEADS, HEAD_DIM)
        return a.transpose(0, 2, 1, 3, 4).reshape(DEC_BATCH, N_GROUPS, DEC_SEQ * A_HEADS, HEAD_DIM)

    caches = [c[0] for c in (cache_kv_w128, cache_kv_w512, cache_kv_w2048)]
    o_attn_s, kn_s = _attn_sample(by_group(COL_Q), by_group(COL_K), by_group(COL_V), caches,
                                  g_q[0], g_k[0], n_batch=DEC_BATCH, t_new=DEC_SEQ, groups=DIL_GROUPS)
    o_attn = lax.dynamic_update_slice(o_attn, o_attn_s.reshape(N_SAMPLE, A_WIDTH).astype(BF16), (SEQ, 0))

    o_hg, st_p = _hgrn_prompt(z, lb, g_hg_out[0], seq=SEQ, tb=256, hb=8, out_rows=M_ALL)
    o_hg_s, st_s = _hgrn_sample(zs[:, COL_HQ:COL_HF], zs[:, COL_HF:COL_HI], zs[:, COL_HI:COL_HOG],
                                zs[:, COL_HOG:COL_GA], lb, g_hg_out[0], state_hgrn[0],
                                n_batch=DEC_BATCH, t_new=DEC_SEQ)
    o_hg = lax.dynamic_update_slice(o_hg, o_hg_s.astype(BF16), (SEQ, 0))

    def merge_epilogue(accs, i, j, rows_sl, a_refs, extra, outs):
        ga, gb = extra[0][rows_sl, :], extra[1][rows_sl, :]
        outs[0][rows_sl, :] = (_sigmoid(ga) * accs[0] + _sigmoid(gb) * accs[1]).astype(BF16)

    def out_proj_epilogue(accs, i, j, rows_sl, a_refs, extra, outs):
        xp_ref, xs_ref = extra
        outs[0][rows_sl, :] = xp_ref[rows_sl, :] + accs[0]
        if rows_sl.stop == bm_res:
            t0 = bm_res - N_SAMPLE

            @pl.when(i == pl.num_programs(0) - 1)
            def _():
                outs[0][t0:, :] = xs_ref[...] + accs[0][t0 - rows_sl.start:, :]

    merged, = _matmul_rows([(o_attn, w_up_attn[0]), (o_hg, w_up_hgrn[0])], bm=bm, bn=bn, extras=(z, z),
                           extra_specs=(_tile_spec(bm, bn, COL_GA), _tile_spec(bm, bn, COL_GB)),
                           out_shapes=(rows(D_MODEL, BF16),), epilogue=merge_epilogue, name="up_merge")
    x1, = _matmul_rows([(merged, w_out[0])], bm=bm_res, bn=bn, extras=(xp, xs),
                       extra_specs=(_tile_spec(bm_res, bn), pl.BlockSpec((N_SAMPLE, bn), lambda i, j: (0, j))),
                       out_shapes=(rows(D_MODEL, F32),), epilogue=out_proj_epilogue, name="out_proj")

    def ffn_up_epilogue(accs, i, j, rows_sl, a_refs, extra, outs):
        outs[0][rows_sl, :] = jnp.square(jnp.maximum(accs[0], 0.0)).astype(BF16)

    h_ffn = _rmsnorm_rows(x1, g_ffn[0], name="norm_ffn")
    hid, = _matmul_rows([(h_ffn, w_ff_up[0])], bm=bm, bn=512, out_shapes=(rows(FFN_HIDDEN, BF16),),
                        epilogue=ffn_up_epilogue, name="ffn_up")
    x2 = _matmul_ksplit(hid, w_ff_down[0], x1, bm=1408, bn=2048, bk=1024, name="ffn_down")

    n_ple_blocks = M_ALL // bm_ple

    def ple_epilogue(accs, i, j, rows_sl, a_refs, extra, outs):
        pp_ref, ps_ref, wp_ref = extra
        yp_ref, ys_ref = outs
        x2_tile = a_refs[0][rows_sl, pl.ds(pl.multiple_of(j * bn, bn), bn)]
        gate = _sigmoid(accs[0])
        wp = wp_ref[...].astype(BF16)
        yp_ref[rows_sl, :] = x2_tile + gate * _dot(pp_ref[rows_sl, :].astype(BF16), wp)
        if rows_sl.stop == bm_ple:
            s0 = bm_ple - N_SAMPLE - rows_sl.start

            @pl.when(i == n_ple_blocks - 1)
            def _():
                ys_ref[...] = x2_tile[s0:, :] + gate[s0:, :] * _dot(ps_ref[...].astype(BF16), wp)

    y_p, y_s = _matmul_rows(
        [(x2, w_ple_gate[0])], bm=bm_ple, bn=bn, extras=(pp, ps, w_ple[0]),
        extra_specs=(pl.BlockSpec((bm_ple, PLE_DIM), lambda i, j: (i, 0)),
                     pl.BlockSpec((N_SAMPLE, PLE_DIM), lambda i, j: (0, 0)),
                     pl.BlockSpec((PLE_DIM, bn), lambda i, j: (0, j))),
        out_shapes=(jax.ShapeDtypeStruct((SEQ, D_MODEL), F32), jax.ShapeDtypeStruct((N_SAMPLE, D_MODEL), F32)),
        out_specs=(pl.BlockSpec((bm_ple, bn), lambda i, j: (i, j)),
                   pl.BlockSpec((N_SAMPLE, bn), lambda i, j: (0, jnp.where(i == n_ple_blocks - 1, j, 0)))),
        epilogue=ple_epilogue, name="ple")

    y_prompt = y_p.reshape(1, SEQ, D_MODEL)
    y_sample = y_s.reshape(DEC_BATCH, DEC_SEQ, D_MODEL)
    kv_p, kv_s = [], []
    for g, (window, _) in enumerate(DIL_GROUPS):
        length = min(window, SEQ)
        kn_g = (kn0, kn1, kn2)[g][SEQ - length:].reshape(length, A_HEADS, HEAD_DIM)
        v_g = z[SEQ - length:SEQ, COL_V + g * A_WIDTH:COL_V + (g + 1) * A_WIDTH].reshape(length, A_HEADS, HEAD_DIM)
        kv_p.append(jnp.stack([kn_g, v_g], axis=1)[None, None])
        ks_g = kn_s[:, g].reshape(DEC_BATCH, DEC_SEQ, A_HEADS, HEAD_DIM)
        vs_g = zs[:, COL_V + g * A_WIDTH:COL_V + (g + 1) * A_WIDTH].reshape(DEC_BATCH, DEC_SEQ, A_HEADS, HEAD_DIM)
        kv_s.append(jnp.stack([ks_g, vs_g], axis=2)[None])
    return (y_prompt, y_sample, kv_p[0], kv_p[1], kv_p[2], st_p[None, None],
            kv_s[0], kv_s[1], kv_s[2], st_s[None])
```

```python
import functools

import numpy as np
import jax
import jax.numpy as jnp
from jax import lax
from jax.experimental import pallas as pl
from jax.experimental.pallas import tpu as pltpu

F32 = jnp.float32
BF16 = jnp.bfloat16

D_MODEL = 4096
SEQ = 8192
DEC_BATCH = 32
DEC_SEQ = 8
N_SAMPLE = DEC_BATCH * DEC_SEQ
M_ALL = SEQ + N_SAMPLE
DIL_GROUPS = ((128, 1), (512, 4), (2048, 16))
N_GROUPS = 3
A_HEADS = 8
HEAD_DIM = 128
A_WIDTH = A_HEADS * HEAD_DIM
BAND = 128
HG_HEADS = 16
HG_KDIM = 128
HG_VDIM = 128
HG_WIDTH = HG_HEADS * HG_VDIM
HG_CHUNK = 32
FFN_HIDDEN = 4 * D_MODEL
PLE_DIM = 256
NORM_EPS = 1e-6
ATTN_SCALE = HEAD_DIM ** -0.5

COL_Q = 0
COL_K = COL_Q + N_GROUPS * A_WIDTH
COL_V = COL_K + N_GROUPS * A_WIDTH
COL_HQ = COL_V + N_GROUPS * A_WIDTH
COL_HF = COL_HQ + HG_HEADS * HG_KDIM
COL_HI = COL_HF + HG_HEADS * HG_KDIM
COL_HOG = COL_HI + HG_WIDTH
COL_GA = COL_HOG + HG_WIDTH
COL_GB = COL_GA + D_MODEL
IN_WIDTH = COL_GB + D_MODEL

VMEM_LIMIT = 56 * 1024 * 1024
LANES = 128


def _dot(a, b):
    return jnp.dot(a, b, preferred_element_type=F32)


def _dot_nt(a, b):
    return lax.dot_general(a, b, (((1,), (1,)), ((), ())), preferred_element_type=F32)


def _dot_tn(a, b):
    return lax.dot_general(a, b, (((0,), (0,)), ((), ())), preferred_element_type=F32)


def _rms(x, g):
    return x * lax.rsqrt(jnp.mean(x * x, axis=-1, keepdims=True) + NORM_EPS) * g


def _sigmoid(x):
    return 1.0 / (1.0 + jnp.exp(-x))


def _store(outs, *vals):
    for o_ref, v in zip(outs, vals):
        o_ref[...] = v.astype(o_ref.dtype)


_CAST_ROWS = 128
_DOT_ROWS = 704


def _mm_rows_body(*refs, needs_cast, n_extra, n_out, epilogue, row_chunk, first_col_init):
    n_pairs = len(needs_cast)
    a_refs = refs[0:2 * n_pairs:2]
    w_refs = refs[1:2 * n_pairs:2]
    pos = 2 * n_pairs
    extra = refs[pos:pos + n_extra]
    outs = refs[pos + n_extra:pos + n_extra + n_out]
    scratch = list(refs[pos + n_extra + n_out:])
    i, j = pl.program_id(0), pl.program_id(1)
    lhs_refs = []
    for a_ref, cast in zip(a_refs, needs_cast):
        if cast:
            ab_ref = scratch.pop(0)

            @pl.when(j == 0)
            def _(a_ref=a_ref, ab_ref=ab_ref):
                def rows(c, carry):
                    r0 = pl.multiple_of(c * _CAST_ROWS, _CAST_ROWS)
                    ab_ref[pl.ds(r0, _CAST_ROWS), :] = a_ref[pl.ds(r0, _CAST_ROWS), :].astype(BF16)
                    return carry
                lax.fori_loop(0, a_ref.shape[0] // _CAST_ROWS, rows, 0)

            lhs_refs.append(ab_ref)
        else:
            lhs_refs.append(a_ref)
    if first_col_init is not None:
        @pl.when(j == 0)
        def _():
            first_col_init(outs)

    wbs = [w_ref[...].astype(BF16) for w_ref in w_refs]
    for r0 in range(0, lhs_refs[0].shape[0], row_chunk):
        rows = slice(r0, r0 + row_chunk)
        accs = [_dot(lhs_ref[rows, :], wb) for lhs_ref, wb in zip(lhs_refs, wbs)]
        epilogue(accs, i, j, rows, a_refs, extra, outs)


def _matmul_rows(pairs, *, bm, bn, extras=(), extra_specs=(), out_shapes, out_specs=None, epilogue, name,
                 row_chunk=_DOT_ROWS, first_col_init=None):
    m = pairs[0][0].shape[0]
    n = pairs[0][1].shape[1]
    assert m % bm == 0 and n % bn == 0 and bm % _CAST_ROWS == 0 and bm % row_chunk == 0
    in_specs, operands, scratch, needs_cast = [], [], [], []
    for a, w in pairs:
        kdim = w.shape[0]
        assert a.shape == (m, kdim) and w.shape[1] == n
        in_specs += [pl.BlockSpec((bm, kdim), lambda i, j: (i, 0), pipeline_mode=pl.Buffered(1)),
                     pl.BlockSpec((kdim, bn), lambda i, j: (0, j))]
        operands += [a, w]
        needs_cast.append(a.dtype != BF16)
        if needs_cast[-1]:
            scratch.append(pltpu.VMEM((bm, kdim), BF16))
    if out_specs is None:
        out_specs = tuple(pl.BlockSpec((bm, bn), lambda i, j: (i, j)) for _ in out_shapes)
    body = functools.partial(_mm_rows_body, needs_cast=tuple(needs_cast), n_extra=len(extras),
                             n_out=len(out_shapes), epilogue=epilogue, row_chunk=row_chunk,
                             first_col_init=first_col_init)
    return pl.pallas_call(
        body,
        grid=(m // bm, n // bn),
        in_specs=in_specs + list(extra_specs),
        out_specs=tuple(out_specs),
        out_shape=tuple(out_shapes),
        scratch_shapes=scratch,
        compiler_params=pltpu.CompilerParams(dimension_semantics=("arbitrary", "arbitrary"),
                                             vmem_limit_bytes=VMEM_LIMIT),
        name=name,
    )(*operands, *extras)


def _mm_ksplit_body(a_ref, w_ref, r_ref, o_ref, *, row_chunk):
    @pl.when(pl.program_id(2) == 0)
    def _():
        o_ref[...] = r_ref[...]

    wb = w_ref[...].astype(BF16)
    for r0 in range(0, o_ref.shape[0], row_chunk):
        rows = slice(r0, r0 + row_chunk)
        o_ref[rows, :] += _dot(a_ref[rows, :], wb)


def _matmul_ksplit(a, w, resid, *, bm, bn, bk, name, row_chunk=_DOT_ROWS):
    m, kdim = a.shape
    n = w.shape[1]
    assert m % bm == 0 and n % bn == 0 and kdim % bk == 0 and bm % row_chunk == 0 and a.dtype == BF16
    tile = lambda: pl.BlockSpec((bm, bn), lambda j, i, k: (i, j), pipeline_mode=pl.Buffered(1))
    return pl.pallas_call(
        functools.partial(_mm_ksplit_body, row_chunk=row_chunk),
        grid=(n // bn, m // bm, kdim // bk),
        in_specs=[pl.BlockSpec((bm, bk), lambda j, i, k: (i, k)),
                  pl.BlockSpec((bk, bn), lambda j, i, k: (k, j)),
                  tile()],
        out_specs=tile(),
        out_shape=jax.ShapeDtypeStruct((m, n), F32),
        compiler_params=pltpu.CompilerParams(dimension_semantics=("arbitrary", "arbitrary", "arbitrary"),
                                             vmem_limit_bytes=VMEM_LIMIT),
        name=name,
    )(a, w, resid)


def _rmsnorm_body(*refs, n_main):
    g_ref, o_ref = refs[-2:]
    if n_main is None:
        o_ref[...] = _rms(refs[0][...], g_ref[...]).astype(o_ref.dtype)
        return
    i = pl.program_id(0)

    @pl.when(i < n_main)
    def _():
        o_ref[...] = _rms(refs[0][...], g_ref[...]).astype(o_ref.dtype)

    @pl.when(i >= n_main)
    def _():
        o_ref[...] = _rms(refs[1][...], g_ref[...]).astype(o_ref.dtype)


def _rmsnorm_rows(x, g, *, x_tail=None, bm=256, name="rmsnorm"):
    m, d = x.shape
    assert m % bm == 0
    n_main = m // bm
    in_specs = [pl.BlockSpec((bm, d), lambda i: (jnp.minimum(i, n_main - 1), 0))]
    operands = [x]
    if x_tail is not None:
        assert x_tail.shape[0] % bm == 0
        in_specs.append(pl.BlockSpec((bm, d), lambda i: (jnp.maximum(i - n_main, 0), 0)))
        operands.append(x_tail)
        m += x_tail.shape[0]
    in_specs.append(pl.BlockSpec((1, d), lambda i: (0, 0)))
    return pl.pallas_call(
        functools.partial(_rmsnorm_body, n_main=None if x_tail is None else n_main),
        grid=(m // bm,),
        in_specs=in_specs,
        out_specs=pl.BlockSpec((bm, d), lambda i: (i, 0)),
        out_shape=jax.ShapeDtypeStruct((m, d), BF16),
        compiler_params=pltpu.CompilerParams(dimension_semantics=("arbitrary",)),
        name=name,
    )(*operands, g.reshape(1, d))


def _ld_rows(ref, start, dil, n=BAND):
    if dil == 1:
        return ref[pl.ds(start, n), :]
    return ref[pl.ds(start, n, stride=dil), :]


def _st_rows(ref, g, start, dil, val):
    if dil == 1:
        ref[g, pl.ds(start, BAND), :] = val
    else:
        ref[g, pl.ds(start, BAND, stride=dil), :] = val


def _attn_prompt_body(slope_ref, gq_ref, gk_ref,
                      q0, k0, v0, kp0, vp0, q1, k1, v1, kp1, vp1, q2, k2, v2, kp2, vp2,
                      o_ref, kn0, kn1, kn2,
                      qn_s, kn_s, vv_s, og_s, lse_s, *, sb, dils):
    first_block = pl.program_id(1) == 0
    groups = ((q0, k0, v0, kp0, vp0, kn0), (q1, k1, v1, kp1, vp1, kn1), (q2, k2, v2, kp2, vp2, kn2))
    ii = lax.broadcasted_iota(jnp.int32, (BAND, BAND), 0)
    jj = lax.broadcasted_iota(jnp.int32, (BAND, BAND), 1)
    dist_cur = (ii - jj).astype(F32)
    dist_prev = (BAND + ii - jj).astype(F32)
    ok_cur = jj <= ii
    ok_prev = jj >= ii
    neg_inf = jnp.float32(-jnp.inf)
    base = kn_s.shape[0] - sb

    for g, dil in enumerate(dils):
        q_ref, k_ref, v_ref, kp_ref, vp_ref, kn_out = groups[g]
        pr = BAND * dil
        gq = gq_ref[g:g + 1, :]
        gk = gk_ref[g:g + 1, :]
        slope = slope_ref[0, :, g * HEAD_DIM:(g + 1) * HEAD_DIM]
        qn_s[...] = _rms(q_ref[...], gq)
        kn = _rms(k_ref[...], gk)
        kn_out[...] = kn
        kn_s[pl.ds(base, sb), :] = kn
        kn_s[pl.ds(base - pr, pr), :] = _rms(kp_ref[...], gk)
        vv_s[pl.ds(base, sb), :] = v_ref[...]
        vv_s[pl.ds(base - pr, pr), :] = vp_ref[...]
        bias_cur = jnp.where(ok_cur, -slope * (dist_cur * dil), neg_inf)
        bias_prev = jnp.where(ok_prev, -slope * (dist_prev * dil), neg_inf)
        bias = jnp.concatenate([bias_prev, bias_cur], axis=1)
        bias_first = jnp.concatenate([jnp.where(first_block, neg_inf, bias_prev), bias_cur], axis=1)
        ones = jnp.ones((2 * BAND, HEAD_DIM), BF16)
        for r in range(dil):
            for s in range(sb // pr):
                c0 = s * pr + r
                qs = _ld_rows(qn_s, c0, dil).astype(BF16)
                kk = _ld_rows(kn_s, base + c0 - pr, dil, 2 * BAND).astype(BF16)
                vv = _ld_rows(vv_s, base + c0 - pr, dil, 2 * BAND).astype(BF16)
                lg = _dot_nt(qs, kk) * ATTN_SCALE + (bias_first if s == 0 else bias)
                mx = jnp.max(lg, axis=-1, keepdims=True)
                p = jnp.exp(lg - mx).astype(BF16)
                oe = _dot(p, jnp.concatenate([vv, ones], axis=1))
                den = oe[:, HEAD_DIM:]
                _st_rows(og_s, g, c0, dil, oe[:, :HEAD_DIM] / den)
                _st_rows(lse_s, g, c0, dil, mx + jnp.log(den))

    l0, l1, l2 = lse_s[0], lse_s[1], lse_s[2]
    mx = jnp.maximum(jnp.maximum(l0, l1), l2)
    w0, w1, w2 = jnp.exp(l0 - mx), jnp.exp(l1 - mx), jnp.exp(l2 - mx)
    o = (w0 * og_s[0] + w1 * og_s[1] + w2 * og_s[2]) / (w0 + w1 + w2)
    o_ref[...] = o.astype(o_ref.dtype)


def _alibi_slopes():
    n = N_GROUPS * A_HEADS
    e = jnp.arange(1, n + 1, dtype=F32)
    return jnp.exp2(-8.0 * e / n).reshape(N_GROUPS, A_HEADS)


def _attn_prompt(z, g_q, g_k, *, seq, sb, dils, out_rows=None, n_heads=A_HEADS, col_q=COL_Q, col_k=COL_K,
                 col_v=COL_V):
    out_rows = seq if out_rows is None else out_rows
    nb = seq // sb
    slopes = _alibi_slopes()
    slope_arr = jnp.broadcast_to(slopes.T[:, None, :, None], (n_heads, 1, N_GROUPS, HEAD_DIM))
    slope_arr = slope_arr.reshape(n_heads, 1, N_GROUPS * HEAD_DIM)
    a_width = n_heads * HEAD_DIM
    in_specs = [pl.BlockSpec((1, 1, N_GROUPS * HEAD_DIM), lambda h, i: (h, 0, 0)),
                pl.BlockSpec((N_GROUPS, HEAD_DIM), lambda h, i: (0, 0)),
                pl.BlockSpec((N_GROUPS, HEAD_DIM), lambda h, i: (0, 0))]
    operands = [slope_arr, g_q, g_k]
    max_pr = BAND * max(dils)
    for g, dil in enumerate(dils):
        pr = BAND * dil
        ratio = sb // pr
        cq = (col_q + g * a_width) // HEAD_DIM
        ck = (col_k + g * a_width) // HEAD_DIM
        cv = (col_v + g * a_width) // HEAD_DIM
        cur = lambda c: pl.BlockSpec((sb, HEAD_DIM), lambda h, i, c=c: (i, c + h))
        prev = lambda c: pl.BlockSpec((pr, HEAD_DIM),
                                      lambda h, i, c=c, ratio=ratio: (jnp.maximum(i * ratio - 1, 0), c + h))
        in_specs += [cur(cq), cur(ck), cur(cv), prev(ck), prev(cv)]
        operands += [z, z, z, z, z]
    out_block = pl.BlockSpec((sb, HEAD_DIM), lambda h, i: (i, h))
    out_shape = (jax.ShapeDtypeStruct((out_rows, a_width), BF16),) + tuple(
        jax.ShapeDtypeStruct((seq, a_width), F32) for _ in dils)
    body = functools.partial(_attn_prompt_body, sb=sb, dils=tuple(dils))
    return pl.pallas_call(
        body,
        grid=(n_heads, nb),
        in_specs=in_specs,
        out_specs=(out_block,) * 4,
        out_shape=out_shape,
        scratch_shapes=[pltpu.VMEM((sb, HEAD_DIM), F32),
                        pltpu.VMEM((sb + max_pr, HEAD_DIM), F32),
                        pltpu.VMEM((sb + max_pr, HEAD_DIM), F32),
                        pltpu.VMEM((N_GROUPS, sb, HEAD_DIM), F32),
                        pltpu.VMEM((N_GROUPS, sb, HEAD_DIM), F32)],
        compiler_params=pltpu.CompilerParams(dimension_semantics=("arbitrary", "arbitrary"),
                                             vmem_limit_bytes=VMEM_LIMIT),
        name="attn_prompt",
    )(*operands)


def _sample_problems(dil, t_new):
    n_prob = min(dil, t_new)
    return n_prob, t_new // n_prob


def _sample_bias_tables(groups, t_new, n_heads):
    n_soft = len(groups) * n_heads
    tabs_c, tabs_n = [], []
    for g, (window, dil) in enumerate(groups):
        n_prob, tok_per = _sample_problems(dil, t_new)
        slopes = 2.0 ** (-8.0 * (g * n_heads + np.arange(n_heads) + 1) / n_soft)
        rows = np.arange(tok_per * n_heads)
        i, hp = rows // n_heads, rows % n_heads
        cols = np.arange(BAND * n_heads)
        m, h = cols // n_heads, cols % n_heads
        dist = window + i[:, None] * n_prob - m[None, :] * dil
        ok = (h[None, :] == hp[:, None]) & (dist <= window) & (dist > 0) & (dist % dil == 0)
        tabs_c.append(np.where(ok, -slopes[hp][:, None] * dist, -np.inf).astype(np.float32))
        cols = np.arange(t_new * n_heads)
        s, h = cols // n_heads, cols % n_heads
        per_p = []
        for p in range(n_prob):
            dist = (p + i * n_prob)[:, None] - s[None, :]
            ok = (h[None, :] == hp[:, None]) & (dist >= 0) & (dist % dil == 0)
            per_p.append(np.where(ok, -slopes[hp][:, None] * dist, -np.inf).astype(np.float32))
        tabs_n.append(np.stack(per_p))
    return tabs_c, tabs_n


def _attn_sample_body(gq_ref, gk_ref, q_ref, k_ref, v_ref, c0_ref, c1_ref, c2_ref,
                      bc0, bc1, bc2, bn0, bn1, bn2, o_ref, kn_ref, og_s, lse_s, *, t_new, groups, n_heads):
    caches = (c0_ref, c1_ref, c2_ref)
    bias_c = (bc0, bc1, bc2)
    bias_n = (bn0, bn1, bn2)
    kv_rows = 2 * n_heads
    for g, (window, dil) in enumerate(groups):
        c_ref = caches[g]
        n_prob, tok_per = _sample_problems(dil, t_new)
        qn = _rms(q_ref[0, g], gq_ref[g:g + 1, :])
        kn = _rms(k_ref[0, g], gk_ref[g:g + 1, :])
        kn_ref[0, g] = kn
        knb = kn.astype(BF16)
        vnb = v_ref[0, g].astype(BF16)
        for p in range(n_prob):
            toks = [p + i * n_prob for i in range(tok_per)]
            parts = [qn[t * n_heads:(t + 1) * n_heads] for t in toks]
            qp = (parts[0] if tok_per == 1 else jnp.concatenate(parts, axis=0)).astype(BF16)
            kc = c_ref[:, p * kv_rows:p * kv_rows + n_heads, :].reshape(BAND * n_heads, HEAD_DIM).astype(BF16)
            vc = c_ref[:, p * kv_rows + n_heads:(p + 1) * kv_rows, :].reshape(BAND * n_heads, HEAD_DIM).astype(BF16)
            lc = _dot_nt(qp, kc) * ATTN_SCALE + bias_c[g][...]
            ln = _dot_nt(qp, knb) * ATTN_SCALE + bias_n[g][p]
            mx = jnp.maximum(jnp.max(lc, axis=-1, keepdims=True), jnp.max(ln, axis=-1, keepdims=True))
            pc = jnp.exp(lc - mx)
            pn = jnp.exp(ln - mx)
            ssum = jnp.sum(pc, axis=-1, keepdims=True) + jnp.sum(pn, axis=-1, keepdims=True)
            o = (_dot(pc.astype(BF16), vc) + _dot(pn.astype(BF16), vnb)) / ssum
            lse = jnp.broadcast_to(mx + jnp.log(ssum), o.shape)
            for i, t in enumerate(toks):
                og_s[g, t * n_heads:(t + 1) * n_heads, :] = o[i * n_heads:(i + 1) * n_heads]
                lse_s[g, t * n_heads:(t + 1) * n_heads, :] = lse[i * n_heads:(i + 1) * n_heads]
    l0, l1, l2 = lse_s[0], lse_s[1], lse_s[2]
    mx = jnp.maximum(jnp.maximum(l0, l1), l2)
    w0, w1, w2 = jnp.exp(l0 - mx), jnp.exp(l1 - mx), jnp.exp(l2 - mx)
    o_ref[0] = (w0 * og_s[0] + w1 * og_s[1] + w2 * og_s[2]) / (w0 + w1 + w2)


def _attn_sample(qs, ks, vs, caches, g_q, g_k, *, n_batch, t_new, groups, n_heads=A_HEADS):
    n_g = len(groups)
    rows = t_new * n_heads
    tabs_c, tabs_n = _sample_bias_tables(groups, t_new, n_heads)
    new_spec = lambda: pl.BlockSpec((1, n_g, rows, HEAD_DIM), lambda b: (b, 0, 0, 0))
    in_specs = [pl.BlockSpec((n_g, HEAD_DIM), lambda b: (0, 0)),
                pl.BlockSpec((n_g, HEAD_DIM), lambda b: (0, 0)),
                new_spec(), new_spec(), new_spec()]
    operands = [g_q, g_k, qs, ks, vs]
    for c, (window, dil) in zip(caches, groups):
        assert c.shape[1] == window and window == BAND * dil
        n_prob, _ = _sample_problems(dil, t_new)
        c3 = c.reshape(n_batch * BAND, dil * 2 * n_heads, HEAD_DIM)
        in_specs.append(pl.BlockSpec((BAND, n_prob * 2 * n_heads, HEAD_DIM), lambda b: (b, 0, 0)))
        operands.append(c3)
    for tab in tabs_c:
        in_specs.append(pl.BlockSpec(tab.shape, lambda b: (0, 0)))
        operands.append(jnp.asarray(tab))
    for tab in tabs_n:
        in_specs.append(pl.BlockSpec(tab.shape, lambda b: (0, 0, 0)))
        operands.append(jnp.asarray(tab))
    body = functools.partial(_attn_sample_body, t_new=t_new, groups=tuple(groups), n_heads=n_heads)
    return pl.pallas_call(
        body,
        grid=(n_batch,),
        in_specs=in_specs,
        out_specs=(pl.BlockSpec((1, rows, HEAD_DIM), lambda b: (b, 0, 0)), new_spec()),
        out_shape=(jax.ShapeDtypeStruct((n_batch, rows, HEAD_DIM), F32),
                   jax.ShapeDtypeStruct((n_batch, n_g, rows, HEAD_DIM), F32)),
        scratch_shapes=[pltpu.VMEM((n_g, rows, HEAD_DIM), F32), pltpu.VMEM((n_g, rows, HEAD_DIM), F32)],
        compiler_params=pltpu.CompilerParams(dimension_semantics=("arbitrary",),
                                             vmem_limit_bytes=VMEM_LIMIT),
        name="attn_sample",
    )(*operands)


def _split3(x):
    hi = x.astype(BF16)
    r1 = x - hi.astype(F32)
    mid = r1.astype(BF16)
    lo = (r1 - mid.astype(F32)).astype(BF16)
    return hi, mid, lo


def _hgrn_heads(hq, hf, hi, hog, lb, g_out, sts, *, tb, n_valid=None):
    c = HG_CHUNK
    nh = len(sts)
    sts = list(sts)
    q = hq * _sigmoid(hq) * (HG_KDIM ** -0.5)
    gate = lb + (1.0 - lb) * _sigmoid(hf)
    log_g = jnp.log(gate)
    k = (1.0 - lb) * _sigmoid(-hf)
    if n_valid is not None:
        rows = lax.broadcasted_iota(jnp.int32, hq.shape, 0)
        log_g = jnp.where(rows < n_valid, log_g, 0.0)
        k = jnp.where(rows < n_valid, k, 0.0)
        q = jnp.where(rows < n_valid, q, 0.0)
    row = lax.broadcasted_iota(jnp.int32, (tb, tb), 0)
    col = lax.broadcasted_iota(jnp.int32, (tb, tb), 1)
    same = (row // c) == (col // c)
    tri_ok = same & (col <= row)
    tri = jnp.where(tri_ok, 1.0, 0.0).astype(BF16)
    blk = jnp.where(same, 1.0, 0.0).astype(BF16)
    p_hi, p_mid, p_lo = _split3(log_g)
    b = _dot(tri, p_hi) + _dot(tri, p_mid) + _dot(tri, p_lo)
    b_last = _dot(blk, p_hi) + _dot(blk, p_mid) + _dot(blk, p_lo)
    q_dec = (q * jnp.exp(b)).astype(BF16)
    k_dec = (k * jnp.exp(-b)).astype(BF16)
    k_end = (k * jnp.exp(b_last - b)).astype(BF16)
    decay = jnp.exp(b_last)
    vb = hi.astype(BF16)
    head = lambda x, h: x[:, h * HG_KDIM:(h + 1) * HG_KDIM]
    o_intra = []
    for h in range(nh):
        a = jnp.where(tri_ok, _dot_nt(head(q_dec, h), head(k_dec, h)), 0.0)
        o_intra.append(_dot(a.astype(BF16), head(vb, h)))
    parts = [[] for _ in range(nh)]
    for ci in range(tb // c):
        sl = slice(ci * c, (ci + 1) * c)
        for h in range(nh):
            parts[h].append(_dot_nt(head(q_dec, h)[sl], sts[h].astype(BF16)))
            sts[h] = sts[h] * head(decay, h)[ci * c:ci * c + 1, :] + _dot_tn(head(vb, h)[sl], head(k_end, h)[sl])
    outs = []
    for h in range(nh):
        o = o_intra[h] + (jnp.concatenate(parts[h], axis=0) if len(parts[h]) > 1 else parts[h][0])
        outs.append(_rms(o, g_out) * _sigmoid(head(hog, h)))
    return outs, sts


def _hgrn_prompt_body(hq_ref, hf_ref, hi_ref, hog_ref, lb_ref, go_ref, o_ref, s_ref, st_s, *, tb, nt, hb):
    t = pl.program_id(1)

    @pl.when(t == 0)
    def _():
        st_s[...] = jnp.zeros_like(st_s)

    outs, sts = _hgrn_heads(hq_ref[...], hf_ref[...], hi_ref[...], hog_ref[...], lb_ref[0], go_ref[...],
                            [st_s[hh] for hh in range(hb)], tb=tb)
    for hh in range(hb):
        o_ref[:, hh * HG_VDIM:(hh + 1) * HG_VDIM] = outs[hh].astype(o_ref.dtype)
        st_s[hh] = sts[hh]

    @pl.when(t == nt - 1)
    def _():
        for hh in range(hb):
            s_ref[hh] = sts[hh].T


def _hgrn_prompt(z, lb, g_out, *, seq, tb, hb, out_rows=None, n_heads=HG_HEADS, col_hq=COL_HQ, col_hf=COL_HF,
                 col_hi=COL_HI, col_hog=COL_HOG):
    out_rows = seq if out_rows is None else out_rows
    nt = seq // tb
    bw = hb * HG_KDIM
    assert all(c0 % bw == 0 for c0 in (col_hq, col_hf, col_hi, col_hog))
    blk = lambda c0: pl.BlockSpec((tb, bw), lambda h, t, c=c0 // bw: (t, c + h))
    body = functools.partial(_hgrn_prompt_body, tb=tb, nt=nt, hb=hb)
    return pl.pallas_call(
        body,
        grid=(n_heads // hb, nt),
        in_specs=[blk(col_hq), blk(col_hf), blk(col_hi), blk(col_hog),
                  pl.BlockSpec((1, 1, bw), lambda h, t: (h, 0, 0)),
                  pl.BlockSpec((1, HG_VDIM), lambda h, t: (0, 0))],
        out_specs=(pl.BlockSpec((tb, bw), lambda h, t: (t, h)),
                   pl.BlockSpec((hb, HG_KDIM, HG_VDIM), lambda h, t: (h, 0, 0))),
        out_shape=(jax.ShapeDtypeStruct((out_rows, n_heads * HG_VDIM), BF16),
                   jax.ShapeDtypeStruct((n_heads, HG_KDIM, HG_VDIM), F32)),
        scratch_shapes=[pltpu.VMEM((hb, HG_VDIM, HG_KDIM), F32)],
        compiler_params=pltpu.CompilerParams(dimension_semantics=("arbitrary", "arbitrary"),
                                             vmem_limit_bytes=VMEM_LIMIT),
        name="hgrn_prompt",
    )(z, z, z, z, lb.reshape(n_heads // hb, 1, bw), g_out.reshape(1, HG_VDIM))


def _hgrn_sample_body(hq_ref, hf_ref, hi_ref, hog_ref, lb_ref, go_ref, s0_ref, o_ref, s_ref, *, t_new, n_heads):
    pad = jnp.zeros((HG_CHUNK - t_new, n_heads * HG_KDIM), F32)
    ext = lambda ref: jnp.concatenate([ref[...], pad], axis=0)
    outs, sts = _hgrn_heads(ext(hq_ref), ext(hf_ref), ext(hi_ref), ext(hog_ref), lb_ref[...], go_ref[...],
                            [s0_ref[0, h].T for h in range(n_heads)], tb=HG_CHUNK, n_valid=t_new)
    for h in range(n_heads):
        o_ref[:, h * HG_VDIM:(h + 1) * HG_VDIM] = outs[h][:t_new]
        s_ref[0, h] = sts[h].T


def _hgrn_sample(hq, hf, hi, hog, lb, g_out, s0, *, n_batch, t_new, n_heads=HG_HEADS):
    width = n_heads * HG_KDIM
    row = lambda: pl.BlockSpec((t_new, width), lambda b: (b, 0))
    st = lambda: pl.BlockSpec((1, n_heads, HG_KDIM, HG_VDIM), lambda b: (b, 0, 0, 0))
    body = functools.partial(_hgrn_sample_body, t_new=t_new, n_heads=n_heads)
    return pl.pallas_call(
        body,
        grid=(n_batch,),
        in_specs=[row(), row(), row(), row(),
                  pl.BlockSpec((1, width), lambda b: (0, 0)),
                  pl.BlockSpec((1, HG_VDIM), lambda b: (0, 0)),
                  st()],
        out_specs=(row(), st()),
        out_shape=(jax.ShapeDtypeStruct((n_batch * t_new, width), F32),
                   jax.ShapeDtypeStruct(s0.shape, F32)),
        compiler_params=pltpu.CompilerParams(dimension_semantics=("arbitrary",),
                                             vmem_limit_bytes=VMEM_LIMIT),
        name="hgrn_sample",
    )(hq, hf, hi, hog, lb.reshape(1, width), g_out.reshape(1, HG_VDIM), s0)


def _tile_spec(bm, bn, col0=0):
    assert col0 % bn == 0
    cb = col0 // bn
    return pl.BlockSpec((bm, bn), lambda i, j: (i, cb + j))


def kernel(x_prompt, x_sample, cache_kv_w128, cache_kv_w512, cache_kv_w2048, state_hgrn, p_prompt, p_sample,
           g_mix, w_in, g_q, g_k, hg_lb_raw, g_hg_out, w_up_attn, w_up_hgrn, w_out, g_ffn, w_ff_up, w_ff_down,
           w_ple, w_ple_gate):
    bm, bn = 2816, 256
    bm_ple = 1408
    bm_res = 2816
    xp, xs = x_prompt.reshape(SEQ, D_MODEL), x_sample.reshape(N_SAMPLE, D_MODEL)
    pp, ps = p_prompt.reshape(SEQ, PLE_DIM), p_sample.reshape(N_SAMPLE, PLE_DIM)
    lb = jnp.cumsum(jax.nn.softmax(hg_lb_raw.astype(F32), axis=0), axis=0)[0]
    rows = lambda width, dt: jax.ShapeDtypeStruct((M_ALL, width), dt)

    def plain(accs, i, j, rows_sl, a_refs, extra, outs):
        outs[0][rows_sl, :] = accs[0]

    n_mix = _rmsnorm_rows(xp, g_mix[0], x_tail=xs, name="norm_mix")
    z, = _matmul_rows([(n_mix, w_in[0])], bm=bm, bn=512, out_shapes=(rows(IN_WIDTH, F32),), epilogue=plain,
                      name="in_proj")

    o_attn, kn0, kn1, kn2 = _attn_prompt(z, g_q[0], g_k[0], seq=SEQ, sb=2048, out_rows=M_ALL,
                                         dils=tuple(d for _, d in DIL_GROUPS))
    zs = z[SEQ:]
    def by_group(col0):
        a = zs[:, col0:col0 + N_GROUPS * A_WIDTH].reshape(DEC_BATCH, DEC_SEQ, N_GROUPS, A_HEADS, HEAD_DIM)
        return a.transpose(0, 2, 1, 3, 4).reshape(DEC_BATCH, N_GROUPS, DEC_SEQ * A_HEADS, HEAD_DIM)

    caches = [c[0] for c in (cache_kv_w128, cache_kv_w512, cache_kv_w2048)]
    o_attn_s, kn_s = _attn_sample(by_group(COL_Q), by_group(COL_K), by_group(COL_V), caches,
                                  g_q[0], g_k[0], n_batch=DEC_BATCH, t_new=DEC_SEQ, groups=DIL_GROUPS)
    o_attn = lax.dynamic_update_slice(o_attn, o_attn_s.reshape(N_SAMPLE, A_WIDTH).astype(BF16), (SEQ, 0))

    o_hg, st_p = _hgrn_prompt(z, lb, g_hg_out[0], seq=SEQ, tb=256, hb=8, out_rows=M_ALL)
    o_hg_s, st_s = _hgrn_sample(zs[:, COL_HQ:COL_HF], zs[:, COL_HF:COL_HI], zs[:, COL_HI:COL_HOG],
                                zs[:, COL_HOG:COL_GA], lb, g_hg_out[0], state_hgrn[0],
                                n_batch=DEC_BATCH, t_new=DEC_SEQ)
    o_hg = lax.dynamic_update_slice(o_hg, o_hg_s.astype(BF16), (SEQ, 0))

    def merge_epilogue(accs, i, j, rows_sl, a_refs, extra, outs):
        ga, gb = extra[0][rows_sl, :], extra[1][rows_sl, :]
        outs[0][rows_sl, :] = (_sigmoid(ga) * accs[0] + _sigmoid(gb) * accs[1]).astype(BF16)

    def out_proj_epilogue(accs, i, j, rows_sl, a_refs, extra, outs):
        xp_ref, xs_ref, g_ref = extra
        x1_ref, x1g_ref, ssq_ref = outs

        def emit(rows, x1, ssq_before):
            x1_ref[rows, :] = x1
            x1g_ref[rows, :] = (x1 * g_ref[...]).astype(BF16)
            ssq_ref[rows, :] = ssq_before + jnp.sum(x1 * x1, axis=-1, keepdims=True)

        ssq_before = ssq_ref[rows_sl, :]
        emit(rows_sl, xp_ref[rows_sl, :] + accs[0], ssq_before)
        if rows_sl.stop == bm_res:
            t0 = bm_res - N_SAMPLE - rows_sl.start

            @pl.when(i == pl.num_programs(0) - 1)
            def _():
                emit(slice(bm_res - N_SAMPLE, bm_res), xs_ref[...] + accs[0][t0:, :], ssq_before[t0:, :])

    merged, = _matmul_rows([(o_attn, w_up_attn[0]), (o_hg, w_up_hgrn[0])], bm=bm, bn=bn, extras=(z, z),
                           extra_specs=(_tile_spec(bm, bn, COL_GA), _tile_spec(bm, bn, COL_GB)),
                           out_shapes=(rows(D_MODEL, BF16),), epilogue=merge_epilogue, name="up_merge")
    def zero_ssq(outs):
        outs[2][...] = jnp.zeros(outs[2].shape, F32)

    x1, x1g, ssq = _matmul_rows(
        [(merged, w_out[0])], bm=bm_res, bn=bn, extras=(xp, xs, g_ffn),
        extra_specs=(_tile_spec(bm_res, bn), pl.BlockSpec((N_SAMPLE, bn), lambda i, j: (0, j)),
                     pl.BlockSpec((1, bn), lambda i, j: (0, j))),
        out_shapes=(rows(D_MODEL, F32), rows(D_MODEL, BF16), rows(LANES, F32)),
        out_specs=(_tile_spec(bm_res, bn), _tile_spec(bm_res, bn),
                   pl.BlockSpec((bm_res, LANES), lambda i, j: (i, 0))),
        epilogue=out_proj_epilogue, first_col_init=zero_ssq, name="out_proj")

    def ffn_up_epilogue(accs, i, j, rows_sl, a_refs, extra, outs):
        inv_rms = lax.rsqrt(extra[0][rows_sl, 0:1] * (1.0 / D_MODEL) + NORM_EPS)
        outs[0][rows_sl, :] = jnp.square(jnp.maximum(accs[0] * inv_rms, 0.0)).astype(BF16)

    hid, = _matmul_rows([(x1g, w_ff_up[0])], bm=bm, bn=512, extras=(ssq,),
                        extra_specs=(pl.BlockSpec((bm, LANES), lambda i, j: (i, 0)),),
                        out_shapes=(rows(FFN_HIDDEN, BF16),), epilogue=ffn_up_epilogue, name="ffn_up")
    x2 = _matmul_ksplit(hid, w_ff_down[0], x1, bm=1408, bn=2048, bk=1024, name="ffn_down")

    n_ple_blocks = M_ALL // bm_ple

    def ple_epilogue(accs, i, j, rows_sl, a_refs, extra, outs):
        pp_ref, ps_ref, wp_ref = extra
        yp_ref, ys_ref = outs
        x2_tile = a_refs[0][rows_sl, pl.ds(pl.multiple_of(j * bn, bn), bn)]
        gate = _sigmoid(accs[0])
        wp = wp_ref[...].astype(BF16)
        yp_ref[rows_sl, :] = x2_tile + gate * _dot(pp_ref[rows_sl, :].astype(BF16), wp)
        if rows_sl.stop == bm_ple:
            s0 = bm_ple - N_SAMPLE - rows_sl.start

            @pl.when(i == n_ple_blocks - 1)
            def _():
                ys_ref[...] = x2_tile[s0:, :] + gate[s0:, :] * _dot(ps_ref[...].astype(BF16), wp)

    y_p, y_s = _matmul_rows(
        [(x2, w_ple_gate[0])], bm=bm_ple, bn=bn, extras=(pp, ps, w_ple[0]),
        extra_specs=(pl.BlockSpec((bm_ple, PLE_DIM), lambda i, j: (i, 0)),
                     pl.BlockSpec((N_SAMPLE, PLE_DIM), lambda i, j: (0, 0)),
                     pl.BlockSpec((PLE_DIM, bn), lambda i, j: (0, j))),
        out_shapes=(jax.ShapeDtypeStruct((SEQ, D_MODEL), F32), jax.ShapeDtypeStruct((N_SAMPLE, D_MODEL), F32)),
        out_specs=(pl.BlockSpec((bm_ple, bn), lambda i, j: (i, j)),
                   pl.BlockSpec((N_SAMPLE, bn), lambda i, j: (0, jnp.where(i == n_ple_blocks - 1, j, 0)))),
        epilogue=ple_epilogue, name="ple")

    y_prompt = y_p.reshape(1, SEQ, D_MODEL)
    y_sample = y_s.reshape(DEC_BATCH, DEC_SEQ, D_MODEL)
    kv_p, kv_s = [], []
    for g, (window, _) in enumerate(DIL_GROUPS):
        length = min(window, SEQ)
        kn_g = (kn0, kn1, kn2)[g][SEQ - length:].reshape(length, A_HEADS, HEAD_DIM)
        v_g = z[SEQ - length:SEQ, COL_V + g * A_WIDTH:COL_V + (g + 1) * A_WIDTH].reshape(length, A_HEADS, HEAD_DIM)
        kv_p.append(jnp.stack([kn_g, v_g], axis=1)[None, None])
        ks_g = kn_s[:, g].reshape(DEC_BATCH, DEC_SEQ, A_HEADS, HEAD_DIM)
        vs_g = zs[:, COL_V + g * A_WIDTH:COL_V + (g + 1) * A_WIDTH].reshape(DEC_BATCH, DEC_SEQ, A_HEADS, HEAD_DIM)
        kv_s.append(jnp.stack([ks_g, vs_g], axis=2)[None])
    return (y_prompt, y_sample, kv_p[0], kv_p[1], kv_p[2], st_p[None, None],
            kv_s[0], kv_s[1], kv_s[2], st_s[None])
```

```python
import functools

import numpy as np
import jax
import jax.numpy as jnp
from jax import lax
from jax.experimental import pallas as pl
from jax.experimental.pallas import tpu as pltpu

F32 = jnp.float32
BF16 = jnp.bfloat16

D_MODEL = 4096
SEQ = 8192
DEC_BATCH = 32
DEC_SEQ = 8
N_SAMPLE = DEC_BATCH * DEC_SEQ
M_ALL = SEQ + N_SAMPLE
DIL_GROUPS = ((128, 1), (512, 4), (2048, 16))
N_GROUPS = 3
A_HEADS = 8
HEAD_DIM = 128
A_WIDTH = A_HEADS * HEAD_DIM
BAND = 128
HG_HEADS = 16
HG_KDIM = 128
HG_VDIM = 128
HG_WIDTH = HG_HEADS * HG_VDIM
HG_CHUNK = 32
FFN_HIDDEN = 4 * D_MODEL
PLE_DIM = 256
NORM_EPS = 1e-6
ATTN_SCALE = HEAD_DIM ** -0.5

COL_Q = 0
COL_K = COL_Q + N_GROUPS * A_WIDTH
COL_V = COL_K + N_GROUPS * A_WIDTH
COL_HQ = COL_V + N_GROUPS * A_WIDTH
COL_HF = COL_HQ + HG_HEADS * HG_KDIM
COL_HI = COL_HF + HG_HEADS * HG_KDIM
COL_HOG = COL_HI + HG_WIDTH
COL_GA = COL_HOG + HG_WIDTH
COL_GB = COL_GA + D_MODEL
IN_WIDTH = COL_GB + D_MODEL

VMEM_LIMIT = 56 * 1024 * 1024
LANES = 128


def _dot(a, b):
    return jnp.dot(a, b, preferred_element_type=F32)


def _dot_nt(a, b):
    return lax.dot_general(a, b, (((1,), (1,)), ((), ())), preferred_element_type=F32)


def _dot_tn(a, b):
    return lax.dot_general(a, b, (((0,), (0,)), ((), ())), preferred_element_type=F32)


def _rms(x, g):
    return x * lax.rsqrt(jnp.mean(x * x, axis=-1, keepdims=True) + NORM_EPS) * g


def _sigmoid(x):
    return 1.0 / (1.0 + jnp.exp(-x))


def _store(outs, *vals):
    for o_ref, v in zip(outs, vals):
        o_ref[...] = v.astype(o_ref.dtype)


_CAST_ROWS = 128
_DOT_ROWS = 704


def _mm_rows_body(*refs, needs_cast, n_extra, n_out, epilogue, row_chunk, first_col_init):
    n_pairs = len(needs_cast)
    a_refs = refs[0:2 * n_pairs:2]
    w_refs = refs[1:2 * n_pairs:2]
    pos = 2 * n_pairs
    extra = refs[pos:pos + n_extra]
    outs = refs[pos + n_extra:pos + n_extra + n_out]
    scratch = list(refs[pos + n_extra + n_out:])
    i, j = pl.program_id(0), pl.program_id(1)
    lhs_refs = []
    for a_ref, cast in zip(a_refs, needs_cast):
        if cast:
            ab_ref = scratch.pop(0)

            @pl.when(j == 0)
            def _(a_ref=a_ref, ab_ref=ab_ref):
                def rows(c, carry):
                    r0 = pl.multiple_of(c * _CAST_ROWS, _CAST_ROWS)
                    ab_ref[pl.ds(r0, _CAST_ROWS), :] = a_ref[pl.ds(r0, _CAST_ROWS), :].astype(BF16)
                    return carry
                lax.fori_loop(0, a_ref.shape[0] // _CAST_ROWS, rows, 0)

            lhs_refs.append(ab_ref)
        else:
            lhs_refs.append(a_ref)
    if first_col_init is not None:
        @pl.when(j == 0)
        def _():
            first_col_init(outs)

    wbs = [w_ref[...].astype(BF16) for w_ref in w_refs]
    for r0 in range(0, lhs_refs[0].shape[0], row_chunk):
        rows = slice(r0, r0 + row_chunk)
        accs = [_dot(lhs_ref[rows, :], wb) for lhs_ref, wb in zip(lhs_refs, wbs)]
        epilogue(accs, i, j, rows, a_refs, extra, outs)


def _matmul_rows(pairs, *, bm, bn, extras=(), extra_specs=(), out_shapes, out_specs=None, epilogue, name,
                 row_chunk=_DOT_ROWS, first_col_init=None):
    m = pairs[0][0].shape[0]
    n = pairs[0][1].shape[1]
    assert m % bm == 0 and n % bn == 0 and bm % _CAST_ROWS == 0 and bm % row_chunk == 0
    in_specs, operands, scratch, needs_cast = [], [], [], []
    for a, w in pairs:
        kdim = w.shape[0]
        assert a.shape == (m, kdim) and w.shape[1] == n
        in_specs += [pl.BlockSpec((bm, kdim), lambda i, j: (i, 0), pipeline_mode=pl.Buffered(1)),
                     pl.BlockSpec((kdim, bn), lambda i, j: (0, j))]
        operands += [a, w]
        needs_cast.append(a.dtype != BF16)
        if needs_cast[-1]:
            scratch.append(pltpu.VMEM((bm, kdim), BF16))
    if out_specs is None:
        out_specs = tuple(pl.BlockSpec((bm, bn), lambda i, j: (i, j)) for _ in out_shapes)
    body = functools.partial(_mm_rows_body, needs_cast=tuple(needs_cast), n_extra=len(extras),
                             n_out=len(out_shapes), epilogue=epilogue, row_chunk=row_chunk,
                             first_col_init=first_col_init)
    return pl.pallas_call(
        body,
        grid=(m // bm, n // bn),
        in_specs=in_specs + list(extra_specs),
        out_specs=tuple(out_specs),
        out_shape=tuple(out_shapes),
        scratch_shapes=scratch,
        compiler_params=pltpu.CompilerParams(dimension_semantics=("arbitrary", "arbitrary"),
                                             vmem_limit_bytes=VMEM_LIMIT),
        name=name,
    )(*operands, *extras)


def _mm_ksplit_body(a_ref, w_ref, r_hbm, o_ref, r_buf, r_sem, *, row_chunk):
    j, i, k = pl.program_id(0), pl.program_id(1), pl.program_id(2)
    bm, bn = o_ref.shape
    r_copy = pltpu.make_async_copy(
        r_hbm.at[pl.ds(pl.multiple_of(i * bm, bm), bm), pl.ds(pl.multiple_of(j * bn, bn), bn)], r_buf, r_sem)

    @pl.when(k == 0)
    def _():
        r_copy.start()
        o_ref[...] = jnp.zeros(o_ref.shape, F32)

    wb = w_ref[...].astype(BF16)
    for r0 in range(0, bm, row_chunk):
        rows = slice(r0, r0 + row_chunk)
        o_ref[rows, :] += _dot(a_ref[rows, :], wb)

    @pl.when(k == pl.num_programs(2) - 1)
    def _():
        r_copy.wait()
        o_ref[...] += r_buf[...]


def _matmul_ksplit(a, w, resid, *, bm, bn, bk, name, row_chunk=_DOT_ROWS):
    m, kdim = a.shape
    n = w.shape[1]
    assert m % bm == 0 and n % bn == 0 and kdim % bk == 0 and bm % row_chunk == 0 and a.dtype == BF16
    assert resid.shape == (m, n) and resid.dtype == F32
    return pl.pallas_call(
        functools.partial(_mm_ksplit_body, row_chunk=row_chunk),
        grid=(n // bn, m // bm, kdim // bk),
        in_specs=[pl.BlockSpec((bm, bk), lambda j, i, k: (i, k)),
                  pl.BlockSpec((bk, bn), lambda j, i, k: (k, j)),
                  pl.BlockSpec(memory_space=pl.ANY)],
        out_specs=pl.BlockSpec((bm, bn), lambda j, i, k: (i, j), pipeline_mode=pl.Buffered(1)),
        out_shape=jax.ShapeDtypeStruct((m, n), F32),
        scratch_shapes=[pltpu.VMEM((bm, bn), F32), pltpu.SemaphoreType.DMA],
        compiler_params=pltpu.CompilerParams(dimension_semantics=("arbitrary", "arbitrary", "arbitrary"),
                                             vmem_limit_bytes=VMEM_LIMIT),
        name=name,
    )(a, w, resid)


def _rmsnorm_body(*refs, n_main):
    g_ref, o_ref = refs[-2:]
    if n_main is None:
        o_ref[...] = _rms(refs[0][...], g_ref[...]).astype(o_ref.dtype)
        return
    i = pl.program_id(0)

    @pl.when(i < n_main)
    def _():
        o_ref[...] = _rms(refs[0][...], g_ref[...]).astype(o_ref.dtype)

    @pl.when(i >= n_main)
    def _():
        o_ref[...] = _rms(refs[1][...], g_ref[...]).astype(o_ref.dtype)


def _rmsnorm_rows(x, g, *, x_tail=None, bm=256, name="rmsnorm"):
    m, d = x.shape
    assert m % bm == 0
    n_main = m // bm
    in_specs = [pl.BlockSpec((bm, d), lambda i: (jnp.minimum(i, n_main - 1), 0))]
    operands = [x]
    if x_tail is not None:
        assert x_tail.shape[0] % bm == 0
        in_specs.append(pl.BlockSpec((bm, d), lambda i: (jnp.maximum(i - n_main, 0), 0)))
        operands.append(x_tail)
        m += x_tail.shape[0]
    in_specs.append(pl.BlockSpec((1, d), lambda i: (0, 0)))
    return pl.pallas_call(
        functools.partial(_rmsnorm_body, n_main=None if x_tail is None else n_main),
        grid=(m // bm,),
        in_specs=in_specs,
        out_specs=pl.BlockSpec((bm, d), lambda i: (i, 0)),
        out_shape=jax.ShapeDtypeStruct((m, d), BF16),
        compiler_params=pltpu.CompilerParams(dimension_semantics=("arbitrary",)),
        name=name,
    )(*operands, g.reshape(1, d))


def _ld_rows(ref, start, dil, n=BAND):
    if dil == 1:
        return ref[pl.ds(start, n), :]
    return ref[pl.ds(start, n, stride=dil), :]


def _st_rows(ref, g, start, dil, val):
    if dil == 1:
        ref[g, pl.ds(start, BAND), :] = val
    else:
        ref[g, pl.ds(start, BAND, stride=dil), :] = val


def _attn_prompt_body(slope_ref, gq_ref, gk_ref,
                      q0, k0, v0, kp0, vp0, q1, k1, v1, kp1, vp1, q2, k2, v2, kp2, vp2,
                      o_ref, kn0, kn1, kn2,
                      qn_s, kn_s, vv_s, og_s, lse_s, *, sb, dils):
    first_block = pl.program_id(1) == 0
    groups = ((q0, k0, v0, kp0, vp0, kn0), (q1, k1, v1, kp1, vp1, kn1), (q2, k2, v2, kp2, vp2, kn2))
    ii = lax.broadcasted_iota(jnp.int32, (BAND, BAND), 0)
    jj = lax.broadcasted_iota(jnp.int32, (BAND, BAND), 1)
    dist_cur = (ii - jj).astype(F32)
    dist_prev = (BAND + ii - jj).astype(F32)
    ok_cur = jj <= ii
    ok_prev = jj >= ii
    neg_inf = jnp.float32(-jnp.inf)
    base = kn_s.shape[0] - sb

    for g, dil in enumerate(dils):
        q_ref, k_ref, v_ref, kp_ref, vp_ref, kn_out = groups[g]
        pr = BAND * dil
        gq = gq_ref[g:g + 1, :]
        gk = gk_ref[g:g + 1, :]
        slope = slope_ref[0, :, g * HEAD_DIM:(g + 1) * HEAD_DIM]
        qn_s[...] = _rms(q_ref[...], gq)
        kn = _rms(k_ref[...], gk)
        kn_out[...] = kn
        kn_s[pl.ds(base, sb), :] = kn
        kn_s[pl.ds(base - pr, pr), :] = _rms(kp_ref[...], gk)
        vv_s[pl.ds(base, sb), :] = v_ref[...]
        vv_s[pl.ds(base - pr, pr), :] = vp_ref[...]
        bias_cur = jnp.where(ok_cur, -slope * (dist_cur * dil), neg_inf)
        bias_prev = jnp.where(ok_prev, -slope * (dist_prev * dil), neg_inf)
        bias = jnp.concatenate([bias_prev, bias_cur], axis=1)
        bias_first = jnp.concatenate([jnp.where(first_block, neg_inf, bias_prev), bias_cur], axis=1)
        ones = jnp.ones((2 * BAND, HEAD_DIM), BF16)
        for r in range(dil):
            for s in range(sb // pr):
                c0 = s * pr + r
                qs = _ld_rows(qn_s, c0, dil).astype(BF16)
                kk = _ld_rows(kn_s, base + c0 - pr, dil, 2 * BAND).astype(BF16)
                vv = _ld_rows(vv_s, base + c0 - pr, dil, 2 * BAND).astype(BF16)
                lg = _dot_nt(qs, kk) * ATTN_SCALE + (bias_first if s == 0 else bias)
                mx = jnp.max(lg, axis=-1, keepdims=True)
                p = jnp.exp(lg - mx).astype(BF16)
                oe = _dot(p, jnp.concatenate([vv, ones], axis=1))
                den = oe[:, HEAD_DIM:]
                _st_rows(og_s, g, c0, dil, oe[:, :HEAD_DIM] / den)
                _st_rows(lse_s, g, c0, dil, mx + jnp.log(den))

    l0, l1, l2 = lse_s[0], lse_s[1], lse_s[2]
    mx = jnp.maximum(jnp.maximum(l0, l1), l2)
    w0, w1, w2 = jnp.exp(l0 - mx), jnp.exp(l1 - mx), jnp.exp(l2 - mx)
    o = (w0 * og_s[0] + w1 * og_s[1] + w2 * og_s[2]) / (w0 + w1 + w2)
    o_ref[...] = o.astype(o_ref.dtype)


def _alibi_slopes():
    n = N_GROUPS * A_HEADS
    e = jnp.arange(1, n + 1, dtype=F32)
    return jnp.exp2(-8.0 * e / n).reshape(N_GROUPS, A_HEADS)


def _attn_prompt(z, g_q, g_k, *, seq, sb, dils, out_rows=None, n_heads=A_HEADS, col_q=COL_Q, col_k=COL_K,
                 col_v=COL_V):
    out_rows = seq if out_rows is None else out_rows
    nb = seq // sb
    slopes = _alibi_slopes()
    slope_arr = jnp.broadcast_to(slopes.T[:, None, :, None], (n_heads, 1, N_GROUPS, HEAD_DIM))
    slope_arr = slope_arr.reshape(n_heads, 1, N_GROUPS * HEAD_DIM)
    a_width = n_heads * HEAD_DIM
    in_specs = [pl.BlockSpec((1, 1, N_GROUPS * HEAD_DIM), lambda h, i: (h, 0, 0)),
                pl.BlockSpec((N_GROUPS, HEAD_DIM), lambda h, i: (0, 0)),
                pl.BlockSpec((N_GROUPS, HEAD_DIM), lambda h, i: (0, 0))]
    operands = [slope_arr, g_q, g_k]
    max_pr = BAND * max(dils)
    for g, dil in enumerate(dils):
        pr = BAND * dil
        ratio = sb // pr
        cq = (col_q + g * a_width) // HEAD_DIM
        ck = (col_k + g * a_width) // HEAD_DIM
        cv = (col_v + g * a_width) // HEAD_DIM
        cur = lambda c: pl.BlockSpec((sb, HEAD_DIM), lambda h, i, c=c: (i, c + h))
        prev = lambda c: pl.BlockSpec((pr, HEAD_DIM),
                                      lambda h, i, c=c, ratio=ratio: (jnp.maximum(i * ratio - 1, 0), c + h))
        in_specs += [cur(cq), cur(ck), cur(cv), prev(ck), prev(cv)]
        operands += [z, z, z, z, z]
    out_block = pl.BlockSpec((sb, HEAD_DIM), lambda h, i: (i, h))
    out_shape = (jax.ShapeDtypeStruct((out_rows, a_width), BF16),) + tuple(
        jax.ShapeDtypeStruct((seq, a_width), F32) for _ in dils)
    body = functools.partial(_attn_prompt_body, sb=sb, dils=tuple(dils))
    return pl.pallas_call(
        body,
        grid=(n_heads, nb),
        in_specs=in_specs,
        out_specs=(out_block,) * 4,
        out_shape=out_shape,
        scratch_shapes=[pltpu.VMEM((sb, HEAD_DIM), F32),
                        pltpu.VMEM((sb + max_pr, HEAD_DIM), F32),
                        pltpu.VMEM((sb + max_pr, HEAD_DIM), F32),
                        pltpu.VMEM((N_GROUPS, sb, HEAD_DIM), F32),
                        pltpu.VMEM((N_GROUPS, sb, HEAD_DIM), F32)],
        compiler_params=pltpu.CompilerParams(dimension_semantics=("arbitrary", "arbitrary"),
                                             vmem_limit_bytes=VMEM_LIMIT),
        name="attn_prompt",
    )(*operands)


def _sample_problems(dil, t_new):
    n_prob = min(dil, t_new)
    return n_prob, t_new // n_prob


def _sample_bias_tables(groups, t_new, n_heads):
    n_soft = len(groups) * n_heads
    tabs_c, tabs_n = [], []
    for g, (window, dil) in enumerate(groups):
        n_prob, tok_per = _sample_problems(dil, t_new)
        slopes = 2.0 ** (-8.0 * (g * n_heads + np.arange(n_heads) + 1) / n_soft)
        rows = np.arange(tok_per * n_heads)
        i, hp = rows // n_heads, rows % n_heads
        cols = np.arange(BAND * n_heads)
        m, h = cols // n_heads, cols % n_heads
        dist = window + i[:, None] * n_prob - m[None, :] * dil
        ok = (h[None, :] == hp[:, None]) & (dist <= window) & (dist > 0) & (dist % dil == 0)
        tabs_c.append(np.where(ok, -slopes[hp][:, None] * dist, -np.inf).astype(np.float32))
        cols = np.arange(t_new * n_heads)
        s, h = cols // n_heads, cols % n_heads
        per_p = []
        for p in range(n_prob):
            dist = (p + i * n_prob)[:, None] - s[None, :]
            ok = (h[None, :] == hp[:, None]) & (dist >= 0) & (dist % dil == 0)
            per_p.append(np.where(ok, -slopes[hp][:, None] * dist, -np.inf).astype(np.float32))
        tabs_n.append(np.stack(per_p))
    return tabs_c, tabs_n


def _attn_sample_body(gq_ref, gk_ref, q_ref, k_ref, v_ref, c0_ref, c1_ref, c2_ref,
                      bc0, bc1, bc2, bn0, bn1, bn2, o_ref, kn_ref, og_s, lse_s, *, t_new, groups, n_heads):
    caches = (c0_ref, c1_ref, c2_ref)
    bias_c = (bc0, bc1, bc2)
    bias_n = (bn0, bn1, bn2)
    kv_rows = 2 * n_heads
    for g, (window, dil) in enumerate(groups):
        c_ref = caches[g]
        n_prob, tok_per = _sample_problems(dil, t_new)
        qn = _rms(q_ref[0, g], gq_ref[g:g + 1, :])
        kn = _rms(k_ref[0, g], gk_ref[g:g + 1, :])
        kn_ref[0, g] = kn
        knb = kn.astype(BF16)
        vnb = v_ref[0, g].astype(BF16)
        for p in range(n_prob):
            toks = [p + i * n_prob for i in range(tok_per)]
            parts = [qn[t * n_heads:(t + 1) * n_heads] for t in toks]
            qp = (parts[0] if tok_per == 1 else jnp.concatenate(parts, axis=0)).astype(BF16)
            kc = c_ref[:, p * kv_rows:p * kv_rows + n_heads, :].reshape(BAND * n_heads, HEAD_DIM).astype(BF16)
            vc = c_ref[:, p * kv_rows + n_heads:(p + 1) * kv_rows, :].reshape(BAND * n_heads, HEAD_DIM).astype(BF16)
            lc = _dot_nt(qp, kc) * ATTN_SCALE + bias_c[g][...]
            ln = _dot_nt(qp, knb) * ATTN_SCALE + bias_n[g][p]
            mx = jnp.maximum(jnp.max(lc, axis=-1, keepdims=True), jnp.max(ln, axis=-1, keepdims=True))
            pc = jnp.exp(lc - mx)
            pn = jnp.exp(ln - mx)
            ssum = jnp.sum(pc, axis=-1, keepdims=True) + jnp.sum(pn, axis=-1, keepdims=True)
            o = (_dot(pc.astype(BF16), vc) + _dot(pn.astype(BF16), vnb)) / ssum
            lse = jnp.broadcast_to(mx + jnp.log(ssum), o.shape)
            for i, t in enumerate(toks):
                og_s[g, t * n_heads:(t + 1) * n_heads, :] = o[i * n_heads:(i + 1) * n_heads]
                lse_s[g, t * n_heads:(t + 1) * n_heads, :] = lse[i * n_heads:(i + 1) * n_heads]
    l0, l1, l2 = lse_s[0], lse_s[1], lse_s[2]
    mx = jnp.maximum(jnp.maximum(l0, l1), l2)
    w0, w1, w2 = jnp.exp(l0 - mx), jnp.exp(l1 - mx), jnp.exp(l2 - mx)
    o_ref[0] = (w0 * og_s[0] + w1 * og_s[1] + w2 * og_s[2]) / (w0 + w1 + w2)


def _attn_sample(qs, ks, vs, caches, g_q, g_k, *, n_batch, t_new, groups, n_heads=A_HEADS):
    n_g = len(groups)
    rows = t_new * n_heads
    tabs_c, tabs_n = _sample_bias_tables(groups, t_new, n_heads)
    new_spec = lambda: pl.BlockSpec((1, n_g, rows, HEAD_DIM), lambda b: (b, 0, 0, 0))
    in_specs = [pl.BlockSpec((n_g, HEAD_DIM), lambda b: (0, 0)),
                pl.BlockSpec((n_g, HEAD_DIM), lambda b: (0, 0)),
                new_spec(), new_spec(), new_spec()]
    operands = [g_q, g_k, qs, ks, vs]
    for c, (window, dil) in zip(caches, groups):
        assert c.shape[1] == window and window == BAND * dil
        n_prob, _ = _sample_problems(dil, t_new)
        c3 = c.reshape(n_batch * BAND, dil * 2 * n_heads, HEAD_DIM)
        in_specs.append(pl.BlockSpec((BAND, n_prob * 2 * n_heads, HEAD_DIM), lambda b: (b, 0, 0)))
        operands.append(c3)
    for tab in tabs_c:
        in_specs.append(pl.BlockSpec(tab.shape, lambda b: (0, 0)))
        operands.append(jnp.asarray(tab))
    for tab in tabs_n:
        in_specs.append(pl.BlockSpec(tab.shape, lambda b: (0, 0, 0)))
        operands.append(jnp.asarray(tab))
    body = functools.partial(_attn_sample_body, t_new=t_new, groups=tuple(groups), n_heads=n_heads)
    return pl.pallas_call(
        body,
        grid=(n_batch,),
        in_specs=in_specs,
        out_specs=(pl.BlockSpec((1, rows, HEAD_DIM), lambda b: (b, 0, 0)), new_spec()),
        out_shape=(jax.ShapeDtypeStruct((n_batch, rows, HEAD_DIM), F32),
                   jax.ShapeDtypeStruct((n_batch, n_g, rows, HEAD_DIM), F32)),
        scratch_shapes=[pltpu.VMEM((n_g, rows, HEAD_DIM), F32), pltpu.VMEM((n_g, rows, HEAD_DIM), F32)],
        compiler_params=pltpu.CompilerParams(dimension_semantics=("arbitrary",),
                                             vmem_limit_bytes=VMEM_LIMIT),
        name="attn_sample",
    )(*operands)


def _split3(x):
    hi = x.astype(BF16)
    r1 = x - hi.astype(F32)
    mid = r1.astype(BF16)
    lo = (r1 - mid.astype(F32)).astype(BF16)
    return hi, mid, lo


def _hgrn_heads(hq, hf, hi, hog, lb, g_out, sts, *, tb, n_valid=None):
    c = HG_CHUNK
    nh = len(sts)
    sts = list(sts)
    q = hq * _sigmoid(hq) * (HG_KDIM ** -0.5)
    gate = lb + (1.0 - lb) * _sigmoid(hf)
    log_g = jnp.log(gate)
    k = (1.0 - lb) * _sigmoid(-hf)
    if n_valid is not None:
        rows = lax.broadcasted_iota(jnp.int32, hq.shape, 0)
        log_g = jnp.where(rows < n_valid, log_g, 0.0)
        k = jnp.where(rows < n_valid, k, 0.0)
        q = jnp.where(rows < n_valid, q, 0.0)
    row = lax.broadcasted_iota(jnp.int32, (tb, tb), 0)
    col = lax.broadcasted_iota(jnp.int32, (tb, tb), 1)
    same = (row // c) == (col // c)
    tri_ok = same & (col <= row)
    tri = jnp.where(tri_ok, 1.0, 0.0).astype(BF16)
    blk = jnp.where(same, 1.0, 0.0).astype(BF16)
    p_hi, p_mid, p_lo = _split3(log_g)
    b = _dot(tri, p_hi) + _dot(tri, p_mid) + _dot(tri, p_lo)
    b_last = _dot(blk, p_hi) + _dot(blk, p_mid) + _dot(blk, p_lo)
    q_dec = (q * jnp.exp(b)).astype(BF16)
    k_dec = (k * jnp.exp(-b)).astype(BF16)
    k_end = (k * jnp.exp(b_last - b)).astype(BF16)
    decay = jnp.exp(b_last)
    vb = hi.astype(BF16)
    head = lambda x, h: x[:, h * HG_KDIM:(h + 1) * HG_KDIM]
    o_intra = []
    for h in range(nh):
        a = jnp.where(tri_ok, _dot_nt(head(q_dec, h), head(k_dec, h)), 0.0)
        o_intra.append(_dot(a.astype(BF16), head(vb, h)))
    parts = [[] for _ in range(nh)]
    for ci in range(tb // c):
        sl = slice(ci * c, (ci + 1) * c)
        for h in range(nh):
            parts[h].append(_dot_nt(head(q_dec, h)[sl], sts[h].astype(BF16)))
            sts[h] = sts[h] * head(decay, h)[ci * c:ci * c + 1, :] + _dot_tn(head(vb, h)[sl], head(k_end, h)[sl])
    outs = []
    for h in range(nh):
        o = o_intra[h] + (jnp.concatenate(parts[h], axis=0) if len(parts[h]) > 1 else parts[h][0])
        outs.append(_rms(o, g_out) * _sigmoid(head(hog, h)))
    return outs, sts


def _hgrn_prompt_body(hq_ref, hf_ref, hi_ref, hog_ref, lb_ref, go_ref, o_ref, s_ref, st_s, *, tb, nt, hb):
    t = pl.program_id(1)

    @pl.when(t == 0)
    def _():
        st_s[...] = jnp.zeros_like(st_s)

    outs, sts = _hgrn_heads(hq_ref[...], hf_ref[...], hi_ref[...], hog_ref[...], lb_ref[0], go_ref[...],
                            [st_s[hh] for hh in range(hb)], tb=tb)
    for hh in range(hb):
        o_ref[:, hh * HG_VDIM:(hh + 1) * HG_VDIM] = outs[hh].astype(o_ref.dtype)
        st_s[hh] = sts[hh]

    @pl.when(t == nt - 1)
    def _():
        for hh in range(hb):
            s_ref[hh] = sts[hh].T


def _hgrn_prompt(z, lb, g_out, *, seq, tb, hb, out_rows=None, n_heads=HG_HEADS, col_hq=COL_HQ, col_hf=COL_HF,
                 col_hi=COL_HI, col_hog=COL_HOG):
    out_rows = seq if out_rows is None else out_rows
    nt = seq // tb
    bw = hb * HG_KDIM
    assert all(c0 % bw == 0 for c0 in (col_hq, col_hf, col_hi, col_hog))
    blk = lambda c0: pl.BlockSpec((tb, bw), lambda h, t, c=c0 // bw: (t, c + h))
    body = functools.partial(_hgrn_prompt_body, tb=tb, nt=nt, hb=hb)
    return pl.pallas_call(
        body,
        grid=(n_heads // hb, nt),
        in_specs=[blk(col_hq), blk(col_hf), blk(col_hi), blk(col_hog),
                  pl.BlockSpec((1, 1, bw), lambda h, t: (h, 0, 0)),
                  pl.BlockSpec((1, HG_VDIM), lambda h, t: (0, 0))],
        out_specs=(pl.BlockSpec((tb, bw), lambda h, t: (t, h)),
                   pl.BlockSpec((hb, HG_KDIM, HG_VDIM), lambda h, t: (h, 0, 0))),
        out_shape=(jax.ShapeDtypeStruct((out_rows, n_heads * HG_VDIM), BF16),
                   jax.ShapeDtypeStruct((n_heads, HG_KDIM, HG_VDIM), F32)),
        scratch_shapes=[pltpu.VMEM((hb, HG_VDIM, HG_KDIM), F32)],
        compiler_params=pltpu.CompilerParams(dimension_semantics=("arbitrary", "arbitrary"),
                                             vmem_limit_bytes=VMEM_LIMIT),
        name="hgrn_prompt",
    )(z, z, z, z, lb.reshape(n_heads // hb, 1, bw), g_out.reshape(1, HG_VDIM))


def _hgrn_sample_body(hq_ref, hf_ref, hi_ref, hog_ref, lb_ref, go_ref, s0_ref, o_ref, s_ref, *, t_new, n_heads):
    pad = jnp.zeros((HG_CHUNK - t_new, n_heads * HG_KDIM), F32)
    ext = lambda ref: jnp.concatenate([ref[...], pad], axis=0)
    outs, sts = _hgrn_heads(ext(hq_ref), ext(hf_ref), ext(hi_ref), ext(hog_ref), lb_ref[...], go_ref[...],
                            [s0_ref[0, h].T for h in range(n_heads)], tb=HG_CHUNK, n_valid=t_new)
    for h in range(n_heads):
        o_ref[:, h * HG_VDIM:(h + 1) * HG_VDIM] = outs[h][:t_new]
        s_ref[0, h] = sts[h].T


def _hgrn_sample(hq, hf, hi, hog, lb, g_out, s0, *, n_batch, t_new, n_heads=HG_HEADS):
    width = n_heads * HG_KDIM
    row = lambda: pl.BlockSpec((t_new, width), lambda b: (b, 0))
    st = lambda: pl.BlockSpec((1, n_heads, HG_KDIM, HG_VDIM), lambda b: (b, 0, 0, 0))
    body = functools.partial(_hgrn_sample_body, t_new=t_new, n_heads=n_heads)
    return pl.pallas_call(
        body,
        grid=(n_batch,),
        in_specs=[row(), row(), row(), row(),
                  pl.BlockSpec((1, width), lambda b: (0, 0)),
                  pl.BlockSpec((1, HG_VDIM), lambda b: (0, 0)),
                  st()],
        out_specs=(row(), st()),
        out_shape=(jax.ShapeDtypeStruct((n_batch * t_new, width), F32),
                   jax.ShapeDtypeStruct(s0.shape, F32)),
        compiler_params=pltpu.CompilerParams(dimension_semantics=("arbitrary",),
                                             vmem_limit_bytes=VMEM_LIMIT),
        name="hgrn_sample",
    )(hq, hf, hi, hog, lb.reshape(1, width), g_out.reshape(1, HG_VDIM), s0)


def _tile_spec(bm, bn, col0=0):
    assert col0 % bn == 0
    cb = col0 // bn
    return pl.BlockSpec((bm, bn), lambda i, j: (i, cb + j))


def kernel(x_prompt, x_sample, cache_kv_w128, cache_kv_w512, cache_kv_w2048, state_hgrn, p_prompt, p_sample,
           g_mix, w_in, g_q, g_k, hg_lb_raw, g_hg_out, w_up_attn, w_up_hgrn, w_out, g_ffn, w_ff_up, w_ff_down,
           w_ple, w_ple_gate):
    bm, bn = 2816, 256
    bm_ple = 1408
    bm_res = 2816
    xp, xs = x_prompt.reshape(SEQ, D_MODEL), x_sample.reshape(N_SAMPLE, D_MODEL)
    pp, ps = p_prompt.reshape(SEQ, PLE_DIM), p_sample.reshape(N_SAMPLE, PLE_DIM)
    lb = jnp.cumsum(jax.nn.softmax(hg_lb_raw.astype(F32), axis=0), axis=0)[0]
    rows = lambda width, dt: jax.ShapeDtypeStruct((M_ALL, width), dt)

    def plain(accs, i, j, rows_sl, a_refs, extra, outs):
        outs[0][rows_sl, :] = accs[0]

    n_mix = _rmsnorm_rows(xp, g_mix[0], x_tail=xs, name="norm_mix")
    z, = _matmul_rows([(n_mix, w_in[0])], bm=bm, bn=512, out_shapes=(rows(IN_WIDTH, F32),), epilogue=plain,
                      name="in_proj")

    o_attn, kn0, kn1, kn2 = _attn_prompt(z, g_q[0], g_k[0], seq=SEQ, sb=2048, out_rows=M_ALL,
                                         dils=tuple(d for _, d in DIL_GROUPS))
    zs = z[SEQ:]
    def by_group(col0):
        a = zs[:, col0:col0 + N_GROUPS * A_WIDTH].reshape(DEC_BATCH, DEC_SEQ, N_GROUPS, A_HEADS, HEAD_DIM)
        return a.transpose(0, 2, 1, 3, 4).reshape(DEC_BATCH, N_GROUPS, DEC_SEQ * A_HEADS, HEAD_DIM)

    caches = [c[0] for c in (cache_kv_w128, cache_kv_w512, cache_kv_w2048)]
    o_attn_s, kn_s = _attn_sample(by_group(COL_Q), by_group(COL_K), by_group(COL_V), caches,
                                  g_q[0], g_k[0], n_batch=DEC_BATCH, t_new=DEC_SEQ, groups=DIL_GROUPS)
    o_attn = lax.dynamic_update_slice(o_attn, o_attn_s.reshape(N_SAMPLE, A_WIDTH).astype(BF16), (SEQ, 0))

    o_hg, st_p = _hgrn_prompt(z, lb, g_hg_out[0], seq=SEQ, tb=256, hb=8, out_rows=M_ALL)
    o_hg_s, st_s = _hgrn_sample(zs[:, COL_HQ:COL_HF], zs[:, COL_HF:COL_HI], zs[:, COL_HI:COL_HOG],
                                zs[:, COL_HOG:COL_GA], lb, g_hg_out[0], state_hgrn[0],
                                n_batch=DEC_BATCH, t_new=DEC_SEQ)
    o_hg = lax.dynamic_update_slice(o_hg, o_hg_s.astype(BF16), (SEQ, 0))

    def merge_epilogue(accs, i, j, rows_sl, a_refs, extra, outs):
        ga, gb = extra[0][rows_sl, :], extra[1][rows_sl, :]
        outs[0][rows_sl, :] = (_sigmoid(ga) * accs[0] + _sigmoid(gb) * accs[1]).astype(BF16)

    def out_proj_epilogue(accs, i, j, rows_sl, a_refs, extra, outs):
        xp_ref, xs_ref, g_ref = extra
        x1_ref, x1g_ref, ssq_ref = outs

        def emit(rows, x1, ssq_before):
            x1_ref[rows, :] = x1
            x1g_ref[rows, :] = (x1 * g_ref[...]).astype(BF16)
            ssq_ref[rows, :] = ssq_before + jnp.sum(x1 * x1, axis=-1, keepdims=True)

        ssq_before = ssq_ref[rows_sl, :]
        emit(rows_sl, xp_ref[rows_sl, :] + accs[0], ssq_before)
        if rows_sl.stop == bm_res:
            t0 = bm_res - N_SAMPLE - rows_sl.start

            @pl.when(i == pl.num_programs(0) - 1)
            def _():
                emit(slice(bm_res - N_SAMPLE, bm_res), xs_ref[...] + accs[0][t0:, :], ssq_before[t0:, :])

    merged, = _matmul_rows([(o_attn, w_up_attn[0]), (o_hg, w_up_hgrn[0])], bm=bm, bn=bn, extras=(z, z),
                           extra_specs=(_tile_spec(bm, bn, COL_GA), _tile_spec(bm, bn, COL_GB)),
                           out_shapes=(rows(D_MODEL, BF16),), epilogue=merge_epilogue, name="up_merge")
    def zero_ssq(outs):
        outs[2][...] = jnp.zeros(outs[2].shape, F32)

    x1, x1g, ssq = _matmul_rows(
        [(merged, w_out[0])], bm=bm_res, bn=bn, extras=(xp, xs, g_ffn),
        extra_specs=(_tile_spec(bm_res, bn), pl.BlockSpec((N_SAMPLE, bn), lambda i, j: (0, j)),
                     pl.BlockSpec((1, bn), lambda i, j: (0, j))),
        out_shapes=(rows(D_MODEL, F32), rows(D_MODEL, BF16), rows(LANES, F32)),
        out_specs=(_tile_spec(bm_res, bn), _tile_spec(bm_res, bn),
                   pl.BlockSpec((bm_res, LANES), lambda i, j: (i, 0))),
        epilogue=out_proj_epilogue, first_col_init=zero_ssq, name="out_proj")

    def ffn_up_epilogue(accs, i, j, rows_sl, a_refs, extra, outs):
        inv_rms = lax.rsqrt(extra[0][rows_sl, 0:1] * (1.0 / D_MODEL) + NORM_EPS)
        outs[0][rows_sl, :] = jnp.square(jnp.maximum(accs[0] * inv_rms, 0.0)).astype(BF16)

    hid, = _matmul_rows([(x1g, w_ff_up[0])], bm=bm, bn=512, extras=(ssq,),
                        extra_specs=(pl.BlockSpec((bm, LANES), lambda i, j: (i, 0)),),
                        out_shapes=(rows(FFN_HIDDEN, BF16),), epilogue=ffn_up_epilogue, name="ffn_up")
    x2 = _matmul_ksplit(hid, w_ff_down[0], x1, bm=1408, bn=2048, bk=1024, name="ffn_down")

    n_ple_blocks = M_ALL // bm_ple

    def ple_epilogue(accs, i, j, rows_sl, a_refs, extra, outs):
        pp_ref, ps_ref, wp_ref = extra
        yp_ref, ys_ref = outs
        x2_tile = a_refs[0][rows_sl, pl.ds(pl.multiple_of(j * bn, bn), bn)]
        gate = _sigmoid(accs[0])
        wp = wp_ref[...].astype(BF16)
        yp_ref[rows_sl, :] = x2_tile + gate * _dot(pp_ref[rows_sl, :].astype(BF16), wp)
        if rows_sl.stop == bm_ple:
            s0 = bm_ple - N_SAMPLE - rows_sl.start

            @pl.when(i == n_ple_blocks - 1)
            def _():
                ys_ref[...] = x2_tile[s0:, :] + gate[s0:, :] * _dot(ps_ref[...].astype(BF16), wp)

    y_p, y_s = _matmul_rows(
        [(x2, w_ple_gate[0])], bm=bm_ple, bn=bn, extras=(pp, ps, w_ple[0]),
        extra_specs=(pl.BlockSpec((bm_ple, PLE_DIM), lambda i, j: (i, 0)),
                     pl.BlockSpec((N_SAMPLE, PLE_DIM), lambda i, j: (0, 0)),
                     pl.BlockSpec((PLE_DIM, bn), lambda i, j: (0, j))),
        out_shapes=(jax.ShapeDtypeStruct((SEQ, D_MODEL), F32), jax.ShapeDtypeStruct((N_SAMPLE, D_MODEL), F32)),
        out_specs=(pl.BlockSpec((bm_ple, bn), lambda i, j: (i, j)),
                   pl.BlockSpec((N_SAMPLE, bn), lambda i, j: (0, jnp.where(i == n_ple_blocks - 1, j, 0)))),
        epilogue=ple_epilogue, name="ple")

    y_prompt = y_p.reshape(1, SEQ, D_MODEL)
    y_sample = y_s.reshape(DEC_BATCH, DEC_SEQ, D_MODEL)
    kv_p, kv_s = [], []
    for g, (window, _) in enumerate(DIL_GROUPS):
        length = min(window, SEQ)
        kn_g = (kn0, kn1, kn2)[g][SEQ - length:].reshape(length, A_HEADS, HEAD_DIM)
        v_g = z[SEQ - length:SEQ, COL_V + g * A_WIDTH:COL_V + (g + 1) * A_WIDTH].reshape(length, A_HEADS, HEAD_DIM)
        kv_p.append(jnp.stack([kn_g, v_g], axis=1)[None, None])
        ks_g = kn_s[:, g].reshape(DEC_BATCH, DEC_SEQ, A_HEADS, HEAD_DIM)
        vs_g = zs[:, COL_V + g * A_WIDTH:COL_V + (g + 1) * A_WIDTH].reshape(DEC_BATCH, DEC_SEQ, A_HEADS, HEAD_DIM)
        kv_s.append(jnp.stack([ks_g, vs_g], axis=2)[None])
    return (y_prompt, y_sample, kv_p[0], kv_p[1], kv_p[2], st_p[None, None],
            kv_s[0], kv_s[1], kv_s[2], st_s[None])
```

```python
import functools

import numpy as np
import jax
import jax.numpy as jnp
from jax import lax
from jax.experimental import pallas as pl
from jax.experimental.pallas import tpu as pltpu

F32 = jnp.float32
BF16 = jnp.bfloat16

D_MODEL = 4096
SEQ = 8192
DEC_BATCH = 32
DEC_SEQ = 8
N_SAMPLE = DEC_BATCH * DEC_SEQ
M_ALL = SEQ + N_SAMPLE
DIL_GROUPS = ((128, 1), (512, 4), (2048, 16))
N_GROUPS = 3
A_HEADS = 8
HEAD_DIM = 128
A_WIDTH = A_HEADS * HEAD_DIM
BAND = 128
HG_HEADS = 16
HG_KDIM = 128
HG_VDIM = 128
HG_WIDTH = HG_HEADS * HG_VDIM
HG_CHUNK = 32
FFN_HIDDEN = 4 * D_MODEL
PLE_DIM = 256
NORM_EPS = 1e-6
ATTN_SCALE = HEAD_DIM ** -0.5

COL_Q = 0
COL_K = COL_Q + N_GROUPS * A_WIDTH
COL_V = COL_K + N_GROUPS * A_WIDTH
COL_HQ = COL_V + N_GROUPS * A_WIDTH
COL_HF = COL_HQ + HG_HEADS * HG_KDIM
COL_HI = COL_HF + HG_HEADS * HG_KDIM
COL_HOG = COL_HI + HG_WIDTH
COL_GA = COL_HOG + HG_WIDTH
COL_GB = COL_GA + D_MODEL
IN_WIDTH = COL_GB + D_MODEL

VMEM_LIMIT = 56 * 1024 * 1024
LANES = 128


def _dot(a, b):
    return jnp.dot(a, b, preferred_element_type=F32)


def _dot_nt(a, b):
    return lax.dot_general(a, b, (((1,), (1,)), ((), ())), preferred_element_type=F32)


def _dot_tn(a, b):
    return lax.dot_general(a, b, (((0,), (0,)), ((), ())), preferred_element_type=F32)


def _rms(x, g):
    return x * lax.rsqrt(jnp.mean(x * x, axis=-1, keepdims=True) + NORM_EPS) * g


def _sigmoid(x):
    return 1.0 / (1.0 + jnp.exp(-x))


def _store(outs, *vals):
    for o_ref, v in zip(outs, vals):
        o_ref[...] = v.astype(o_ref.dtype)


_CAST_ROWS = 128
_DOT_ROWS = 704


def _mm_rows_body(*refs, needs_cast, n_extra, n_out, epilogue, row_chunk, first_col_init):
    n_pairs = len(needs_cast)
    a_refs = refs[0:2 * n_pairs:2]
    w_refs = refs[1:2 * n_pairs:2]
    pos = 2 * n_pairs
    extra = refs[pos:pos + n_extra]
    outs = refs[pos + n_extra:pos + n_extra + n_out]
    scratch = list(refs[pos + n_extra + n_out:])
    i, j = pl.program_id(0), pl.program_id(1)
    lhs_refs = []
    for a_ref, cast in zip(a_refs, needs_cast):
        if cast:
            ab_ref = scratch.pop(0)

            @pl.when(j == 0)
            def _(a_ref=a_ref, ab_ref=ab_ref):
                def rows(c, carry):
                    r0 = pl.multiple_of(c * _CAST_ROWS, _CAST_ROWS)
                    ab_ref[pl.ds(r0, _CAST_ROWS), :] = a_ref[pl.ds(r0, _CAST_ROWS), :].astype(BF16)
                    return carry
                lax.fori_loop(0, a_ref.shape[0] // _CAST_ROWS, rows, 0)

            lhs_refs.append(ab_ref)
        else:
            lhs_refs.append(a_ref)
    if first_col_init is not None:
        @pl.when(j == 0)
        def _():
            first_col_init(outs)

    wbs = [w_ref[...].astype(BF16) for w_ref in w_refs]
    for r0 in range(0, lhs_refs[0].shape[0], row_chunk):
        rows = slice(r0, r0 + row_chunk)
        accs = [_dot(lhs_ref[rows, :], wb) for lhs_ref, wb in zip(lhs_refs, wbs)]
        epilogue(accs, i, j, rows, a_refs, extra, outs)


def _matmul_rows(pairs, *, bm, bn, extras=(), extra_specs=(), out_shapes, out_specs=None, epilogue, name,
                 row_chunk=_DOT_ROWS, first_col_init=None):
    m = pairs[0][0].shape[0]
    n = pairs[0][1].shape[1]
    assert m % bm == 0 and n % bn == 0 and bm % _CAST_ROWS == 0 and bm % row_chunk == 0
    in_specs, operands, scratch, needs_cast = [], [], [], []
    for a, w in pairs:
        kdim = w.shape[0]
        assert a.shape == (m, kdim) and w.shape[1] == n
        in_specs += [pl.BlockSpec((bm, kdim), lambda i, j: (i, 0), pipeline_mode=pl.Buffered(1)),
                     pl.BlockSpec((kdim, bn), lambda i, j: (0, j))]
        operands += [a, w]
        needs_cast.append(a.dtype != BF16)
        if needs_cast[-1]:
            scratch.append(pltpu.VMEM((bm, kdim), BF16))
    if out_specs is None:
        out_specs = tuple(pl.BlockSpec((bm, bn), lambda i, j: (i, j)) for _ in out_shapes)
    body = functools.partial(_mm_rows_body, needs_cast=tuple(needs_cast), n_extra=len(extras),
                             n_out=len(out_shapes), epilogue=epilogue, row_chunk=row_chunk,
                             first_col_init=first_col_init)
    return pl.pallas_call(
        body,
        grid=(m // bm, n // bn),
        in_specs=in_specs + list(extra_specs),
        out_specs=tuple(out_specs),
        out_shape=tuple(out_shapes),
        scratch_shapes=scratch,
        compiler_params=pltpu.CompilerParams(dimension_semantics=("arbitrary", "arbitrary"),
                                             vmem_limit_bytes=VMEM_LIMIT),
        name=name,
    )(*operands, *extras)


_RESID_SLOTS = 2


def _mm_ksplit_body(a_ref, w_ref, r_hbm, o_ref, r_buf, r_sem, *, row_chunk, resid_rows):
    j, i, k = pl.program_id(0), pl.program_id(1), pl.program_id(2)
    bm, bn = o_ref.shape
    n_res = bm // resid_rows

    def r_copy(c):
        r0 = pl.multiple_of(i * bm + c * resid_rows, resid_rows)
        src = r_hbm.at[pl.ds(r0, resid_rows), pl.ds(pl.multiple_of(j * bn, bn), bn)]
        return pltpu.make_async_copy(src, r_buf.at[c % _RESID_SLOTS], r_sem.at[c % _RESID_SLOTS])

    @pl.when(k == 0)
    def _():
        for c in range(min(_RESID_SLOTS, n_res)):
            r_copy(c).start()
        o_ref[...] = jnp.zeros(o_ref.shape, F32)

    wb = w_ref[...].astype(BF16)
    for r0 in range(0, bm, row_chunk):
        rows = slice(r0, r0 + row_chunk)
        o_ref[rows, :] += _dot(a_ref[rows, :], wb)

    @pl.when(k == pl.num_programs(2) - 1)
    def _():
        for c in range(n_res):
            rows = slice(c * resid_rows, (c + 1) * resid_rows)
            r_copy(c).wait()
            o_ref[rows, :] += r_buf[c % _RESID_SLOTS]
            if c + _RESID_SLOTS < n_res:
                r_copy(c + _RESID_SLOTS).start()


def _matmul_ksplit(a, w, resid, *, bm, bn, bk, name, row_chunk=_DOT_ROWS, resid_rows=352):
    m, kdim = a.shape
    n = w.shape[1]
    assert m % bm == 0 and n % bn == 0 and kdim % bk == 0 and bm % row_chunk == 0 and a.dtype == BF16
    assert resid.shape == (m, n) and resid.dtype == F32 and bm % resid_rows == 0
    return pl.pallas_call(
        functools.partial(_mm_ksplit_body, row_chunk=row_chunk, resid_rows=resid_rows),
        grid=(n // bn, m // bm, kdim // bk),
        in_specs=[pl.BlockSpec((bm, bk), lambda j, i, k: (i, k)),
                  pl.BlockSpec((bk, bn), lambda j, i, k: (k, j)),
                  pl.BlockSpec(memory_space=pl.ANY)],
        out_specs=pl.BlockSpec((bm, bn), lambda j, i, k: (i, j)),
        out_shape=jax.ShapeDtypeStruct((m, n), F32),
        scratch_shapes=[pltpu.VMEM((_RESID_SLOTS, resid_rows, bn), F32),
                        pltpu.SemaphoreType.DMA((_RESID_SLOTS,))],
        compiler_params=pltpu.CompilerParams(dimension_semantics=("arbitrary", "arbitrary", "arbitrary"),
                                             vmem_limit_bytes=VMEM_LIMIT),
        name=name,
    )(a, w, resid)


def _rmsnorm_body(*refs, n_main):
    g_ref, o_ref = refs[-2:]
    if n_main is None:
        o_ref[...] = _rms(refs[0][...], g_ref[...]).astype(o_ref.dtype)
        return
    i = pl.program_id(0)

    @pl.when(i < n_main)
    def _():
        o_ref[...] = _rms(refs[0][...], g_ref[...]).astype(o_ref.dtype)

    @pl.when(i >= n_main)
    def _():
        o_ref[...] = _rms(refs[1][...], g_ref[...]).astype(o_ref.dtype)


def _rmsnorm_rows(x, g, *, x_tail=None, bm=256, name="rmsnorm"):
    m, d = x.shape
    assert m % bm == 0
    n_main = m // bm
    in_specs = [pl.BlockSpec((bm, d), lambda i: (jnp.minimum(i, n_main - 1), 0))]
    operands = [x]
    if x_tail is not None:
        assert x_tail.shape[0] % bm == 0
        in_specs.append(pl.BlockSpec((bm, d), lambda i: (jnp.maximum(i - n_main, 0), 0)))
        operands.append(x_tail)
        m += x_tail.shape[0]
    in_specs.append(pl.BlockSpec((1, d), lambda i: (0, 0)))
    return pl.pallas_call(
        functools.partial(_rmsnorm_body, n_main=None if x_tail is None else n_main),
        grid=(m // bm,),
        in_specs=in_specs,
        out_specs=pl.BlockSpec((bm, d), lambda i: (i, 0)),
        out_shape=jax.ShapeDtypeStruct((m, d), BF16),
        compiler_params=pltpu.CompilerParams(dimension_semantics=("arbitrary",)),
        name=name,
    )(*operands, g.reshape(1, d))


def _ld_rows(ref, start, dil, n=BAND):
    if dil == 1:
        return ref[pl.ds(start, n), :]
    return ref[pl.ds(start, n, stride=dil), :]


def _st_rows(ref, g, start, dil, val):
    if dil == 1:
        ref[g, pl.ds(start, BAND), :] = val
    else:
        ref[g, pl.ds(start, BAND, stride=dil), :] = val


def _attn_prompt_body(slope_ref, gq_ref, gk_ref,
                      q0, k0, v0, kp0, vp0, q1, k1, v1, kp1, vp1, q2, k2, v2, kp2, vp2,
                      o_ref, kn0, kn1, kn2,
                      qn_s, kn_s, vv_s, og_s, lse_s, *, sb, dils):
    first_block = pl.program_id(1) == 0
    groups = ((q0, k0, v0, kp0, vp0, kn0), (q1, k1, v1, kp1, vp1, kn1), (q2, k2, v2, kp2, vp2, kn2))
    ii = lax.broadcasted_iota(jnp.int32, (BAND, BAND), 0)
    jj = lax.broadcasted_iota(jnp.int32, (BAND, BAND), 1)
    dist_cur = (ii - jj).astype(F32)
    dist_prev = (BAND + ii - jj).astype(F32)
    ok_cur = jj <= ii
    ok_prev = jj >= ii
    neg_inf = jnp.float32(-jnp.inf)
    base = kn_s.shape[0] - sb

    for g, dil in enumerate(dils):
        q_ref, k_ref, v_ref, kp_ref, vp_ref, kn_out = groups[g]
        pr = BAND * dil
        gq = gq_ref[g:g + 1, :]
        gk = gk_ref[g:g + 1, :]
        slope = slope_ref[0, :, g * HEAD_DIM:(g + 1) * HEAD_DIM]
        qn_s[...] = _rms(q_ref[...], gq)
        kn = _rms(k_ref[...], gk)
        kn_out[...] = kn
        kn_s[pl.ds(base, sb), :] = kn
        kn_s[pl.ds(base - pr, pr), :] = _rms(kp_ref[...], gk)
        vv_s[pl.ds(base, sb), :] = v_ref[...]
        vv_s[pl.ds(base - pr, pr), :] = vp_ref[...]
        bias_cur = jnp.where(ok_cur, -slope * (dist_cur * dil), neg_inf)
        bias_prev = jnp.where(ok_prev, -slope * (dist_prev * dil), neg_inf)
        bias = jnp.concatenate([bias_prev, bias_cur], axis=1)
        bias_first = jnp.concatenate([jnp.where(first_block, neg_inf, bias_prev), bias_cur], axis=1)
        ones = jnp.ones((2 * BAND, HEAD_DIM), BF16)
        for r in range(dil):
            for s in range(sb // pr):
                c0 = s * pr + r
                qs = _ld_rows(qn_s, c0, dil).astype(BF16)
                kk = _ld_rows(kn_s, base + c0 - pr, dil, 2 * BAND).astype(BF16)
                vv = _ld_rows(vv_s, base + c0 - pr, dil, 2 * BAND).astype(BF16)
                lg = _dot_nt(qs, kk) * ATTN_SCALE + (bias_first if s == 0 else bias)
                mx = jnp.max(lg, axis=-1, keepdims=True)
                p = jnp.exp(lg - mx).astype(BF16)
                oe = _dot(p, jnp.concatenate([vv, ones], axis=1))
                den = oe[:, HEAD_DIM:]
                _st_rows(og_s, g, c0, dil, oe[:, :HEAD_DIM] / den)
                _st_rows(lse_s, g, c0, dil, mx + jnp.log(den))

    l0, l1, l2 = lse_s[0], lse_s[1], lse_s[2]
    mx = jnp.maximum(jnp.maximum(l0, l1), l2)
    w0, w1, w2 = jnp.exp(l0 - mx), jnp.exp(l1 - mx), jnp.exp(l2 - mx)
    o = (w0 * og_s[0] + w1 * og_s[1] + w2 * og_s[2]) / (w0 + w1 + w2)
    o_ref[...] = o.astype(o_ref.dtype)


def _alibi_slopes():
    n = N_GROUPS * A_HEADS
    e = jnp.arange(1, n + 1, dtype=F32)
    return jnp.exp2(-8.0 * e / n).reshape(N_GROUPS, A_HEADS)


def _attn_prompt(z, g_q, g_k, *, seq, sb, dils, out_rows=None, n_heads=A_HEADS, col_q=COL_Q, col_k=COL_K,
                 col_v=COL_V):
    out_rows = seq if out_rows is None else out_rows
    nb = seq // sb
    slopes = _alibi_slopes()
    slope_arr = jnp.broadcast_to(slopes.T[:, None, :, None], (n_heads, 1, N_GROUPS, HEAD_DIM))
    slope_arr = slope_arr.reshape(n_heads, 1, N_GROUPS * HEAD_DIM)
    a_width = n_heads * HEAD_DIM
    in_specs = [pl.BlockSpec((1, 1, N_GROUPS * HEAD_DIM), lambda h, i: (h, 0, 0)),
                pl.BlockSpec((N_GROUPS, HEAD_DIM), lambda h, i: (0, 0)),
                pl.BlockSpec((N_GROUPS, HEAD_DIM), lambda h, i: (0, 0))]
    operands = [slope_arr, g_q, g_k]
    max_pr = BAND * max(dils)
    for g, dil in enumerate(dils):
        pr = BAND * dil
        ratio = sb // pr
        cq = (col_q + g * a_width) // HEAD_DIM
        ck = (col_k + g * a_width) // HEAD_DIM
        cv = (col_v + g * a_width) // HEAD_DIM
        cur = lambda c: pl.BlockSpec((sb, HEAD_DIM), lambda h, i, c=c: (i, c + h))
        prev = lambda c: pl.BlockSpec((pr, HEAD_DIM),
                                      lambda h, i, c=c, ratio=ratio: (jnp.maximum(i * ratio - 1, 0), c + h))
        in_specs += [cur(cq), cur(ck), cur(cv), prev(ck), prev(cv)]
        operands += [z, z, z, z, z]
    out_block = pl.BlockSpec((sb, HEAD_DIM), lambda h, i: (i, h))
    out_shape = (jax.ShapeDtypeStruct((out_rows, a_width), BF16),) + tuple(
        jax.ShapeDtypeStruct((seq, a_width), F32) for _ in dils)
    body = functools.partial(_attn_prompt_body, sb=sb, dils=tuple(dils))
    return pl.pallas_call(
        body,
        grid=(n_heads, nb),
        in_specs=in_specs,
        out_specs=(out_block,) * 4,
        out_shape=out_shape,
        scratch_shapes=[pltpu.VMEM((sb, HEAD_DIM), F32),
                        pltpu.VMEM((sb + max_pr, HEAD_DIM), F32),
                        pltpu.VMEM((sb + max_pr, HEAD_DIM), F32),
                        pltpu.VMEM((N_GROUPS, sb, HEAD_DIM), F32),
                        pltpu.VMEM((N_GROUPS, sb, HEAD_DIM), F32)],
        compiler_params=pltpu.CompilerParams(dimension_semantics=("arbitrary", "arbitrary"),
                                             vmem_limit_bytes=VMEM_LIMIT),
        name="attn_prompt",
    )(*operands)


def _sample_problems(dil, t_new):
    n_prob = min(dil, t_new)
    return n_prob, t_new // n_prob


def _sample_bias_tables(groups, t_new, n_heads):
    n_soft = len(groups) * n_heads
    tabs_c, tabs_n = [], []
    for g, (window, dil) in enumerate(groups):
        n_prob, tok_per = _sample_problems(dil, t_new)
        slopes = 2.0 ** (-8.0 * (g * n_heads + np.arange(n_heads) + 1) / n_soft)
        rows = np.arange(tok_per * n_heads)
        i, hp = rows // n_heads, rows % n_heads
        cols = np.arange(BAND * n_heads)
        m, h = cols // n_heads, cols % n_heads
        dist = window + i[:, None] * n_prob - m[None, :] * dil
        ok = (h[None, :] == hp[:, None]) & (dist <= window) & (dist > 0) & (dist % dil == 0)
        tabs_c.append(np.where(ok, -slopes[hp][:, None] * dist, -np.inf).astype(np.float32))
        cols = np.arange(t_new * n_heads)
        s, h = cols // n_heads, cols % n_heads
        per_p = []
        for p in range(n_prob):
            dist = (p + i * n_prob)[:, None] - s[None, :]
            ok = (h[None, :] == hp[:, None]) & (dist >= 0) & (dist % dil == 0)
            per_p.append(np.where(ok, -slopes[hp][:, None] * dist, -np.inf).astype(np.float32))
        tabs_n.append(np.stack(per_p))
    return tabs_c, tabs_n


def _attn_sample_body(gq_ref, gk_ref, q_ref, k_ref, v_ref, c0_ref, c1_ref, c2_ref,
                      bc0, bc1, bc2, bn0, bn1, bn2, o_ref, kn_ref, og_s, lse_s, *, t_new, groups, n_heads):
    caches = (c0_ref, c1_ref, c2_ref)
    bias_c = (bc0, bc1, bc2)
    bias_n = (bn0, bn1, bn2)
    kv_rows = 2 * n_heads
    for g, (window, dil) in enumerate(groups):
        c_ref = caches[g]
        n_prob, tok_per = _sample_problems(dil, t_new)
        qn = _rms(q_ref[0, g], gq_ref[g:g + 1, :])
        kn = _rms(k_ref[0, g], gk_ref[g:g + 1, :])
        kn_ref[0, g] = kn
        knb = kn.astype(BF16)
        vnb = v_ref[0, g].astype(BF16)
        for p in range(n_prob):
            toks = [p + i * n_prob for i in range(tok_per)]
            parts = [qn[t * n_heads:(t + 1) * n_heads] for t in toks]
            qp = (parts[0] if tok_per == 1 else jnp.concatenate(parts, axis=0)).astype(BF16)
            kc = c_ref[:, p * kv_rows:p * kv_rows + n_heads, :].reshape(BAND * n_heads, HEAD_DIM).astype(BF16)
            vc = c_ref[:, p * kv_rows + n_heads:(p + 1) * kv_rows, :].reshape(BAND * n_heads, HEAD_DIM).astype(BF16)
            lc = _dot_nt(qp, kc) * ATTN_SCALE + bias_c[g][...]
            ln = _dot_nt(qp, knb) * ATTN_SCALE + bias_n[g][p]
            mx = jnp.maximum(jnp.max(lc, axis=-1, keepdims=True), jnp.max(ln, axis=-1, keepdims=True))
            pc = jnp.exp(lc - mx)
            pn = jnp.exp(ln - mx)
            ssum = jnp.sum(pc, axis=-1, keepdims=True) + jnp.sum(pn, axis=-1, keepdims=True)
            o = (_dot(pc.astype(BF16), vc) + _dot(pn.astype(BF16), vnb)) / ssum
            lse = jnp.broadcast_to(mx + jnp.log(ssum), o.shape)
            for i, t in enumerate(toks):
                og_s[g, t * n_heads:(t + 1) * n_heads, :] = o[i * n_heads:(i + 1) * n_heads]
                lse_s[g, t * n_heads:(t + 1) * n_heads, :] = lse[i * n_heads:(i + 1) * n_heads]
    l0, l1, l2 = lse_s[0], lse_s[1], lse_s[2]
    mx = jnp.maximum(jnp.maximum(l0, l1), l2)
    w0, w1, w2 = jnp.exp(l0 - mx), jnp.exp(l1 - mx), jnp.exp(l2 - mx)
    o_ref[0] = (w0 * og_s[0] + w1 * og_s[1] + w2 * og_s[2]) / (w0 + w1 + w2)


def _attn_sample(qs, ks, vs, caches, g_q, g_k, *, n_batch, t_new, groups, n_heads=A_HEADS):
    n_g = len(groups)
    rows = t_new * n_heads
    tabs_c, tabs_n = _sample_bias_tables(groups, t_new, n_heads)
    new_spec = lambda: pl.BlockSpec((1, n_g, rows, HEAD_DIM), lambda b: (b, 0, 0, 0))
    in_specs = [pl.BlockSpec((n_g, HEAD_DIM), lambda b: (0, 0)),
                pl.BlockSpec((n_g, HEAD_DIM), lambda b: (0, 0)),
                new_spec(), new_spec(), new_spec()]
    operands = [g_q, g_k, qs, ks, vs]
    for c, (window, dil) in zip(caches, groups):
        assert c.shape[1] == window and window == BAND * dil
        n_prob, _ = _sample_problems(dil, t_new)
        c3 = c.reshape(n_batch * BAND, dil * 2 * n_heads, HEAD_DIM)
        in_specs.append(pl.BlockSpec((BAND, n_prob * 2 * n_heads, HEAD_DIM), lambda b: (b, 0, 0)))
        operands.append(c3)
    for tab in tabs_c:
        in_specs.append(pl.BlockSpec(tab.shape, lambda b: (0, 0)))
        operands.append(jnp.asarray(tab))
    for tab in tabs_n:
        in_specs.append(pl.BlockSpec(tab.shape, lambda b: (0, 0, 0)))
        operands.append(jnp.asarray(tab))
    body = functools.partial(_attn_sample_body, t_new=t_new, groups=tuple(groups), n_heads=n_heads)
    return pl.pallas_call(
        body,
        grid=(n_batch,),
        in_specs=in_specs,
        out_specs=(pl.BlockSpec((1, rows, HEAD_DIM), lambda b: (b, 0, 0)), new_spec()),
        out_shape=(jax.ShapeDtypeStruct((n_batch, rows, HEAD_DIM), F32),
                   jax.ShapeDtypeStruct((n_batch, n_g, rows, HEAD_DIM), F32)),
        scratch_shapes=[pltpu.VMEM((n_g, rows, HEAD_DIM), F32), pltpu.VMEM((n_g, rows, HEAD_DIM), F32)],
        compiler_params=pltpu.CompilerParams(dimension_semantics=("arbitrary",),
                                             vmem_limit_bytes=VMEM_LIMIT),
        name="attn_sample",
    )(*operands)


def _split3(x):
    hi = x.astype(BF16)
    r1 = x - hi.astype(F32)
    mid = r1.astype(BF16)
    lo = (r1 - mid.astype(F32)).astype(BF16)
    return hi, mid, lo


def _hgrn_heads(hq, hf, hi, hog, lb, g_out, sts, *, tb, n_valid=None):
    c = HG_CHUNK
    nh = len(sts)
    sts = list(sts)
    q = hq * _sigmoid(hq) * (HG_KDIM ** -0.5)
    gate = lb + (1.0 - lb) * _sigmoid(hf)
    log_g = jnp.log(gate)
    k = (1.0 - lb) * _sigmoid(-hf)
    if n_valid is not None:
        rows = lax.broadcasted_iota(jnp.int32, hq.shape, 0)
        log_g = jnp.where(rows < n_valid, log_g, 0.0)
        k = jnp.where(rows < n_valid, k, 0.0)
        q = jnp.where(rows < n_valid, q, 0.0)
    row = lax.broadcasted_iota(jnp.int32, (tb, tb), 0)
    col = lax.broadcasted_iota(jnp.int32, (tb, tb), 1)
    same = (row // c) == (col // c)
    tri_ok = same & (col <= row)
    tri = jnp.where(tri_ok, 1.0, 0.0).astype(BF16)
    blk = jnp.where(same, 1.0, 0.0).astype(BF16)
    p_hi, p_mid, p_lo = _split3(log_g)
    b = _dot(tri, p_hi) + _dot(tri, p_mid) + _dot(tri, p_lo)
    b_last = _dot(blk, p_hi) + _dot(blk, p_mid) + _dot(blk, p_lo)
    q_dec = (q * jnp.exp(b)).astype(BF16)
    k_dec = (k * jnp.exp(-b)).astype(BF16)
    k_end = (k * jnp.exp(b_last - b)).astype(BF16)
    decay = jnp.exp(b_last)
    vb = hi.astype(BF16)
    head = lambda x, h: x[:, h * HG_KDIM:(h + 1) * HG_KDIM]
    o_intra = []
    for h in range(nh):
        a = jnp.where(tri_ok, _dot_nt(head(q_dec, h), head(k_dec, h)), 0.0)
        o_intra.append(_dot(a.astype(BF16), head(vb, h)))
    parts = [[] for _ in range(nh)]
    for ci in range(tb // c):
        sl = slice(ci * c, (ci + 1) * c)
        for h in range(nh):
            parts[h].append(_dot_nt(head(q_dec, h)[sl], sts[h].astype(BF16)))
            sts[h] = sts[h] * head(decay, h)[ci * c:ci * c + 1, :] + _dot_tn(head(vb, h)[sl], head(k_end, h)[sl])
    outs = []
    for h in range(nh):
        o = o_intra[h] + (jnp.concatenate(parts[h], axis=0) if len(parts[h]) > 1 else parts[h][0])
        outs.append(_rms(o, g_out) * _sigmoid(head(hog, h)))
    return outs, sts


def _hgrn_prompt_body(hq_ref, hf_ref, hi_ref, hog_ref, lb_ref, go_ref, o_ref, s_ref, st_s, *, tb, nt, hb):
    t = pl.program_id(1)

    @pl.when(t == 0)
    def _():
        st_s[...] = jnp.zeros_like(st_s)

    outs, sts = _hgrn_heads(hq_ref[...], hf_ref[...], hi_ref[...], hog_ref[...], lb_ref[0], go_ref[...],
                            [st_s[hh] for hh in range(hb)], tb=tb)
    for hh in range(hb):
        o_ref[:, hh * HG_VDIM:(hh + 1) * HG_VDIM] = outs[hh].astype(o_ref.dtype)
        st_s[hh] = sts[hh]

    @pl.when(t == nt - 1)
    def _():
        for hh in range(hb):
            s_ref[hh] = sts[hh].T


def _hgrn_prompt(z, lb, g_out, *, seq, tb, hb, out_rows=None, n_heads=HG_HEADS, col_hq=COL_HQ, col_hf=COL_HF,
                 col_hi=COL_HI, col_hog=COL_HOG):
    out_rows = seq if out_rows is None else out_rows
    nt = seq // tb
    bw = hb * HG_KDIM
    assert all(c0 % bw == 0 for c0 in (col_hq, col_hf, col_hi, col_hog))
    blk = lambda c0: pl.BlockSpec((tb, bw), lambda h, t, c=c0 // bw: (t, c + h))
    body = functools.partial(_hgrn_prompt_body, tb=tb, nt=nt, hb=hb)
    return pl.pallas_call(
        body,
        grid=(n_heads // hb, nt),
        in_specs=[blk(col_hq), blk(col_hf), blk(col_hi), blk(col_hog),
                  pl.BlockSpec((1, 1, bw), lambda h, t: (h, 0, 0)),
                  pl.BlockSpec((1, HG_VDIM), lambda h, t: (0, 0))],
        out_specs=(pl.BlockSpec((tb, bw), lambda h, t: (t, h)),
                   pl.BlockSpec((hb, HG_KDIM, HG_VDIM), lambda h, t: (h, 0, 0))),
        out_shape=(jax.ShapeDtypeStruct((out_rows, n_heads * HG_VDIM), BF16),
                   jax.ShapeDtypeStruct((n_heads, HG_KDIM, HG_VDIM), F32)),
        scratch_shapes=[pltpu.VMEM((hb, HG_VDIM, HG_KDIM), F32)],
        compiler_params=pltpu.CompilerParams(dimension_semantics=("arbitrary", "arbitrary"),
                                             vmem_limit_bytes=VMEM_LIMIT),
        name="hgrn_prompt",
    )(z, z, z, z, lb.reshape(n_heads // hb, 1, bw), g_out.reshape(1, HG_VDIM))


def _hgrn_sample_body(hq_ref, hf_ref, hi_ref, hog_ref, lb_ref, go_ref, s0_ref, o_ref, s_ref, *, t_new, n_heads):
    pad = jnp.zeros((HG_CHUNK - t_new, n_heads * HG_KDIM), F32)
    ext = lambda ref: jnp.concatenate([ref[...], pad], axis=0)
    outs, sts = _hgrn_heads(ext(hq_ref), ext(hf_ref), ext(hi_ref), ext(hog_ref), lb_ref[...], go_ref[...],
                            [s0_ref[0, h].T for h in range(n_heads)], tb=HG_CHUNK, n_valid=t_new)
    for h in range(n_heads):
        o_ref[:, h * HG_VDIM:(h + 1) * HG_VDIM] = outs[h][:t_new]
        s_ref[0, h] = sts[h].T


def _hgrn_sample(hq, hf, hi, hog, lb, g_out, s0, *, n_batch, t_new, n_heads=HG_HEADS):
    width = n_heads * HG_KDIM
    row = lambda: pl.BlockSpec((t_new, width), lambda b: (b, 0))
    st = lambda: pl.BlockSpec((1, n_heads, HG_KDIM, HG_VDIM), lambda b: (b, 0, 0, 0))
    body = functools.partial(_hgrn_sample_body, t_new=t_new, n_heads=n_heads)
    return pl.pallas_call(
        body,
        grid=(n_batch,),
        in_specs=[row(), row(), row(), row(),
                  pl.BlockSpec((1, width), lambda b: (0, 0)),
                  pl.BlockSpec((1, HG_VDIM), lambda b: (0, 0)),
                  st()],
        out_specs=(row(), st()),
        out_shape=(jax.ShapeDtypeStruct((n_batch * t_new, width), F32),
                   jax.ShapeDtypeStruct(s0.shape, F32)),
        compiler_params=pltpu.CompilerParams(dimension_semantics=("arbitrary",),
                                             vmem_limit_bytes=VMEM_LIMIT),
        name="hgrn_sample",
    )(hq, hf, hi, hog, lb.reshape(1, width), g_out.reshape(1, HG_VDIM), s0)


def _tile_spec(bm, bn, col0=0):
    assert col0 % bn == 0
    cb = col0 // bn
    return pl.BlockSpec((bm, bn), lambda i, j: (i, cb + j))


def kernel(x_prompt, x_sample, cache_kv_w128, cache_kv_w512, cache_kv_w2048, state_hgrn, p_prompt, p_sample,
           g_mix, w_in, g_q, g_k, hg_lb_raw, g_hg_out, w_up_attn, w_up_hgrn, w_out, g_ffn, w_ff_up, w_ff_down,
           w_ple, w_ple_gate):
    bm, bn = 2816, 256
    bm_ple = 1408
    bm_res = 2816
    xp, xs = x_prompt.reshape(SEQ, D_MODEL), x_sample.reshape(N_SAMPLE, D_MODEL)
    pp, ps = p_prompt.reshape(SEQ, PLE_DIM), p_sample.reshape(N_SAMPLE, PLE_DIM)
    lb = jnp.cumsum(jax.nn.softmax(hg_lb_raw.astype(F32), axis=0), axis=0)[0]
    rows = lambda width, dt: jax.ShapeDtypeStruct((M_ALL, width), dt)

    def plain(accs, i, j, rows_sl, a_refs, extra, outs):
        outs[0][rows_sl, :] = accs[0]

    n_mix = _rmsnorm_rows(xp, g_mix[0], x_tail=xs, name="norm_mix")
    z, = _matmul_rows([(n_mix, w_in[0])], bm=bm, bn=512, out_shapes=(rows(IN_WIDTH, F32),), epilogue=plain,
                      name="in_proj")

    o_attn, kn0, kn1, kn2 = _attn_prompt(z, g_q[0], g_k[0], seq=SEQ, sb=2048, out_rows=M_ALL,
                                         dils=tuple(d for _, d in DIL_GROUPS))
    zs = z[SEQ:]
    def by_group(col0):
        a = zs[:, col0:col0 + N_GROUPS * A_WIDTH].reshape(DEC_BATCH, DEC_SEQ, N_GROUPS, A_HEADS, HEAD_DIM)
        return a.transpose(0, 2, 1, 3, 4).reshape(DEC_BATCH, N_GROUPS, DEC_SEQ * A_HEADS, HEAD_DIM)

    caches = [c[0] for c in (cache_kv_w128, cache_kv_w512, cache_kv_w2048)]
    o_attn_s, kn_s = _attn_sample(by_group(COL_Q), by_group(COL_K), by_group(COL_V), caches,
                                  g_q[0], g_k[0], n_batch=DEC_BATCH, t_new=DEC_SEQ, groups=DIL_GROUPS)
    o_attn = lax.dynamic_update_slice(o_attn, o_attn_s.reshape(N_SAMPLE, A_WIDTH).astype(BF16), (SEQ, 0))

    o_hg, st_p = _hgrn_prompt(z, lb, g_hg_out[0], seq=SEQ, tb=256, hb=8, out_rows=M_ALL)
    o_hg_s, st_s = _hgrn_sample(zs[:, COL_HQ:COL_HF], zs[:, COL_HF:COL_HI], zs[:, COL_HI:COL_HOG],
                                zs[:, COL_HOG:COL_GA], lb, g_hg_out[0], state_hgrn[0],
                                n_batch=DEC_BATCH, t_new=DEC_SEQ)
    o_hg = lax.dynamic_update_slice(o_hg, o_hg_s.astype(BF16), (SEQ, 0))

    def merge_epilogue(accs, i, j, rows_sl, a_refs, extra, outs):
        ga, gb = extra[0][rows_sl, :], extra[1][rows_sl, :]
        outs[0][rows_sl, :] = (_sigmoid(ga) * accs[0] + _sigmoid(gb) * accs[1]).astype(BF16)

    def out_proj_epilogue(accs, i, j, rows_sl, a_refs, extra, outs):
        xp_ref, xs_ref, g_ref = extra
        x1_ref, x1g_ref, ssq_ref = outs

        def emit(rows, x1, ssq_before):
            x1_ref[rows, :] = x1
            x1g_ref[rows, :] = (x1 * g_ref[...]).astype(BF16)
            ssq_ref[rows, :] = ssq_before + jnp.sum(x1 * x1, axis=-1, keepdims=True)

        ssq_before = ssq_ref[rows_sl, :]
        emit(rows_sl, xp_ref[rows_sl, :] + accs[0], ssq_before)
        if rows_sl.stop == bm_res:
            t0 = bm_res - N_SAMPLE - rows_sl.start

            @pl.when(i == pl.num_programs(0) - 1)
            def _():
                emit(slice(bm_res - N_SAMPLE, bm_res), xs_ref[...] + accs[0][t0:, :], ssq_before[t0:, :])

    merged, = _matmul_rows([(o_attn, w_up_attn[0]), (o_hg, w_up_hgrn[0])], bm=bm, bn=bn, extras=(z, z),
                           extra_specs=(_tile_spec(bm, bn, COL_GA), _tile_spec(bm, bn, COL_GB)),
                           out_shapes=(rows(D_MODEL, BF16),), epilogue=merge_epilogue, row_chunk=352,
                           name="up_merge")
    def zero_ssq(outs):
        outs[2][...] = jnp.zeros(outs[2].shape, F32)

    x1, x1g, ssq = _matmul_rows(
        [(merged, w_out[0])], bm=bm_res, bn=bn, extras=(xp, xs, g_ffn),
        extra_specs=(_tile_spec(bm_res, bn), pl.BlockSpec((N_SAMPLE, bn), lambda i, j: (0, j)),
                     pl.BlockSpec((1, bn), lambda i, j: (0, j))),
        out_shapes=(rows(D_MODEL, F32), rows(D_MODEL, BF16), rows(LANES, F32)),
        out_specs=(_tile_spec(bm_res, bn), _tile_spec(bm_res, bn),
                   pl.BlockSpec((bm_res, LANES), lambda i, j: (i, 0))),
        epilogue=out_proj_epilogue, first_col_init=zero_ssq, name="out_proj")

    def ffn_up_epilogue(accs, i, j, rows_sl, a_refs, extra, outs):
        inv_rms = lax.rsqrt(extra[0][rows_sl, 0:1] * (1.0 / D_MODEL) + NORM_EPS)
        outs[0][rows_sl, :] = jnp.square(jnp.maximum(accs[0] * inv_rms, 0.0)).astype(BF16)

    hid, = _matmul_rows([(x1g, w_ff_up[0])], bm=bm, bn=512, extras=(ssq,),
                        extra_specs=(pl.BlockSpec((bm, LANES), lambda i, j: (i, 0)),),
                        out_shapes=(rows(FFN_HIDDEN, BF16),), epilogue=ffn_up_epilogue, name="ffn_up")
    x2 = _matmul_ksplit(hid, w_ff_down[0], x1, bm=1408, bn=2048, bk=1024, name="ffn_down")

    n_ple_blocks = M_ALL // bm_ple

    def ple_epilogue(accs, i, j, rows_sl, a_refs, extra, outs):
        pp_ref, ps_ref, wp_ref = extra
        yp_ref, ys_ref = outs
        x2_tile = a_refs[0][rows_sl, pl.ds(pl.multiple_of(j * bn, bn), bn)]
        gate = _sigmoid(accs[0])
        wp = wp_ref[...].astype(BF16)
        yp_ref[rows_sl, :] = x2_tile + gate * _dot(pp_ref[rows_sl, :].astype(BF16), wp)
        if rows_sl.stop == bm_ple:
            s0 = bm_ple - N_SAMPLE - rows_sl.start

            @pl.when(i == n_ple_blocks - 1)
            def _():
                ys_ref[...] = x2_tile[s0:, :] + gate[s0:, :] * _dot(ps_ref[...].astype(BF16), wp)

    y_p, y_s = _matmul_rows(
        [(x2, w_ple_gate[0])], bm=bm_ple, bn=bn, extras=(pp, ps, w_ple[0]),
        extra_specs=(pl.BlockSpec((bm_ple, PLE_DIM), lambda i, j: (i, 0)),
                     pl.BlockSpec((N_SAMPLE, PLE_DIM), lambda i, j: (0, 0)),
                     pl.BlockSpec((PLE_DIM, bn), lambda i, j: (0, j))),
        out_shapes=(jax.ShapeDtypeStruct((SEQ, D_MODEL), F32), jax.ShapeDtypeStruct((N_SAMPLE, D_MODEL), F32)),
        out_specs=(pl.BlockSpec((bm_ple, bn), lambda i, j: (i, j)),
                   pl.BlockSpec((N_SAMPLE, bn), lambda i, j: (0, jnp.where(i == n_ple_blocks - 1, j, 0)))),
        epilogue=ple_epilogue, name="ple")

    y_prompt = y_p.reshape(1, SEQ, D_MODEL)
    y_sample = y_s.reshape(DEC_BATCH, DEC_SEQ, D_MODEL)
    kv_p, kv_s = [], []
    for g, (window, _) in enumerate(DIL_GROUPS):
        length = min(window, SEQ)
        kn_g = (kn0, kn1, kn2)[g][SEQ - length:].reshape(length, A_HEADS, HEAD_DIM)
        v_g = z[SEQ - length:SEQ, COL_V + g * A_WIDTH:COL_V + (g + 1) * A_WIDTH].reshape(length, A_HEADS, HEAD_DIM)
        kv_p.append(jnp.stack([kn_g, v_g], axis=1)[None, None])
        ks_g = kn_s[:, g].reshape(DEC_BATCH, DEC_SEQ, A_HEADS, HEAD_DIM)
        vs_g = zs[:, COL_V + g * A_WIDTH:COL_V + (g + 1) * A_WIDTH].reshape(DEC_BATCH, DEC_SEQ, A_HEADS, HEAD_DIM)
        kv_s.append(jnp.stack([ks_g, vs_g], axis=2)[None])
    return (y_prompt, y_sample, kv_p[0], kv_p[1], kv_p[2], st_p[None, None],
            kv_s[0], kv_s[1], kv_s[2], st_s[None])
```

```python
import functools

import numpy as np
import jax
import jax.numpy as jnp
from jax import lax
from jax.experimental import pallas as pl
from jax.experimental.pallas import tpu as pltpu

F32 = jnp.float32
BF16 = jnp.bfloat16

D_MODEL = 4096
SEQ = 8192
DEC_BATCH = 32
DEC_SEQ = 8
N_SAMPLE = DEC_BATCH * DEC_SEQ
M_ALL = SEQ + N_SAMPLE
DIL_GROUPS = ((128, 1), (512, 4), (2048, 16))
N_GROUPS = 3
A_HEADS = 8
HEAD_DIM = 128
A_WIDTH = A_HEADS * HEAD_DIM
BAND = 128
HG_HEADS = 16
HG_KDIM = 128
HG_VDIM = 128
HG_WIDTH = HG_HEADS * HG_VDIM
HG_CHUNK = 32
FFN_HIDDEN = 4 * D_MODEL
PLE_DIM = 256
NORM_EPS = 1e-6
ATTN_SCALE = HEAD_DIM ** -0.5

COL_Q = 0
COL_K = COL_Q + N_GROUPS * A_WIDTH
COL_V = COL_K + N_GROUPS * A_WIDTH
COL_HQ = COL_V + N_GROUPS * A_WIDTH
COL_HF = COL_HQ + HG_HEADS * HG_KDIM
COL_HI = COL_HF + HG_HEADS * HG_KDIM
COL_HOG = COL_HI + HG_WIDTH
COL_GA = COL_HOG + HG_WIDTH
COL_GB = COL_GA + D_MODEL
IN_WIDTH = COL_GB + D_MODEL

VMEM_LIMIT = 56 * 1024 * 1024
LANES = 128


def _dot(a, b):
    return jnp.dot(a, b, preferred_element_type=F32)


def _dot_nt(a, b):
    return lax.dot_general(a, b, (((1,), (1,)), ((), ())), preferred_element_type=F32)


def _dot_tn(a, b):
    return lax.dot_general(a, b, (((0,), (0,)), ((), ())), preferred_element_type=F32)


def _rms(x, g):
    return x * lax.rsqrt(jnp.mean(x * x, axis=-1, keepdims=True) + NORM_EPS) * g


def _sigmoid(x):
    return 1.0 / (1.0 + jnp.exp(-x))


_CAST_ROWS = 128
_DOT_ROWS = 704


def _mm_rows_body(*refs, needs_cast, n_extra, n_out, epilogue, row_chunk, first_col_init):
    n_pairs = len(needs_cast)
    a_refs = refs[0:2 * n_pairs:2]
    w_refs = refs[1:2 * n_pairs:2]
    pos = 2 * n_pairs
    extra = refs[pos:pos + n_extra]
    outs = refs[pos + n_extra:pos + n_extra + n_out]
    scratch = list(refs[pos + n_extra + n_out:])
    i, j = pl.program_id(0), pl.program_id(1)
    lhs_refs = []
    for a_ref, cast in zip(a_refs, needs_cast):
        if cast:
            ab_ref = scratch.pop(0)

            @pl.when(j == 0)
            def _(a_ref=a_ref, ab_ref=ab_ref):
                def rows(c, carry):
                    r0 = pl.multiple_of(c * _CAST_ROWS, _CAST_ROWS)
                    ab_ref[pl.ds(r0, _CAST_ROWS), :] = a_ref[pl.ds(r0, _CAST_ROWS), :].astype(BF16)
                    return carry
                lax.fori_loop(0, a_ref.shape[0] // _CAST_ROWS, rows, 0)

            lhs_refs.append(ab_ref)
        else:
            lhs_refs.append(a_ref)
    if first_col_init is not None:
        @pl.when(j == 0)
        def _():
            first_col_init(outs)

    wbs = [w_ref[...].astype(BF16) for w_ref in w_refs]
    for r0 in range(0, lhs_refs[0].shape[0], row_chunk):
        rows = slice(r0, r0 + row_chunk)
        accs = [_dot(lhs_ref[rows, :], wb) for lhs_ref, wb in zip(lhs_refs, wbs)]
        epilogue(accs, i, j, rows, a_refs, extra, outs)


def _matmul_rows(pairs, *, bm, bn, extras=(), extra_specs=(), out_shapes, out_specs=None, epilogue, name,
                 row_chunk=_DOT_ROWS, first_col_init=None):
    m = pairs[0][0].shape[0]
    n = pairs[0][1].shape[1]
    assert m % bm == 0 and n % bn == 0 and bm % _CAST_ROWS == 0 and bm % row_chunk == 0
    in_specs, operands, scratch, needs_cast = [], [], [], []
    for a, w in pairs:
        kdim = w.shape[0]
        assert a.shape == (m, kdim) and w.shape[1] == n
        in_specs += [pl.BlockSpec((bm, kdim), lambda i, j: (i, 0), pipeline_mode=pl.Buffered(1)),
                     pl.BlockSpec((kdim, bn), lambda i, j: (0, j))]
        operands += [a, w]
        needs_cast.append(a.dtype != BF16)
        if needs_cast[-1]:
            scratch.append(pltpu.VMEM((bm, kdim), BF16))
    if out_specs is None:
        out_specs = tuple(pl.BlockSpec((bm, bn), lambda i, j: (i, j)) for _ in out_shapes)
    body = functools.partial(_mm_rows_body, needs_cast=tuple(needs_cast), n_extra=len(extras),
                             n_out=len(out_shapes), epilogue=epilogue, row_chunk=row_chunk,
                             first_col_init=first_col_init)
    return pl.pallas_call(
        body,
        grid=(m // bm, n // bn),
        in_specs=in_specs + list(extra_specs),
        out_specs=tuple(out_specs),
        out_shape=tuple(out_shapes),
        scratch_shapes=scratch,
        compiler_params=pltpu.CompilerParams(dimension_semantics=("arbitrary", "arbitrary"),
                                             vmem_limit_bytes=VMEM_LIMIT),
        name=name,
    )(*operands, *extras)


_RESID_SLOTS = 2


def _mm_ksplit_body(a_ref, w_ref, r_hbm, o_ref, r_buf, r_sem, *, row_chunk, resid_rows):
    j, i, k = pl.program_id(0), pl.program_id(1), pl.program_id(2)
    bm, bn = o_ref.shape
    n_res = bm // resid_rows

    def r_copy(c):
        r0 = pl.multiple_of(i * bm + c * resid_rows, resid_rows)
        src = r_hbm.at[pl.ds(r0, resid_rows), pl.ds(pl.multiple_of(j * bn, bn), bn)]
        return pltpu.make_async_copy(src, r_buf.at[c % _RESID_SLOTS], r_sem.at[c % _RESID_SLOTS])

    @pl.when(k == 0)
    def _():
        for c in range(min(_RESID_SLOTS, n_res)):
            r_copy(c).start()

    def k_step(first):
        wb = w_ref[...].astype(BF16)
        for r0 in range(0, bm, row_chunk):
            rows = slice(r0, r0 + row_chunk)
            part = _dot(a_ref[rows, :], wb)
            if first:
                o_ref[rows, :] = part
            else:
                o_ref[rows, :] += part

    @pl.when(k == 0)
    def _():
        k_step(True)

    @pl.when(k > 0)
    def _():
        k_step(False)

    @pl.when(k == pl.num_programs(2) - 1)
    def _():
        for c in range(n_res):
            rows = slice(c * resid_rows, (c + 1) * resid_rows)
            r_copy(c).wait()
            o_ref[rows, :] += r_buf[c % _RESID_SLOTS]
            if c + _RESID_SLOTS < n_res:
                r_copy(c + _RESID_SLOTS).start()


def _matmul_ksplit(a, w, resid, *, bm, bn, bk, name, row_chunk=_DOT_ROWS, resid_rows=352):
    m, kdim = a.shape
    n = w.shape[1]
    assert m % bm == 0 and n % bn == 0 and kdim % bk == 0 and bm % row_chunk == 0 and a.dtype == BF16
    assert resid.shape == (m, n) and resid.dtype == F32 and bm % resid_rows == 0
    return pl.pallas_call(
        functools.partial(_mm_ksplit_body, row_chunk=row_chunk, resid_rows=resid_rows),
        grid=(n // bn, m // bm, kdim // bk),
        in_specs=[pl.BlockSpec((bm, bk), lambda j, i, k: (i, k)),
                  pl.BlockSpec((bk, bn), lambda j, i, k: (k, j)),
                  pl.BlockSpec(memory_space=pl.ANY)],
        out_specs=pl.BlockSpec((bm, bn), lambda j, i, k: (i, j)),
        out_shape=jax.ShapeDtypeStruct((m, n), F32),
        scratch_shapes=[pltpu.VMEM((_RESID_SLOTS, resid_rows, bn), F32),
                        pltpu.SemaphoreType.DMA((_RESID_SLOTS,))],
        compiler_params=pltpu.CompilerParams(dimension_semantics=("arbitrary", "arbitrary", "arbitrary"),
                                             vmem_limit_bytes=VMEM_LIMIT),
        name=name,
    )(a, w, resid)


def _rmsnorm_body(*refs, n_main):
    g_ref, o_ref = refs[-2:]
    if n_main is None:
        o_ref[...] = _rms(refs[0][...], g_ref[...]).astype(o_ref.dtype)
        return
    i = pl.program_id(0)

    @pl.when(i < n_main)
    def _():
        o_ref[...] = _rms(refs[0][...], g_ref[...]).astype(o_ref.dtype)

    @pl.when(i >= n_main)
    def _():
        o_ref[...] = _rms(refs[1][...], g_ref[...]).astype(o_ref.dtype)


def _rmsnorm_rows(x, g, *, x_tail=None, bm=256, name="rmsnorm"):
    m, d = x.shape
    assert m % bm == 0
    n_main = m // bm
    in_specs = [pl.BlockSpec((bm, d), lambda i: (jnp.minimum(i, n_main - 1), 0))]
    operands = [x]
    if x_tail is not None:
        assert x_tail.shape[0] % bm == 0
        in_specs.append(pl.BlockSpec((bm, d), lambda i: (jnp.maximum(i - n_main, 0), 0)))
        operands.append(x_tail)
        m += x_tail.shape[0]
    in_specs.append(pl.BlockSpec((1, d), lambda i: (0, 0)))
    return pl.pallas_call(
        functools.partial(_rmsnorm_body, n_main=None if x_tail is None else n_main),
        grid=(m // bm,),
        in_specs=in_specs,
        out_specs=pl.BlockSpec((bm, d), lambda i: (i, 0)),
        out_shape=jax.ShapeDtypeStruct((m, d), BF16),
        compiler_params=pltpu.CompilerParams(dimension_semantics=("arbitrary",)),
        name=name,
    )(*operands, g.reshape(1, d))


def _ld_rows(ref, start, dil, n=BAND):
    if dil == 1:
        return ref[pl.ds(start, n), :]
    return ref[pl.ds(start, n, stride=dil), :]


def _st_rows(ref, g, start, dil, val):
    if dil == 1:
        ref[g, pl.ds(start, BAND), :] = val
    else:
        ref[g, pl.ds(start, BAND, stride=dil), :] = val


def _attn_prompt_body(slope_ref, gq_ref, gk_ref,
                      q0, k0, v0, kp0, vp0, q1, k1, v1, kp1, vp1, q2, k2, v2, kp2, vp2,
                      o_ref, kn0, kn1, kn2,
                      qn_s, kn_s, vv_s, og_s, lse_s, *, sb, dils):
    first_block = pl.program_id(1) == 0
    groups = ((q0, k0, v0, kp0, vp0, kn0), (q1, k1, v1, kp1, vp1, kn1), (q2, k2, v2, kp2, vp2, kn2))
    ii = lax.broadcasted_iota(jnp.int32, (BAND, BAND), 0)
    jj = lax.broadcasted_iota(jnp.int32, (BAND, BAND), 1)
    dist_cur = (ii - jj).astype(F32)
    dist_prev = (BAND + ii - jj).astype(F32)
    ok_cur = jj <= ii
    ok_prev = jj >= ii
    neg_inf = jnp.float32(-jnp.inf)
    base = kn_s.shape[0] - sb

    for g, dil in enumerate(dils):
        q_ref, k_ref, v_ref, kp_ref, vp_ref, kn_out = groups[g]
        pr = BAND * dil
        gq = gq_ref[g:g + 1, :]
        gk = gk_ref[g:g + 1, :]
        slope = slope_ref[0, :, g * HEAD_DIM:(g + 1) * HEAD_DIM]
        qn_s[...] = _rms(q_ref[...], gq)
        kn = _rms(k_ref[...], gk)
        kn_out[...] = kn
        kn_s[pl.ds(base, sb), :] = kn
        kn_s[pl.ds(base - pr, pr), :] = _rms(kp_ref[...], gk)
        vv_s[pl.ds(base, sb), :] = v_ref[...]
        vv_s[pl.ds(base - pr, pr), :] = vp_ref[...]
        bias_cur = jnp.where(ok_cur, -slope * (dist_cur * dil), neg_inf)
        bias_prev = jnp.where(ok_prev, -slope * (dist_prev * dil), neg_inf)
        bias = jnp.concatenate([bias_prev, bias_cur], axis=1)
        bias_first = jnp.concatenate([jnp.where(first_block, neg_inf, bias_prev), bias_cur], axis=1)
        ones = jnp.ones((2 * BAND, HEAD_DIM), BF16)
        for r in range(dil):
            for s in range(sb // pr):
                c0 = s * pr + r
                qs = _ld_rows(qn_s, c0, dil).astype(BF16)
                kk = _ld_rows(kn_s, base + c0 - pr, dil, 2 * BAND).astype(BF16)
                vv = _ld_rows(vv_s, base + c0 - pr, dil, 2 * BAND).astype(BF16)
                lg = _dot_nt(qs, kk) * ATTN_SCALE + (bias_first if s == 0 else bias)
                mx = jnp.max(lg, axis=-1, keepdims=True)
                p = jnp.exp(lg - mx).astype(BF16)
                oe = _dot(p, jnp.concatenate([vv, ones], axis=1))
                den = oe[:, HEAD_DIM:]
                _st_rows(og_s, g, c0, dil, oe[:, :HEAD_DIM] / den)
                _st_rows(lse_s, g, c0, dil, mx + jnp.log(den))

    l0, l1, l2 = lse_s[0], lse_s[1], lse_s[2]
    mx = jnp.maximum(jnp.maximum(l0, l1), l2)
    w0, w1, w2 = jnp.exp(l0 - mx), jnp.exp(l1 - mx), jnp.exp(l2 - mx)
    o = (w0 * og_s[0] + w1 * og_s[1] + w2 * og_s[2]) / (w0 + w1 + w2)
    o_ref[...] = o.astype(o_ref.dtype)


def _alibi_slopes():
    n = N_GROUPS * A_HEADS
    e = jnp.arange(1, n + 1, dtype=F32)
    return jnp.exp2(-8.0 * e / n).reshape(N_GROUPS, A_HEADS)


def _attn_prompt(z, g_q, g_k, *, seq, sb, dils, out_rows=None, n_heads=A_HEADS, col_q=COL_Q, col_k=COL_K,
                 col_v=COL_V):
    out_rows = seq if out_rows is None else out_rows
    nb = seq // sb
    slopes = _alibi_slopes()
    slope_arr = jnp.broadcast_to(slopes.T[:, None, :, None], (n_heads, 1, N_GROUPS, HEAD_DIM))
    slope_arr = slope_arr.reshape(n_heads, 1, N_GROUPS * HEAD_DIM)
    a_width = n_heads * HEAD_DIM
    in_specs = [pl.BlockSpec((1, 1, N_GROUPS * HEAD_DIM), lambda h, i: (h, 0, 0)),
                pl.BlockSpec((N_GROUPS, HEAD_DIM), lambda h, i: (0, 0)),
                pl.BlockSpec((N_GROUPS, HEAD_DIM), lambda h, i: (0, 0))]
    operands = [slope_arr, g_q, g_k]
    max_pr = BAND * max(dils)
    for g, dil in enumerate(dils):
        pr = BAND * dil
        ratio = sb // pr
        cq = (col_q + g * a_width) // HEAD_DIM
        ck = (col_k + g * a_width) // HEAD_DIM
        cv = (col_v + g * a_width) // HEAD_DIM
        cur = lambda c: pl.BlockSpec((sb, HEAD_DIM), lambda h, i, c=c: (i, c + h))
        prev = lambda c: pl.BlockSpec((pr, HEAD_DIM),
                                      lambda h, i, c=c, ratio=ratio: (jnp.maximum(i * ratio - 1, 0), c + h))
        in_specs += [cur(cq), cur(ck), cur(cv), prev(ck), prev(cv)]
        operands += [z, z, z, z, z]
    out_block = pl.BlockSpec((sb, HEAD_DIM), lambda h, i: (i, h))
    out_shape = (jax.ShapeDtypeStruct((out_rows, a_width), BF16),) + tuple(
        jax.ShapeDtypeStruct((seq, a_width), F32) for _ in dils)
    body = functools.partial(_attn_prompt_body, sb=sb, dils=tuple(dils))
    return pl.pallas_call(
        body,
        grid=(n_heads, nb),
        in_specs=in_specs,
        out_specs=(out_block,) * 4,
        out_shape=out_shape,
        scratch_shapes=[pltpu.VMEM((sb, HEAD_DIM), F32),
                        pltpu.VMEM((sb + max_pr, HEAD_DIM), F32),
                        pltpu.VMEM((sb + max_pr, HEAD_DIM), F32),
                        pltpu.VMEM((N_GROUPS, sb, HEAD_DIM), F32),
                        pltpu.VMEM((N_GROUPS, sb, HEAD_DIM), F32)],
        compiler_params=pltpu.CompilerParams(dimension_semantics=("arbitrary", "arbitrary"),
                                             vmem_limit_bytes=VMEM_LIMIT),
        name="attn_prompt",
    )(*operands)


def _sample_problems(dil, t_new):
    n_prob = min(dil, t_new)
    return n_prob, t_new // n_prob


def _sample_bias_tables(groups, t_new, n_heads):
    n_soft = len(groups) * n_heads
    tabs_c, tabs_n = [], []
    for g, (window, dil) in enumerate(groups):
        n_prob, tok_per = _sample_problems(dil, t_new)
        slopes = 2.0 ** (-8.0 * (g * n_heads + np.arange(n_heads) + 1) / n_soft)
        rows = np.arange(tok_per * n_heads)
        i, hp = rows // n_heads, rows % n_heads
        cols = np.arange(BAND * n_heads)
        m, h = cols // n_heads, cols % n_heads
        dist = window + i[:, None] * n_prob - m[None, :] * dil
        ok = (h[None, :] == hp[:, None]) & (dist <= window) & (dist > 0) & (dist % dil == 0)
        tabs_c.append(np.where(ok, -slopes[hp][:, None] * dist, -np.inf).astype(np.float32))
        cols = np.arange(t_new * n_heads)
        s, h = cols // n_heads, cols % n_heads
        per_p = []
        for p in range(n_prob):
            dist = (p + i * n_prob)[:, None] - s[None, :]
            ok = (h[None, :] == hp[:, None]) & (dist >= 0) & (dist % dil == 0)
            per_p.append(np.where(ok, -slopes[hp][:, None] * dist, -np.inf).astype(np.float32))
        tabs_n.append(np.stack(per_p))
    return tabs_c, tabs_n


def _attn_sample_body(gq_ref, gk_ref, q_ref, k_ref, v_ref, c0_ref, c1_ref, c2_ref,
                      bc0, bc1, bc2, bn0, bn1, bn2, o_ref, kn_ref, og_s, lse_s, *, t_new, groups, n_heads):
    caches = (c0_ref, c1_ref, c2_ref)
    bias_c = (bc0, bc1, bc2)
    bias_n = (bn0, bn1, bn2)
    kv_rows = 2 * n_heads
    for g, (window, dil) in enumerate(groups):
        c_ref = caches[g]
        n_prob, tok_per = _sample_problems(dil, t_new)
        qn = _rms(q_ref[0, g], gq_ref[g:g + 1, :])
        kn = _rms(k_ref[0, g], gk_ref[g:g + 1, :])
        kn_ref[0, g] = kn
        knb = kn.astype(BF16)
        vnb = v_ref[0, g].astype(BF16)
        for p in range(n_prob):
            toks = [p + i * n_prob for i in range(tok_per)]
            parts = [qn[t * n_heads:(t + 1) * n_heads] for t in toks]
            qp = (parts[0] if tok_per == 1 else jnp.concatenate(parts, axis=0)).astype(BF16)
            kc = c_ref[:, p * kv_rows:p * kv_rows + n_heads, :].reshape(BAND * n_heads, HEAD_DIM).astype(BF16)
            vc = c_ref[:, p * kv_rows + n_heads:(p + 1) * kv_rows, :].reshape(BAND * n_heads, HEAD_DIM).astype(BF16)
            lc = _dot_nt(qp, kc) * ATTN_SCALE + bias_c[g][...]
            ln = _dot_nt(qp, knb) * ATTN_SCALE + bias_n[g][p]
            mx = jnp.maximum(jnp.max(lc, axis=-1, keepdims=True), jnp.max(ln, axis=-1, keepdims=True))
            pc = jnp.exp(lc - mx)
            pn = jnp.exp(ln - mx)
            ssum = jnp.sum(pc, axis=-1, keepdims=True) + jnp.sum(pn, axis=-1, keepdims=True)
            o = (_dot(pc.astype(BF16), vc) + _dot(pn.astype(BF16), vnb)) / ssum
            lse = jnp.broadcast_to(mx + jnp.log(ssum), o.shape)
            for i, t in enumerate(toks):
                og_s[g, t * n_heads:(t + 1) * n_heads, :] = o[i * n_heads:(i + 1) * n_heads]
                lse_s[g, t * n_heads:(t + 1) * n_heads, :] = lse[i * n_heads:(i + 1) * n_heads]
    l0, l1, l2 = lse_s[0], lse_s[1], lse_s[2]
    mx = jnp.maximum(jnp.maximum(l0, l1), l2)
    w0, w1, w2 = jnp.exp(l0 - mx), jnp.exp(l1 - mx), jnp.exp(l2 - mx)
    o_ref[0] = (w0 * og_s[0] + w1 * og_s[1] + w2 * og_s[2]) / (w0 + w1 + w2)


def _attn_sample(qs, ks, vs, caches, g_q, g_k, *, n_batch, t_new, groups, n_heads=A_HEADS):
    n_g = len(groups)
    rows = t_new * n_heads
    tabs_c, tabs_n = _sample_bias_tables(groups, t_new, n_heads)
    new_spec = lambda: pl.BlockSpec((1, n_g, rows, HEAD_DIM), lambda b: (b, 0, 0, 0))
    in_specs = [pl.BlockSpec((n_g, HEAD_DIM), lambda b: (0, 0)),
                pl.BlockSpec((n_g, HEAD_DIM), lambda b: (0, 0)),
                new_spec(), new_spec(), new_spec()]
    operands = [g_q, g_k, qs, ks, vs]
    for c, (window, dil) in zip(caches, groups):
        assert c.shape[1] == window and window == BAND * dil
        n_prob, _ = _sample_problems(dil, t_new)
        c3 = c.reshape(n_batch * BAND, dil * 2 * n_heads, HEAD_DIM)
        in_specs.append(pl.BlockSpec((BAND, n_prob * 2 * n_heads, HEAD_DIM), lambda b: (b, 0, 0)))
        operands.append(c3)
    for tab in tabs_c:
        in_specs.append(pl.BlockSpec(tab.shape, lambda b: (0, 0)))
        operands.append(jnp.asarray(tab))
    for tab in tabs_n:
        in_specs.append(pl.BlockSpec(tab.shape, lambda b: (0, 0, 0)))
        operands.append(jnp.asarray(tab))
    body = functools.partial(_attn_sample_body, t_new=t_new, groups=tuple(groups), n_heads=n_heads)
    return pl.pallas_call(
        body,
        grid=(n_batch,),
        in_specs=in_specs,
        out_specs=(pl.BlockSpec((1, rows, HEAD_DIM), lambda b: (b, 0, 0)), new_spec()),
        out_shape=(jax.ShapeDtypeStruct((n_batch, rows, HEAD_DIM), F32),
                   jax.ShapeDtypeStruct((n_batch, n_g, rows, HEAD_DIM), F32)),
        scratch_shapes=[pltpu.VMEM((n_g, rows, HEAD_DIM), F32), pltpu.VMEM((n_g, rows, HEAD_DIM), F32)],
        compiler_params=pltpu.CompilerParams(dimension_semantics=("arbitrary",),
                                             vmem_limit_bytes=VMEM_LIMIT),
        name="attn_sample",
    )(*operands)


def _split3(x):
    hi = x.astype(BF16)
    r1 = x - hi.astype(F32)
    mid = r1.astype(BF16)
    lo = (r1 - mid.astype(F32)).astype(BF16)
    return hi, mid, lo


def _hgrn_heads(hq, hf, hi, hog, lb, g_out, sts, *, tb, n_valid=None):
    c = HG_CHUNK
    nh = len(sts)
    sts = list(sts)
    q = hq * _sigmoid(hq) * (HG_KDIM ** -0.5)
    gate = lb + (1.0 - lb) * _sigmoid(hf)
    log_g = jnp.log(gate)
    k = (1.0 - lb) * _sigmoid(-hf)
    if n_valid is not None:
        rows = lax.broadcasted_iota(jnp.int32, hq.shape, 0)
        log_g = jnp.where(rows < n_valid, log_g, 0.0)
        k = jnp.where(rows < n_valid, k, 0.0)
        q = jnp.where(rows < n_valid, q, 0.0)
    row = lax.broadcasted_iota(jnp.int32, (tb, tb), 0)
    col = lax.broadcasted_iota(jnp.int32, (tb, tb), 1)
    same = (row // c) == (col // c)
    tri_ok = same & (col <= row)
    tri = jnp.where(tri_ok, 1.0, 0.0).astype(BF16)
    blk = jnp.where(same, 1.0, 0.0).astype(BF16)
    p_hi, p_mid, p_lo = _split3(log_g)
    b = _dot(tri, p_hi) + _dot(tri, p_mid) + _dot(tri, p_lo)
    b_last = _dot(blk, p_hi) + _dot(blk, p_mid) + _dot(blk, p_lo)
    q_dec = (q * jnp.exp(b)).astype(BF16)
    k_dec = (k * jnp.exp(-b)).astype(BF16)
    k_end = (k * jnp.exp(b_last - b)).astype(BF16)
    decay = jnp.exp(b_last)
    vb = hi.astype(BF16)
    head = lambda x, h: x[:, h * HG_KDIM:(h + 1) * HG_KDIM]
    o_intra = []
    for h in range(nh):
        a = jnp.where(tri_ok, _dot_nt(head(q_dec, h), head(k_dec, h)), 0.0)
        o_intra.append(_dot(a.astype(BF16), head(vb, h)))
    parts = [[] for _ in range(nh)]
    for ci in range(tb // c):
        sl = slice(ci * c, (ci + 1) * c)
        for h in range(nh):
            parts[h].append(_dot_nt(head(q_dec, h)[sl], sts[h].astype(BF16)))
            sts[h] = sts[h] * head(decay, h)[ci * c:ci * c + 1, :] + _dot_tn(head(vb, h)[sl], head(k_end, h)[sl])
    outs = []
    for h in range(nh):
        o = o_intra[h] + (jnp.concatenate(parts[h], axis=0) if len(parts[h]) > 1 else parts[h][0])
        outs.append(_rms(o, g_out) * _sigmoid(head(hog, h)))
    return outs, sts


def _hgrn_prompt_body(hq_ref, hf_ref, hi_ref, hog_ref, lb_ref, go_ref, o_ref, s_ref, st_s, *, tb, nt, hb):
    t = pl.program_id(1)

    @pl.when(t == 0)
    def _():
        st_s[...] = jnp.zeros_like(st_s)

    outs, sts = _hgrn_heads(hq_ref[...], hf_ref[...], hi_ref[...], hog_ref[...], lb_ref[0], go_ref[...],
                            [st_s[hh] for hh in range(hb)], tb=tb)
    for hh in range(hb):
        o_ref[:, hh * HG_VDIM:(hh + 1) * HG_VDIM] = outs[hh].astype(o_ref.dtype)
        st_s[hh] = sts[hh]

    @pl.when(t == nt - 1)
    def _():
        for hh in range(hb):
            s_ref[hh] = sts[hh].T


def _hgrn_prompt(z, lb, g_out, *, seq, tb, hb, out_rows=None, n_heads=HG_HEADS, col_hq=COL_HQ, col_hf=COL_HF,
                 col_hi=COL_HI, col_hog=COL_HOG):
    out_rows = seq if out_rows is None else out_rows
    nt = seq // tb
    bw = hb * HG_KDIM
    assert all(c0 % bw == 0 for c0 in (col_hq, col_hf, col_hi, col_hog))
    blk = lambda c0: pl.BlockSpec((tb, bw), lambda h, t, c=c0 // bw: (t, c + h))
    body = functools.partial(_hgrn_prompt_body, tb=tb, nt=nt, hb=hb)
    return pl.pallas_call(
        body,
        grid=(n_heads // hb, nt),
        in_specs=[blk(col_hq), blk(col_hf), blk(col_hi), blk(col_hog),
                  pl.BlockSpec((1, 1, bw), lambda h, t: (h, 0, 0)),
                  pl.BlockSpec((1, HG_VDIM), lambda h, t: (0, 0))],
        out_specs=(pl.BlockSpec((tb, bw), lambda h, t: (t, h)),
                   pl.BlockSpec((hb, HG_KDIM, HG_VDIM), lambda h, t: (h, 0, 0))),
        out_shape=(jax.ShapeDtypeStruct((out_rows, n_heads * HG_VDIM), BF16),
                   jax.ShapeDtypeStruct((n_heads, HG_KDIM, HG_VDIM), F32)),
        scratch_shapes=[pltpu.VMEM((hb, HG_VDIM, HG_KDIM), F32)],
        compiler_params=pltpu.CompilerParams(dimension_semantics=("arbitrary", "arbitrary"),
                                             vmem_limit_bytes=VMEM_LIMIT),
        name="hgrn_prompt",
    )(z, z, z, z, lb.reshape(n_heads // hb, 1, bw), g_out.reshape(1, HG_VDIM))


def _hgrn_sample_body(hq_ref, hf_ref, hi_ref, hog_ref, lb_ref, go_ref, s0_ref, o_ref, s_ref, *, t_new, n_heads):
    pad = jnp.zeros((HG_CHUNK - t_new, n_heads * HG_KDIM), F32)
    ext = lambda ref: jnp.concatenate([ref[...], pad], axis=0)
    outs, sts = _hgrn_heads(ext(hq_ref), ext(hf_ref), ext(hi_ref), ext(hog_ref), lb_ref[...], go_ref[...],
                            [s0_ref[0, h].T for h in range(n_heads)], tb=HG_CHUNK, n_valid=t_new)
    for h in range(n_heads):
        o_ref[:, h * HG_VDIM:(h + 1) * HG_VDIM] = outs[h][:t_new]
        s_ref[0, h] = sts[h].T


def _hgrn_sample(hq, hf, hi, hog, lb, g_out, s0, *, n_batch, t_new, n_heads=HG_HEADS):
    width = n_heads * HG_KDIM
    row = lambda: pl.BlockSpec((t_new, width), lambda b: (b, 0))
    st = lambda: pl.BlockSpec((1, n_heads, HG_KDIM, HG_VDIM), lambda b: (b, 0, 0, 0))
    body = functools.partial(_hgrn_sample_body, t_new=t_new, n_heads=n_heads)
    return pl.pallas_call(
        body,
        grid=(n_batch,),
        in_specs=[row(), row(), row(), row(),
                  pl.BlockSpec((1, width), lambda b: (0, 0)),
                  pl.BlockSpec((1, HG_VDIM), lambda b: (0, 0)),
                  st()],
        out_specs=(row(), st()),
        out_shape=(jax.ShapeDtypeStruct((n_batch * t_new, width), F32),
                   jax.ShapeDtypeStruct(s0.shape, F32)),
        compiler_params=pltpu.CompilerParams(dimension_semantics=("arbitrary",),
                                             vmem_limit_bytes=VMEM_LIMIT),
        name="hgrn_sample",
    )(hq, hf, hi, hog, lb.reshape(1, width), g_out.reshape(1, HG_VDIM), s0)


def _tile_spec(bm, bn, col0=0):
    assert col0 % bn == 0
    cb = col0 // bn
    return pl.BlockSpec((bm, bn), lambda i, j: (i, cb + j))


def kernel(x_prompt, x_sample, cache_kv_w128, cache_kv_w512, cache_kv_w2048, state_hgrn, p_prompt, p_sample,
           g_mix, w_in, g_q, g_k, hg_lb_raw, g_hg_out, w_up_attn, w_up_hgrn, w_out, g_ffn, w_ff_up, w_ff_down,
           w_ple, w_ple_gate):
    bm, bn = 2816, 256
    bn_wide = 512
    bm_ple = 1408
    bm_res = 2816
    down_blocks = dict(bm=1408, bn=2048, bk=1024)
    xp, xs = x_prompt.reshape(SEQ, D_MODEL), x_sample.reshape(N_SAMPLE, D_MODEL)
    pp, ps = p_prompt.reshape(SEQ, PLE_DIM), p_sample.reshape(N_SAMPLE, PLE_DIM)
    lb = jnp.cumsum(jax.nn.softmax(hg_lb_raw.astype(F32), axis=0), axis=0)[0]
    rows = lambda width, dt: jax.ShapeDtypeStruct((M_ALL, width), dt)

    def plain(accs, i, j, rows_sl, a_refs, extra, outs):
        outs[0][rows_sl, :] = accs[0]

    n_mix = _rmsnorm_rows(xp, g_mix[0], x_tail=xs, name="norm_mix")
    z, = _matmul_rows([(n_mix, w_in[0])], bm=bm, bn=bn_wide, out_shapes=(rows(IN_WIDTH, F32),), epilogue=plain,
                      name="in_proj")

    o_attn, kn0, kn1, kn2 = _attn_prompt(z, g_q[0], g_k[0], seq=SEQ, sb=2048, out_rows=M_ALL,
                                         dils=tuple(d for _, d in DIL_GROUPS))
    zs = z[SEQ:]
    def by_group(col0):
        a = zs[:, col0:col0 + N_GROUPS * A_WIDTH].reshape(DEC_BATCH, DEC_SEQ, N_GROUPS, A_HEADS, HEAD_DIM)
        return a.transpose(0, 2, 1, 3, 4).reshape(DEC_BATCH, N_GROUPS, DEC_SEQ * A_HEADS, HEAD_DIM)

    caches = [c[0] for c in (cache_kv_w128, cache_kv_w512, cache_kv_w2048)]
    o_attn_s, kn_s = _attn_sample(by_group(COL_Q), by_group(COL_K), by_group(COL_V), caches,
                                  g_q[0], g_k[0], n_batch=DEC_BATCH, t_new=DEC_SEQ, groups=DIL_GROUPS)
    o_attn = lax.dynamic_update_slice(o_attn, o_attn_s.reshape(N_SAMPLE, A_WIDTH).astype(BF16), (SEQ, 0))

    o_hg, st_p = _hgrn_prompt(z, lb, g_hg_out[0], seq=SEQ, tb=256, hb=8, out_rows=M_ALL)
    o_hg_s, st_s = _hgrn_sample(zs[:, COL_HQ:COL_HF], zs[:, COL_HF:COL_HI], zs[:, COL_HI:COL_HOG],
                                zs[:, COL_HOG:COL_GA], lb, g_hg_out[0], state_hgrn[0],
                                n_batch=DEC_BATCH, t_new=DEC_SEQ)
    o_hg = lax.dynamic_update_slice(o_hg, o_hg_s.astype(BF16), (SEQ, 0))

    def merge_epilogue(accs, i, j, rows_sl, a_refs, extra, outs):
        ga, gb = extra[0][rows_sl, :], extra[1][rows_sl, :]
        outs[0][rows_sl, :] = (_sigmoid(ga) * accs[0] + _sigmoid(gb) * accs[1]).astype(BF16)

    def out_proj_epilogue(accs, i, j, rows_sl, a_refs, extra, outs):
        xp_ref, xs_ref, g_ref = extra
        x1_ref, x1g_ref, ssq_ref = outs

        def emit(rows, x1, ssq_before):
            x1_ref[rows, :] = x1
            x1g_ref[rows, :] = (x1 * g_ref[...]).astype(BF16)
            ssq_ref[rows, :] = ssq_before + jnp.sum(x1 * x1, axis=-1, keepdims=True)

        ssq_before = ssq_ref[rows_sl, :]
        emit(rows_sl, xp_ref[rows_sl, :] + accs[0], ssq_before)
        if rows_sl.stop == bm_res:
            t0 = bm_res - N_SAMPLE - rows_sl.start

            @pl.when(i == pl.num_programs(0) - 1)
            def _():
                emit(slice(bm_res - N_SAMPLE, bm_res), xs_ref[...] + accs[0][t0:, :], ssq_before[t0:, :])

    merged, = _matmul_rows([(o_attn, w_up_attn[0]), (o_hg, w_up_hgrn[0])], bm=bm, bn=bn, extras=(z, z),
                           extra_specs=(_tile_spec(bm, bn, COL_GA), _tile_spec(bm, bn, COL_GB)),
                           out_shapes=(rows(D_MODEL, BF16),), epilogue=merge_epilogue, row_chunk=352,
                           name="up_merge")
    def zero_ssq(outs):
        outs[2][...] = jnp.zeros(outs[2].shape, F32)

    x1, x1g, ssq = _matmul_rows(
        [(merged, w_out[0])], bm=bm_res, bn=bn, extras=(xp, xs, g_ffn),
        extra_specs=(_tile_spec(bm_res, bn), pl.BlockSpec((N_SAMPLE, bn), lambda i, j: (0, j)),
                     pl.BlockSpec((1, bn), lambda i, j: (0, j))),
        out_shapes=(rows(D_MODEL, F32), rows(D_MODEL, BF16), rows(LANES, F32)),
        out_specs=(_tile_spec(bm_res, bn), _tile_spec(bm_res, bn),
                   pl.BlockSpec((bm_res, LANES), lambda i, j: (i, 0))),
        epilogue=out_proj_epilogue, first_col_init=zero_ssq, name="out_proj")

    def ffn_up_epilogue(accs, i, j, rows_sl, a_refs, extra, outs):
        inv_rms = lax.rsqrt(extra[0][rows_sl, 0:1] * (1.0 / D_MODEL) + NORM_EPS)
        outs[0][rows_sl, :] = jnp.square(jnp.maximum(accs[0] * inv_rms, 0.0)).astype(BF16)

    hid, = _matmul_rows([(x1g, w_ff_up[0])], bm=bm, bn=bn_wide, extras=(ssq,),
                        extra_specs=(pl.BlockSpec((bm, LANES), lambda i, j: (i, 0)),),
                        out_shapes=(rows(FFN_HIDDEN, BF16),), epilogue=ffn_up_epilogue, name="ffn_up")
    x2 = _matmul_ksplit(hid, w_ff_down[0], x1, name="ffn_down", **down_blocks)

    n_ple_blocks = M_ALL // bm_ple

    def ple_epilogue(accs, i, j, rows_sl, a_refs, extra, outs):
        pp_ref, ps_ref, wp_ref = extra
        yp_ref, ys_ref = outs
        x2_tile = a_refs[0][rows_sl, pl.ds(pl.multiple_of(j * bn, bn), bn)]
        gate = _sigmoid(accs[0])
        wp = wp_ref[...].astype(BF16)
        yp_ref[rows_sl, :] = x2_tile + gate * _dot(pp_ref[rows_sl, :].astype(BF16), wp)
        if rows_sl.stop == bm_ple:
            s0 = bm_ple - N_SAMPLE - rows_sl.start

            @pl.when(i == n_ple_blocks - 1)
            def _():
                ys_ref[...] = x2_tile[s0:, :] + gate[s0:, :] * _dot(ps_ref[...].astype(BF16), wp)

    y_p, y_s = _matmul_rows(
        [(x2, w_ple_gate[0])], bm=bm_ple, bn=bn, extras=(pp, ps, w_ple[0]),
        extra_specs=(pl.BlockSpec((bm_ple, PLE_DIM), lambda i, j: (i, 0)),
                     pl.BlockSpec((N_SAMPLE, PLE_DIM), lambda i, j: (0, 0)),
                     pl.BlockSpec((PLE_DIM, bn), lambda i, j: (0, j))),
        out_shapes=(jax.ShapeDtypeStruct((SEQ, D_MODEL), F32), jax.ShapeDtypeStruct((N_SAMPLE, D_MODEL), F32)),
        out_specs=(pl.BlockSpec((bm_ple, bn), lambda i, j: (i, j)),
                   pl.BlockSpec((N_SAMPLE, bn), lambda i, j: (0, jnp.where(i == n_ple_blocks - 1, j, 0)))),
        epilogue=ple_epilogue, name="ple")

    y_prompt = y_p.reshape(1, SEQ, D_MODEL)
    y_sample = y_s.reshape(DEC_BATCH, DEC_SEQ, D_MODEL)
    kv_p, kv_s = [], []
    for g, (window, _) in enumerate(DIL_GROUPS):
        length = min(window, SEQ)
        kn_g = (kn0, kn1, kn2)[g][SEQ - length:].reshape(length, A_HEADS, HEAD_DIM)
        v_g = z[SEQ - length:SEQ, COL_V + g * A_WIDTH:COL_V + (g + 1) * A_WIDTH].reshape(length, A_HEADS, HEAD_DIM)
        kv_p.append(jnp.stack([kn_g, v_g], axis=1)[None, None])
        ks_g = kn_s[:, g].reshape(DEC_BATCH, DEC_SEQ, A_HEADS, HEAD_DIM)
        vs_g = zs[:, COL_V + g * A_WIDTH:COL_V + (g + 1) * A_WIDTH].reshape(DEC_BATCH, DEC_SEQ, A_HEADS, HEAD_DIM)
        kv_s.append(jnp.stack([ks_g, vs_g], axis=2)[None])
    return (y_prompt, y_sample, kv_p[0], kv_p[1], kv_p[2], st_p[None, None],
            kv_s[0], kv_s[1], kv_s[2], st_s[None])
```

```python
import functools

import numpy as np
import jax
import jax.numpy as jnp
from jax import lax
from jax.experimental import pallas as pl
from jax.experimental.pallas import tpu as pltpu

F32 = jnp.float32
BF16 = jnp.bfloat16

D_MODEL = 4096
SEQ = 8192
DEC_BATCH = 32
DEC_SEQ = 8
N_SAMPLE = DEC_BATCH * DEC_SEQ
M_ALL = SEQ + N_SAMPLE
DIL_GROUPS = ((128, 1), (512, 4), (2048, 16))
N_GROUPS = 3
A_HEADS = 8
HEAD_DIM = 128
A_WIDTH = A_HEADS * HEAD_DIM
BAND = 128
HG_HEADS = 16
HG_KDIM = 128
HG_VDIM = 128
HG_WIDTH = HG_HEADS * HG_VDIM
HG_CHUNK = 32
FFN_HIDDEN = 4 * D_MODEL
PLE_DIM = 256
NORM_EPS = 1e-6
ATTN_SCALE = HEAD_DIM ** -0.5

COL_Q = 0
COL_K = COL_Q + N_GROUPS * A_WIDTH
COL_V = COL_K + N_GROUPS * A_WIDTH
COL_HQ = COL_V + N_GROUPS * A_WIDTH
COL_HF = COL_HQ + HG_HEADS * HG_KDIM
COL_HI = COL_HF + HG_HEADS * HG_KDIM
COL_HOG = COL_HI + HG_WIDTH
COL_GA = COL_HOG + HG_WIDTH
COL_GB = COL_GA + D_MODEL
IN_WIDTH = COL_GB + D_MODEL

VMEM_LIMIT = 56 * 1024 * 1024
LANES = 128


def _dot(a, b):
    return jnp.dot(a, b, preferred_element_type=F32)


def _dot_nt(a, b):
    return lax.dot_general(a, b, (((1,), (1,)), ((), ())), preferred_element_type=F32)


def _dot_tn(a, b):
    return lax.dot_general(a, b, (((0,), (0,)), ((), ())), preferred_element_type=F32)


def _rms(x, g):
    return x * lax.rsqrt(jnp.mean(x * x, axis=-1, keepdims=True) + NORM_EPS) * g


def _sigmoid(x):
    return 1.0 / (1.0 + jnp.exp(-x))


_CAST_ROWS = 128
_DOT_ROWS = 704


def _mm_rows_body(*refs, needs_cast, n_extra, n_out, epilogue, row_chunk, first_col_init):
    n_pairs = len(needs_cast)
    a_refs = refs[0:2 * n_pairs:2]
    w_refs = refs[1:2 * n_pairs:2]
    pos = 2 * n_pairs
    extra = refs[pos:pos + n_extra]
    outs = refs[pos + n_extra:pos + n_extra + n_out]
    scratch = list(refs[pos + n_extra + n_out:])
    i, j = pl.program_id(0), pl.program_id(1)
    lhs_refs = []
    for a_ref, cast in zip(a_refs, needs_cast):
        if cast:
            ab_ref = scratch.pop(0)

            @pl.when(j == 0)
            def _(a_ref=a_ref, ab_ref=ab_ref):
                def rows(c, carry):
                    r0 = pl.multiple_of(c * _CAST_ROWS, _CAST_ROWS)
                    ab_ref[pl.ds(r0, _CAST_ROWS), :] = a_ref[pl.ds(r0, _CAST_ROWS), :].astype(BF16)
                    return carry
                lax.fori_loop(0, a_ref.shape[0] // _CAST_ROWS, rows, 0)

            lhs_refs.append(ab_ref)
        else:
            lhs_refs.append(a_ref)
    if first_col_init is not None:
        @pl.when(j == 0)
        def _():
            first_col_init(outs)

    wbs = [w_ref[...].astype(BF16) for w_ref in w_refs]
    for r0 in range(0, lhs_refs[0].shape[0], row_chunk):
        rows = slice(r0, r0 + row_chunk)
        accs = [_dot(lhs_ref[rows, :], wb) for lhs_ref, wb in zip(lhs_refs, wbs)]
        epilogue(accs, i, j, rows, a_refs, extra, outs)


def _matmul_rows(pairs, *, bm, bn, extras=(), extra_specs=(), out_shapes, out_specs=None, epilogue, name,
                 row_chunk=_DOT_ROWS, first_col_init=None):
    m = pairs[0][0].shape[0]
    n = pairs[0][1].shape[1]
    assert m % bm == 0 and n % bn == 0 and bm % _CAST_ROWS == 0 and bm % row_chunk == 0
    in_specs, operands, scratch, needs_cast = [], [], [], []
    for a, w in pairs:
        kdim = w.shape[0]
        assert a.shape == (m, kdim) and w.shape[1] == n
        in_specs += [pl.BlockSpec((bm, kdim), lambda i, j: (i, 0), pipeline_mode=pl.Buffered(1)),
                     pl.BlockSpec((kdim, bn), lambda i, j: (0, j))]
        operands += [a, w]
        needs_cast.append(a.dtype != BF16)
        if needs_cast[-1]:
            scratch.append(pltpu.VMEM((bm, kdim), BF16))
    if out_specs is None:
        out_specs = tuple(pl.BlockSpec((bm, bn), lambda i, j: (i, j)) for _ in out_shapes)
    body = functools.partial(_mm_rows_body, needs_cast=tuple(needs_cast), n_extra=len(extras),
                             n_out=len(out_shapes), epilogue=epilogue, row_chunk=row_chunk,
                             first_col_init=first_col_init)
    return pl.pallas_call(
        body,
        grid=(m // bm, n // bn),
        in_specs=in_specs + list(extra_specs),
        out_specs=tuple(out_specs),
        out_shape=tuple(out_shapes),
        scratch_shapes=scratch,
        compiler_params=pltpu.CompilerParams(dimension_semantics=("arbitrary", "arbitrary"),
                                             vmem_limit_bytes=VMEM_LIMIT),
        name=name,
    )(*operands, *extras)


_RESID_SLOTS = 2


def _mm_ksplit_body(a_ref, w_ref, r_hbm, o_ref, r_buf, r_sem, *, row_chunk, resid_rows):
    j, i, k = pl.program_id(0), pl.program_id(1), pl.program_id(2)
    bm, bn = o_ref.shape
    n_res = bm // resid_rows

    def r_copy(c):
        r0 = pl.multiple_of(i * bm + c * resid_rows, resid_rows)
        src = r_hbm.at[pl.ds(r0, resid_rows), pl.ds(pl.multiple_of(j * bn, bn), bn)]
        return pltpu.make_async_copy(src, r_buf.at[c % _RESID_SLOTS], r_sem.at[c % _RESID_SLOTS])

    @pl.when(k == 0)
    def _():
        for c in range(min(_RESID_SLOTS, n_res)):
            r_copy(c).start()

    def k_step(first):
        wb = w_ref[...].astype(BF16)
        for r0 in range(0, bm, row_chunk):
            rows = slice(r0, r0 + row_chunk)
            part = _dot(a_ref[rows, :], wb)
            if first:
                o_ref[rows, :] = part
            else:
                o_ref[rows, :] += part

    @pl.when(k == 0)
    def _():
        k_step(True)

    @pl.when(k > 0)
    def _():
        k_step(False)

    @pl.when(k == pl.num_programs(2) - 1)
    def _():
        for c in range(n_res):
            rows = slice(c * resid_rows, (c + 1) * resid_rows)
            r_copy(c).wait()
            o_ref[rows, :] += r_buf[c % _RESID_SLOTS]
            if c + _RESID_SLOTS < n_res:
                r_copy(c + _RESID_SLOTS).start()


def _matmul_ksplit(a, w, resid, *, bm, bn, bk, name, row_chunk=_DOT_ROWS, resid_rows=352):
    m, kdim = a.shape
    n = w.shape[1]
    assert m % bm == 0 and n % bn == 0 and kdim % bk == 0 and bm % row_chunk == 0 and a.dtype == BF16
    assert resid.shape == (m, n) and resid.dtype == F32 and bm % resid_rows == 0
    return pl.pallas_call(
        functools.partial(_mm_ksplit_body, row_chunk=row_chunk, resid_rows=resid_rows),
        grid=(n // bn, m // bm, kdim // bk),
        in_specs=[pl.BlockSpec((bm, bk), lambda j, i, k: (i, k)),
                  pl.BlockSpec((bk, bn), lambda j, i, k: (k, j)),
                  pl.BlockSpec(memory_space=pl.ANY)],
        out_specs=pl.BlockSpec((bm, bn), lambda j, i, k: (i, j)),
        out_shape=jax.ShapeDtypeStruct((m, n), F32),
        scratch_shapes=[pltpu.VMEM((_RESID_SLOTS, resid_rows, bn), F32),
                        pltpu.SemaphoreType.DMA((_RESID_SLOTS,))],
        compiler_params=pltpu.CompilerParams(dimension_semantics=("arbitrary", "arbitrary", "arbitrary"),
                                             vmem_limit_bytes=VMEM_LIMIT),
        name=name,
    )(a, w, resid)


def _rmsnorm_body(*refs, n_main):
    g_ref, o_ref = refs[-2:]
    if n_main is None:
        o_ref[...] = _rms(refs[0][...], g_ref[...]).astype(o_ref.dtype)
        return
    i = pl.program_id(0)

    @pl.when(i < n_main)
    def _():
        o_ref[...] = _rms(refs[0][...], g_ref[...]).astype(o_ref.dtype)

    @pl.when(i >= n_main)
    def _():
        o_ref[...] = _rms(refs[1][...], g_ref[...]).astype(o_ref.dtype)


def _rmsnorm_rows(x, g, *, x_tail=None, bm=256, name="rmsnorm"):
    m, d = x.shape
    assert m % bm == 0
    n_main = m // bm
    in_specs = [pl.BlockSpec((bm, d), lambda i: (jnp.minimum(i, n_main - 1), 0))]
    operands = [x]
    if x_tail is not None:
        assert x_tail.shape[0] % bm == 0
        in_specs.append(pl.BlockSpec((bm, d), lambda i: (jnp.maximum(i - n_main, 0), 0)))
        operands.append(x_tail)
        m += x_tail.shape[0]
    in_specs.append(pl.BlockSpec((1, d), lambda i: (0, 0)))
    return pl.pallas_call(
        functools.partial(_rmsnorm_body, n_main=None if x_tail is None else n_main),
        grid=(m // bm,),
        in_specs=in_specs,
        out_specs=pl.BlockSpec((bm, d), lambda i: (i, 0)),
        out_shape=jax.ShapeDtypeStruct((m, d), BF16),
        compiler_params=pltpu.CompilerParams(dimension_semantics=("arbitrary",)),
        name=name,
    )(*operands, g.reshape(1, d))


def _ld_rows(ref, start, dil, n=BAND):
    if dil == 1:
        return ref[pl.ds(start, n), :]
    return ref[pl.ds(start, n, stride=dil), :]


def _st_rows(ref, g, start, dil, val):
    if dil == 1:
        ref[g, pl.ds(start, BAND), :] = val
    else:
        ref[g, pl.ds(start, BAND, stride=dil), :] = val


def _attn_prompt_body(slope_ref, gq_ref, gk_ref,
                      q0, k0, v0, kp0, vp0, q1, k1, v1, kp1, vp1, q2, k2, v2, kp2, vp2,
                      o_ref, kn0, kn1, kn2,
                      qn_s, kn_s, vv_s, og_s, lse_s, *, sb, dils):
    first_block = pl.program_id(1) == 0
    groups = ((q0, k0, v0, kp0, vp0, kn0), (q1, k1, v1, kp1, vp1, kn1), (q2, k2, v2, kp2, vp2, kn2))
    ii = lax.broadcasted_iota(jnp.int32, (BAND, BAND), 0)
    jj = lax.broadcasted_iota(jnp.int32, (BAND, BAND), 1)
    dist_cur = (ii - jj).astype(F32)
    dist_prev = (BAND + ii - jj).astype(F32)
    ok_cur = jj <= ii
    ok_prev = jj >= ii
    neg_inf = jnp.float32(-jnp.inf)
    base = kn_s.shape[0] - sb

    for g, dil in enumerate(dils):
        q_ref, k_ref, v_ref, kp_ref, vp_ref, kn_out = groups[g]
        pr = BAND * dil
        gq = gq_ref[g:g + 1, :]
        gk = gk_ref[g:g + 1, :]
        slope = slope_ref[0, :, g * HEAD_DIM:(g + 1) * HEAD_DIM]
        qn_s[...] = _rms(q_ref[...], gq)
        kn = _rms(k_ref[...], gk)
        kn_out[...] = kn
        kn_s[pl.ds(base, sb), :] = kn
        kn_s[pl.ds(base - pr, pr), :] = _rms(kp_ref[...], gk)
        vv_s[pl.ds(base, sb), :] = v_ref[...]
        vv_s[pl.ds(base - pr, pr), :] = vp_ref[...]
        bias_cur = jnp.where(ok_cur, -slope * (dist_cur * dil), neg_inf)
        bias_prev = jnp.where(ok_prev, -slope * (dist_prev * dil), neg_inf)
        bias = jnp.concatenate([bias_prev, bias_cur], axis=1)
        bias_first = jnp.concatenate([jnp.where(first_block, neg_inf, bias_prev), bias_cur], axis=1)
        ones = jnp.ones((2 * BAND, HEAD_DIM), BF16)
        for r in range(dil):
            for s in range(sb // pr):
                c0 = s * pr + r
                qs = _ld_rows(qn_s, c0, dil).astype(BF16)
                kk = _ld_rows(kn_s, base + c0 - pr, dil, 2 * BAND).astype(BF16)
                vv = _ld_rows(vv_s, base + c0 - pr, dil, 2 * BAND).astype(BF16)
                lg = _dot_nt(qs, kk) * ATTN_SCALE + (bias_first if s == 0 else bias)
                mx = jnp.max(lg, axis=-1, keepdims=True)
                p = jnp.exp(lg - mx).astype(BF16)
                oe = _dot(p, jnp.concatenate([vv, ones], axis=1))
                den = oe[:, HEAD_DIM:]
                _st_rows(og_s, g, c0, dil, oe[:, :HEAD_DIM] / den)
                _st_rows(lse_s, g, c0, dil, mx + jnp.log(den))

    l0, l1, l2 = lse_s[0], lse_s[1], lse_s[2]
    mx = jnp.maximum(jnp.maximum(l0, l1), l2)
    w0, w1, w2 = jnp.exp(l0 - mx), jnp.exp(l1 - mx), jnp.exp(l2 - mx)
    o = (w0 * og_s[0] + w1 * og_s[1] + w2 * og_s[2]) / (w0 + w1 + w2)
    o_ref[...] = o.astype(o_ref.dtype)


def _alibi_slopes():
    n = N_GROUPS * A_HEADS
    e = jnp.arange(1, n + 1, dtype=F32)
    return jnp.exp2(-8.0 * e / n).reshape(N_GROUPS, A_HEADS)


def _attn_prompt(z, g_q, g_k, *, seq, sb, dils, out_rows=None, n_heads=A_HEADS, col_q=COL_Q, col_k=COL_K,
                 col_v=COL_V):
    out_rows = seq if out_rows is None else out_rows
    nb = seq // sb
    slopes = _alibi_slopes()
    slope_arr = jnp.broadcast_to(slopes.T[:, None, :, None], (n_heads, 1, N_GROUPS, HEAD_DIM))
    slope_arr = slope_arr.reshape(n_heads, 1, N_GROUPS * HEAD_DIM)
    a_width = n_heads * HEAD_DIM
    in_specs = [pl.BlockSpec((1, 1, N_GROUPS * HEAD_DIM), lambda h, i: (h, 0, 0)),
                pl.BlockSpec((N_GROUPS, HEAD_DIM), lambda h, i: (0, 0)),
                pl.BlockSpec((N_GROUPS, HEAD_DIM), lambda h, i: (0, 0))]
    operands = [slope_arr, g_q, g_k]
    max_pr = BAND * max(dils)
    for g, dil in enumerate(dils):
        pr = BAND * dil
        ratio = sb // pr
        cq = (col_q + g * a_width) // HEAD_DIM
        ck = (col_k + g * a_width) // HEAD_DIM
        cv = (col_v + g * a_width) // HEAD_DIM
        cur = lambda c: pl.BlockSpec((sb, HEAD_DIM), lambda h, i, c=c: (i, c + h))
        prev = lambda c: pl.BlockSpec((pr, HEAD_DIM),
                                      lambda h, i, c=c, ratio=ratio: (jnp.maximum(i * ratio - 1, 0), c + h))
        in_specs += [cur(cq), cur(ck), cur(cv), prev(ck), prev(cv)]
        operands += [z, z, z, z, z]
    out_block = pl.BlockSpec((sb, HEAD_DIM), lambda h, i: (i, h))
    out_shape = (jax.ShapeDtypeStruct((out_rows, a_width), BF16),) + tuple(
        jax.ShapeDtypeStruct((seq, a_width), F32) for _ in dils)
    body = functools.partial(_attn_prompt_body, sb=sb, dils=tuple(dils))
    return pl.pallas_call(
        body,
        grid=(n_heads, nb),
        in_specs=in_specs,
        out_specs=(out_block,) * 4,
        out_shape=out_shape,
        scratch_shapes=[pltpu.VMEM((sb, HEAD_DIM), F32),
                        pltpu.VMEM((sb + max_pr, HEAD_DIM), F32),
                        pltpu.VMEM((sb + max_pr, HEAD_DIM), F32),
                        pltpu.VMEM((N_GROUPS, sb, HEAD_DIM), F32),
                        pltpu.VMEM((N_GROUPS, sb, HEAD_DIM), F32)],
        compiler_params=pltpu.CompilerParams(dimension_semantics=("arbitrary", "arbitrary"),
                                             vmem_limit_bytes=VMEM_LIMIT),
        name="attn_prompt",
    )(*operands)


def _sample_problems(dil, t_new):
    n_prob = min(dil, t_new)
    return n_prob, t_new // n_prob


def _sample_bias_tables(groups, t_new, n_heads):
    n_soft = len(groups) * n_heads
    tabs_c, tabs_n = [], []
    for g, (window, dil) in enumerate(groups):
        n_prob, tok_per = _sample_problems(dil, t_new)
        slopes = 2.0 ** (-8.0 * (g * n_heads + np.arange(n_heads) + 1) / n_soft)
        rows = np.arange(tok_per * n_heads)
        i, hp = rows // n_heads, rows % n_heads
        cols = np.arange(BAND * n_heads)
        m, h = cols // n_heads, cols % n_heads
        dist = window + i[:, None] * n_prob - m[None, :] * dil
        ok = (h[None, :] == hp[:, None]) & (dist <= window) & (dist > 0) & (dist % dil == 0)
        tabs_c.append(np.where(ok, -slopes[hp][:, None] * dist, -np.inf).astype(np.float32))
        cols = np.arange(t_new * n_heads)
        s, h = cols // n_heads, cols % n_heads
        per_p = []
        for p in range(n_prob):
            dist = (p + i * n_prob)[:, None] - s[None, :]
            ok = (h[None, :] == hp[:, None]) & (dist >= 0) & (dist % dil == 0)
            per_p.append(np.where(ok, -slopes[hp][:, None] * dist, -np.inf).astype(np.float32))
        tabs_n.append(np.stack(per_p))
    return tabs_c, tabs_n


def _attn_sample_body(gq_ref, gk_ref, q_ref, k_ref, v_ref, c0_ref, c1_ref, c2_ref,
                      bc0, bc1, bc2, bn0, bn1, bn2, o_ref, kn_ref, og_s, lse_s, *, t_new, groups, n_heads):
    caches = (c0_ref, c1_ref, c2_ref)
    bias_c = (bc0, bc1, bc2)
    bias_n = (bn0, bn1, bn2)
    kv_rows = 2 * n_heads
    probs = []
    for g, (window, dil) in enumerate(groups):
        c_ref = caches[g]
        n_prob, tok_per = _sample_problems(dil, t_new)
        qn = _rms(q_ref[0, g], gq_ref[g:g + 1, :])
        kn = _rms(k_ref[0, g], gk_ref[g:g + 1, :])
        kn_ref[0, g] = kn
        knb = kn.astype(BF16)
        vnb = v_ref[0, g].astype(BF16)
        for p in range(n_prob):
            toks = [p + i * n_prob for i in range(tok_per)]
            parts = [qn[t * n_heads:(t + 1) * n_heads] for t in toks]
            qp = (parts[0] if tok_per == 1 else jnp.concatenate(parts, axis=0)).astype(BF16)
            kc = c_ref[:, p * kv_rows:p * kv_rows + n_heads, :].reshape(BAND * n_heads, HEAD_DIM).astype(BF16)
            lc = _dot_nt(qp, kc) * ATTN_SCALE + bias_c[g][...]
            ln = _dot_nt(qp, knb) * ATTN_SCALE + bias_n[g][p]
            probs.append(dict(g=g, p=p, toks=toks, lc=lc, ln=ln, vnb=vnb))
    for pr in probs:
        pr["mx"] = jnp.maximum(jnp.max(pr["lc"], axis=-1, keepdims=True), jnp.max(pr["ln"], axis=-1, keepdims=True))
    for pr in probs:
        pr["pc"] = jnp.exp(pr["lc"] - pr["mx"]).astype(BF16)
        pr["pn"] = jnp.exp(pr["ln"] - pr["mx"]).astype(BF16)
    ones_c = jnp.ones((BAND * n_heads, HEAD_DIM), BF16)
    ones_n = jnp.ones((t_new * n_heads, HEAD_DIM), BF16)
    for pr in probs:
        g, p = pr["g"], pr["p"]
        vc = caches[g][:, p * kv_rows + n_heads:(p + 1) * kv_rows, :].reshape(BAND * n_heads, HEAD_DIM).astype(BF16)
        oe = (_dot(pr["pc"], jnp.concatenate([vc, ones_c], axis=1))
              + _dot(pr["pn"], jnp.concatenate([pr["vnb"], ones_n], axis=1)))
        den = oe[:, HEAD_DIM:]
        o = oe[:, :HEAD_DIM] / den
        lse = pr["mx"] + jnp.log(den)
        for i, t in enumerate(pr["toks"]):
            og_s[g, t * n_heads:(t + 1) * n_heads, :] = o[i * n_heads:(i + 1) * n_heads]
            lse_s[g, t * n_heads:(t + 1) * n_heads, :] = lse[i * n_heads:(i + 1) * n_heads]
    l0, l1, l2 = lse_s[0], lse_s[1], lse_s[2]
    mx = jnp.maximum(jnp.maximum(l0, l1), l2)
    w0, w1, w2 = jnp.exp(l0 - mx), jnp.exp(l1 - mx), jnp.exp(l2 - mx)
    o_ref[0] = (w0 * og_s[0] + w1 * og_s[1] + w2 * og_s[2]) / (w0 + w1 + w2)


def _attn_sample(qs, ks, vs, caches, g_q, g_k, *, n_batch, t_new, groups, n_heads=A_HEADS):
    n_g = len(groups)
    rows = t_new * n_heads
    tabs_c, tabs_n = _sample_bias_tables(groups, t_new, n_heads)
    new_spec = lambda: pl.BlockSpec((1, n_g, rows, HEAD_DIM), lambda b: (b, 0, 0, 0))
    in_specs = [pl.BlockSpec((n_g, HEAD_DIM), lambda b: (0, 0)),
                pl.BlockSpec((n_g, HEAD_DIM), lambda b: (0, 0)),
                new_spec(), new_spec(), new_spec()]
    operands = [g_q, g_k, qs, ks, vs]
    for c, (window, dil) in zip(caches, groups):
        assert c.shape[1] == window and window == BAND * dil
        n_prob, _ = _sample_problems(dil, t_new)
        c3 = c.reshape(n_batch * BAND, dil * 2 * n_heads, HEAD_DIM)
        in_specs.append(pl.BlockSpec((BAND, n_prob * 2 * n_heads, HEAD_DIM), lambda b: (b, 0, 0)))
        operands.append(c3)
    for tab in tabs_c:
        in_specs.append(pl.BlockSpec(tab.shape, lambda b: (0, 0)))
        operands.append(jnp.asarray(tab))
    for tab in tabs_n:
        in_specs.append(pl.BlockSpec(tab.shape, lambda b: (0, 0, 0)))
        operands.append(jnp.asarray(tab))
    body = functools.partial(_attn_sample_body, t_new=t_new, groups=tuple(groups), n_heads=n_heads)
    return pl.pallas_call(
        body,
        grid=(n_batch,),
        in_specs=in_specs,
        out_specs=(pl.BlockSpec((1, rows, HEAD_DIM), lambda b: (b, 0, 0)), new_spec()),
        out_shape=(jax.ShapeDtypeStruct((n_batch, rows, HEAD_DIM), F32),
                   jax.ShapeDtypeStruct((n_batch, n_g, rows, HEAD_DIM), F32)),
        scratch_shapes=[pltpu.VMEM((n_g, rows, HEAD_DIM), F32), pltpu.VMEM((n_g, rows, HEAD_DIM), F32)],
        compiler_params=pltpu.CompilerParams(dimension_semantics=("arbitrary",),
                                             vmem_limit_bytes=VMEM_LIMIT),
        name="attn_sample",
    )(*operands)


def _split3(x):
    hi = x.astype(BF16)
    r1 = x - hi.astype(F32)
    mid = r1.astype(BF16)
    lo = (r1 - mid.astype(F32)).astype(BF16)
    return hi, mid, lo


def _hgrn_heads(hq, hf, hi, hog, lb, g_out, sts, *, tb, n_valid=None):
    c = HG_CHUNK
    nh = len(sts)
    sts = list(sts)
    q = hq * _sigmoid(hq) * (HG_KDIM ** -0.5)
    gate = lb + (1.0 - lb) * _sigmoid(hf)
    log_g = jnp.log(gate)
    k = (1.0 - lb) * _sigmoid(-hf)
    if n_valid is not None:
        rows = lax.broadcasted_iota(jnp.int32, hq.shape, 0)
        log_g = jnp.where(rows < n_valid, log_g, 0.0)
        k = jnp.where(rows < n_valid, k, 0.0)
        q = jnp.where(rows < n_valid, q, 0.0)
    row = lax.broadcasted_iota(jnp.int32, (tb, tb), 0)
    col = lax.broadcasted_iota(jnp.int32, (tb, tb), 1)
    same = (row // c) == (col // c)
    tri_ok = same & (col <= row)
    tri = jnp.where(tri_ok, 1.0, 0.0).astype(BF16)
    blk = jnp.where(same, 1.0, 0.0).astype(BF16)
    p_hi, p_mid, p_lo = _split3(log_g)
    b = _dot(tri, p_hi) + _dot(tri, p_mid) + _dot(tri, p_lo)
    b_last = _dot(blk, p_hi) + _dot(blk, p_mid) + _dot(blk, p_lo)
    q_dec = (q * jnp.exp(b)).astype(BF16)
    k_dec = (k * jnp.exp(-b)).astype(BF16)
    k_end = (k * jnp.exp(b_last - b)).astype(BF16)
    decay = jnp.exp(b_last)
    vb = hi.astype(BF16)
    head = lambda x, h: x[:, h * HG_KDIM:(h + 1) * HG_KDIM]
    o_intra = []
    for h in range(nh):
        a = jnp.where(tri_ok, _dot_nt(head(q_dec, h), head(k_dec, h)), 0.0)
        o_intra.append(_dot(a.astype(BF16), head(vb, h)))
    parts = [[] for _ in range(nh)]
    for ci in range(tb // c):
        sl = slice(ci * c, (ci + 1) * c)
        for h in range(nh):
            parts[h].append(_dot_nt(head(q_dec, h)[sl], sts[h].astype(BF16)))
            sts[h] = sts[h] * head(decay, h)[ci * c:ci * c + 1, :] + _dot_tn(head(vb, h)[sl], head(k_end, h)[sl])
    outs = []
    for h in range(nh):
        o = o_intra[h] + (jnp.concatenate(parts[h], axis=0) if len(parts[h]) > 1 else parts[h][0])
        outs.append(_rms(o, g_out) * _sigmoid(head(hog, h)))
    return outs, sts


def _hgrn_prompt_body(hq_ref, hf_ref, hi_ref, hog_ref, lb_ref, go_ref, o_ref, s_ref, st_s, *, tb, nt, hb):
    t = pl.program_id(1)

    @pl.when(t == 0)
    def _():
        st_s[...] = jnp.zeros_like(st_s)

    outs, sts = _hgrn_heads(hq_ref[...], hf_ref[...], hi_ref[...], hog_ref[...], lb_ref[0], go_ref[...],
                            [st_s[hh] for hh in range(hb)], tb=tb)
    for hh in range(hb):
        o_ref[:, hh * HG_VDIM:(hh + 1) * HG_VDIM] = outs[hh].astype(o_ref.dtype)
        st_s[hh] = sts[hh]

    @pl.when(t == nt - 1)
    def _():
        for hh in range(hb):
            s_ref[hh] = sts[hh].T


def _hgrn_prompt(z, lb, g_out, *, seq, tb, hb, out_rows=None, n_heads=HG_HEADS, col_hq=COL_HQ, col_hf=COL_HF,
                 col_hi=COL_HI, col_hog=COL_HOG):
    out_rows = seq if out_rows is None else out_rows
    nt = seq // tb
    bw = hb * HG_KDIM
    assert all(c0 % bw == 0 for c0 in (col_hq, col_hf, col_hi, col_hog))
    blk = lambda c0: pl.BlockSpec((tb, bw), lambda h, t, c=c0 // bw: (t, c + h))
    body = functools.partial(_hgrn_prompt_body, tb=tb, nt=nt, hb=hb)
    return pl.pallas_call(
        body,
        grid=(n_heads // hb, nt),
        in_specs=[blk(col_hq), blk(col_hf), blk(col_hi), blk(col_hog),
                  pl.BlockSpec((1, 1, bw), lambda h, t: (h, 0, 0)),
                  pl.BlockSpec((1, HG_VDIM), lambda h, t: (0, 0))],
        out_specs=(pl.BlockSpec((tb, bw), lambda h, t: (t, h)),
                   pl.BlockSpec((hb, HG_KDIM, HG_VDIM), lambda h, t: (h, 0, 0))),
        out_shape=(jax.ShapeDtypeStruct((out_rows, n_heads * HG_VDIM), BF16),
                   jax.ShapeDtypeStruct((n_heads, HG_KDIM, HG_VDIM), F32)),
        scratch_shapes=[pltpu.VMEM((hb, HG_VDIM, HG_KDIM), F32)],
        compiler_params=pltpu.CompilerParams(dimension_semantics=("arbitrary", "arbitrary"),
                                             vmem_limit_bytes=VMEM_LIMIT),
        name="hgrn_prompt",
    )(z, z, z, z, lb.reshape(n_heads // hb, 1, bw), g_out.reshape(1, HG_VDIM))


def _hgrn_sample_body(hq_ref, hf_ref, hi_ref, hog_ref, lb_ref, go_ref, s0_ref, o_ref, s_ref, *, t_new, n_heads):
    pad = jnp.zeros((HG_CHUNK - t_new, n_heads * HG_KDIM), F32)
    ext = lambda ref: jnp.concatenate([ref[...], pad], axis=0)
    outs, sts = _hgrn_heads(ext(hq_ref), ext(hf_ref), ext(hi_ref), ext(hog_ref), lb_ref[...], go_ref[...],
                            [s0_ref[0, h].T for h in range(n_heads)], tb=HG_CHUNK, n_valid=t_new)
    for h in range(n_heads):
        o_ref[:, h * HG_VDIM:(h + 1) * HG_VDIM] = outs[h][:t_new]
        s_ref[0, h] = sts[h].T


def _hgrn_sample(hq, hf, hi, hog, lb, g_out, s0, *, n_batch, t_new, n_heads=HG_HEADS):
    width = n_heads * HG_KDIM
    row = lambda: pl.BlockSpec((t_new, width), lambda b: (b, 0))
    st = lambda: pl.BlockSpec((1, n_heads, HG_KDIM, HG_VDIM), lambda b: (b, 0, 0, 0))
    body = functools.partial(_hgrn_sample_body, t_new=t_new, n_heads=n_heads)
    return pl.pallas_call(
        body,
        grid=(n_batch,),
        in_specs=[row(), row(), row(), row(),
                  pl.BlockSpec((1, width), lambda b: (0, 0)),
                  pl.BlockSpec((1, HG_VDIM), lambda b: (0, 0)),
                  st()],
        out_specs=(row(), st()),
        out_shape=(jax.ShapeDtypeStruct((n_batch * t_new, width), F32),
                   jax.ShapeDtypeStruct(s0.shape, F32)),
        compiler_params=pltpu.CompilerParams(dimension_semantics=("arbitrary",),
                                             vmem_limit_bytes=VMEM_LIMIT),
        name="hgrn_sample",
    )(hq, hf, hi, hog, lb.reshape(1, width), g_out.reshape(1, HG_VDIM), s0)


def _tile_spec(bm, bn, col0=0):
    assert col0 % bn == 0
    cb = col0 // bn
    return pl.BlockSpec((bm, bn), lambda i, j: (i, cb + j))


def kernel(x_prompt, x_sample, cache_kv_w128, cache_kv_w512, cache_kv_w2048, state_hgrn, p_prompt, p_sample,
           g_mix, w_in, g_q, g_k, hg_lb_raw, g_hg_out, w_up_attn, w_up_hgrn, w_out, g_ffn, w_ff_up, w_ff_down,
           w_ple, w_ple_gate):
    bm, bn = 2816, 256
    bn_wide = 512
    bm_ple = 1408
    bm_res = 2816
    down_blocks = dict(bm=1408, bn=2048, bk=1024)
    xp, xs = x_prompt.reshape(SEQ, D_MODEL), x_sample.reshape(N_SAMPLE, D_MODEL)
    pp, ps = p_prompt.reshape(SEQ, PLE_DIM), p_sample.reshape(N_SAMPLE, PLE_DIM)
    lb = jnp.cumsum(jax.nn.softmax(hg_lb_raw.astype(F32), axis=0), axis=0)[0]
    rows = lambda width, dt: jax.ShapeDtypeStruct((M_ALL, width), dt)

    def plain(accs, i, j, rows_sl, a_refs, extra, outs):
        outs[0][rows_sl, :] = accs[0]

    n_mix = _rmsnorm_rows(xp, g_mix[0], x_tail=xs, name="norm_mix")
    z, = _matmul_rows([(n_mix, w_in[0])], bm=bm, bn=bn_wide, out_shapes=(rows(IN_WIDTH, F32),), epilogue=plain,
                      name="in_proj")

    o_attn, kn0, kn1, kn2 = _attn_prompt(z, g_q[0], g_k[0], seq=SEQ, sb=2048, out_rows=M_ALL,
                                         dils=tuple(d for _, d in DIL_GROUPS))
    zs = z[SEQ:]
    def by_group(col0):
        a = zs[:, col0:col0 + N_GROUPS * A_WIDTH].reshape(DEC_BATCH, DEC_SEQ, N_GROUPS, A_HEADS, HEAD_DIM)
        return a.transpose(0, 2, 1, 3, 4).reshape(DEC_BATCH, N_GROUPS, DEC_SEQ * A_HEADS, HEAD_DIM)

    caches = [c[0] for c in (cache_kv_w128, cache_kv_w512, cache_kv_w2048)]
    o_attn_s, kn_s = _attn_sample(by_group(COL_Q), by_group(COL_K), by_group(COL_V), caches,
                                  g_q[0], g_k[0], n_batch=DEC_BATCH, t_new=DEC_SEQ, groups=DIL_GROUPS)
    o_attn = lax.dynamic_update_slice(o_attn, o_attn_s.reshape(N_SAMPLE, A_WIDTH).astype(BF16), (SEQ, 0))

    o_hg, st_p = _hgrn_prompt(z, lb, g_hg_out[0], seq=SEQ, tb=256, hb=8, out_rows=M_ALL)
    o_hg_s, st_s = _hgrn_sample(zs[:, COL_HQ:COL_HF], zs[:, COL_HF:COL_HI], zs[:, COL_HI:COL_HOG],
                                zs[:, COL_HOG:COL_GA], lb, g_hg_out[0], state_hgrn[0],
                                n_batch=DEC_BATCH, t_new=DEC_SEQ)
    o_hg = lax.dynamic_update_slice(o_hg, o_hg_s.astype(BF16), (SEQ, 0))

    def merge_epilogue(accs, i, j, rows_sl, a_refs, extra, outs):
        ga, gb = extra[0][rows_sl, :], extra[1][rows_sl, :]
        outs[0][rows_sl, :] = (_sigmoid(ga) * accs[0] + _sigmoid(gb) * accs[1]).astype(BF16)

    def out_proj_epilogue(accs, i, j, rows_sl, a_refs, extra, outs):
        xp_ref, xs_ref, g_ref = extra
        x1_ref, x1g_ref, ssq_ref = outs

        def emit(rows, x1, ssq_before):
            x1_ref[rows, :] = x1
            x1g_ref[rows, :] = (x1 * g_ref[...]).astype(BF16)
            ssq_ref[rows, :] = ssq_before + jnp.sum(x1 * x1, axis=-1, keepdims=True)

        ssq_before = ssq_ref[rows_sl, :]
        emit(rows_sl, xp_ref[rows_sl, :] + accs[0], ssq_before)
        if rows_sl.stop == bm_res:
            t0 = bm_res - N_SAMPLE - rows_sl.start

            @pl.when(i == pl.num_programs(0) - 1)
            def _():
                emit(slice(bm_res - N_SAMPLE, bm_res), xs_ref[...] + accs[0][t0:, :], ssq_before[t0:, :])

    merged, = _matmul_rows([(o_attn, w_up_attn[0]), (o_hg, w_up_hgrn[0])], bm=bm, bn=bn, extras=(z, z),
                           extra_specs=(_tile_spec(bm, bn, COL_GA), _tile_spec(bm, bn, COL_GB)),
                           out_shapes=(rows(D_MODEL, BF16),), epilogue=merge_epilogue, row_chunk=352,
                           name="up_merge")
    def zero_ssq(outs):
        outs[2][...] = jnp.zeros(outs[2].shape, F32)

    x1, x1g, ssq = _matmul_rows(
        [(merged, w_out[0])], bm=bm_res, bn=bn, extras=(xp, xs, g_ffn),
        extra_specs=(_tile_spec(bm_res, bn), pl.BlockSpec((N_SAMPLE, bn), lambda i, j: (0, j)),
                     pl.BlockSpec((1, bn), lambda i, j: (0, j))),
        out_shapes=(rows(D_MODEL, F32), rows(D_MODEL, BF16), rows(LANES, F32)),
        out_specs=(_tile_spec(bm_res, bn), _tile_spec(bm_res, bn),
                   pl.BlockSpec((bm_res, LANES), lambda i, j: (i, 0))),
        epilogue=out_proj_epilogue, first_col_init=zero_ssq, name="out_proj")

    def ffn_up_epilogue(accs, i, j, rows_sl, a_refs, extra, outs):
        inv_rms = lax.rsqrt(extra[0][rows_sl, 0:1] * (1.0 / D_MODEL) + NORM_EPS)
        outs[0][rows_sl, :] = jnp.square(jnp.maximum(accs[0] * inv_rms, 0.0)).astype(BF16)

    hid, = _matmul_rows([(x1g, w_ff_up[0])], bm=bm, bn=bn_wide, extras=(ssq,),
                        extra_specs=(pl.BlockSpec((bm, LANES), lambda i, j: (i, 0)),),
                        out_shapes=(rows(FFN_HIDDEN, BF16),), epilogue=ffn_up_epilogue, name="ffn_up")
    x2 = _matmul_ksplit(hid, w_ff_down[0], x1, name="ffn_down", **down_blocks)

    n_ple_blocks = M_ALL // bm_ple

    def ple_epilogue(accs, i, j, rows_sl, a_refs, extra, outs):
        pp_ref, ps_ref, wp_ref = extra
        yp_ref, ys_ref = outs
        x2_tile = a_refs[0][rows_sl, pl.ds(pl.multiple_of(j * bn, bn), bn)]
        gate = _sigmoid(accs[0])
        wp = wp_ref[...].astype(BF16)
        yp_ref[rows_sl, :] = x2_tile + gate * _dot(pp_ref[rows_sl, :].astype(BF16), wp)
        if rows_sl.stop == bm_ple:
            s0 = bm_ple - N_SAMPLE - rows_sl.start

            @pl.when(i == n_ple_blocks - 1)
            def _():
                ys_ref[...] = x2_tile[s0:, :] + gate[s0:, :] * _dot(ps_ref[...].astype(BF16), wp)

    y_p, y_s = _matmul_rows(
        [(x2, w_ple_gate[0])], bm=bm_ple, bn=bn, extras=(pp, ps, w_ple[0]),
        extra_specs=(pl.BlockSpec((bm_ple, PLE_DIM), lambda i, j: (i, 0)),
                     pl.BlockSpec((N_SAMPLE, PLE_DIM), lambda i, j: (0, 0)),
                     pl.BlockSpec((PLE_DIM, bn), lambda i, j: (0, j))),
        out_shapes=(jax.ShapeDtypeStruct((SEQ, D_MODEL), F32), jax.ShapeDtypeStruct((N_SAMPLE, D_MODEL), F32)),
        out_specs=(pl.BlockSpec((bm_ple, bn), lambda i, j: (i, j)),
                   pl.BlockSpec((N_SAMPLE, bn), lambda i, j: (0, jnp.where(i == n_ple_blocks - 1, j, 0)))),
        epilogue=ple_epilogue, name="ple")

    y_prompt = y_p.reshape(1, SEQ, D_MODEL)
    y_sample = y_s.reshape(DEC_BATCH, DEC_SEQ, D_MODEL)
    kv_p, kv_s = [], []
    for g, (window, _) in enumerate(DIL_GROUPS):
        length = min(window, SEQ)
        kn_g = (kn0, kn1, kn2)[g][SEQ - length:].reshape(length, A_HEADS, HEAD_DIM)
        v_g = z[SEQ - length:SEQ, COL_V + g * A_WIDTH:COL_V + (g + 1) * A_WIDTH].reshape(length, A_HEADS, HEAD_DIM)
        kv_p.append(jnp.stack([kn_g, v_g], axis=1)[None, None])
        ks_g = kn_s[:, g].reshape(DEC_BATCH, DEC_SEQ, A_HEADS, HEAD_DIM)
        vs_g = zs[:, COL_V + g * A_WIDTH:COL_V + (g + 1) * A_WIDTH].reshape(DEC_BATCH, DEC_SEQ, A_HEADS, HEAD_DIM)
        kv_s.append(jnp.stack([ks_g, vs_g], axis=2)[None])
    return (y_prompt, y_sample, kv_p[0], kv_p[1], kv_p[2], st_p[None, None],
            kv_s[0], kv_s[1], kv_s[2], st_s[None])
```
